```python
import math
import jax, jax.numpy as jnp
from jax import lax
import numpy as np

D_MODEL = 1024
BATCH = 8
SEQ = 2048
DEPTH = 2
DEC_BATCH = 128
DEC_SEQ = 8
PAST_LEN = 16384
PAGE_SIZE = 128

H_RET = 4
DK_RET = 128
DV_RET = 128
RET_W = H_RET * DV_RET
ROPE_BASE = 10000.0
H_SSD = 16
P_SSD = 64
G_SSD = 2
N_SSD = 64
R_SSD = H_SSD // G_SSD
SSD_W = H_SSD * P_SSD
CONV_K = 4
CONV_DIM = SSD_W + 2 * G_SSD * N_SSD
CHUNK = 128
N_EGROUPS = 4
EXP_PER_GROUP = 4
N_EXPERTS = N_EGROUPS * EXP_PER_GROUP
TOP_K = 2
D_FF = 512
EPS = 1e-6
IN_SIZES = [RET_W, RET_W, RET_W, RET_W, SSD_W, CONV_DIM, H_SSD, D_MODEL, D_MODEL]
IN_COLS = sum(IN_SIZES)

kernel_name = 'retnet_mamba2_hmoe_step'


def rmsnorm(x, g):
    xf = x.astype(jnp.float32)
    r = lax.rsqrt(jnp.mean(xf * xf, axis=-1, keepdims=True) + EPS)
    return (xf * r * g.astype(jnp.float32)).astype(x.dtype)


def rotary(x, pos):
    half = x.shape[-1] // 2
    inv = ROPE_BASE ** (-jnp.arange(half, dtype=jnp.float32) / half)
    ang = pos[:, None] * inv[None, :]
    c = jnp.cos(ang)[None, :, None, :]
    s = jnp.sin(ang)[None, :, None, :]
    x1, x2 = x[..., :half], x[..., half:]
    return jnp.concatenate([x1 * c - x2 * s, x2 * c + x1 * s], axis=-1)


def to_chunks(t, c):
    b, l = t.shape[:2]
    return jnp.moveaxis(t.reshape((b, l // c, c) + t.shape[2:]), 1, 0)


def from_chunks(t):
    nc, b, c = t.shape[:3]
    return jnp.moveaxis(t, 0, 1).reshape((b, nc * c) + t.shape[3:])


def retention(q, k, v, s0):
    l = q.shape[1]
    c = math.gcd(l, CHUNK)
    log_g = jnp.log1p(-jnp.exp2(-5.0 - jnp.arange(H_RET, dtype=jnp.float32)))
    idx = jnp.arange(c, dtype=jnp.float32)
    rel = idx[:, None] - idx[None, :]
    causal = rel >= 0
    dmat = jnp.where(causal[None], jnp.exp(jnp.where(causal, rel, 0.0)[None] * log_g[:, None, None]), 0.0)
    q_dec = jnp.exp((idx[:, None] + 1.0) * log_g[None, :])[None, :, :, None]
    k_dec = jnp.exp((c - 1.0 - idx[:, None]) * log_g[None, :])[None, :, :, None]
    chunk_dec = jnp.exp(c * log_g)[None, :, None, None]

    def step(s, inp):
        qc, kc, vc = inp
        sc = jnp.einsum('bnhd,bmhd->bhnm', qc, kc) * dmat
        o = jnp.einsum('bhnm,bmhv->bnhv', sc, vc) + jnp.einsum('bnhd,bhdv->bnhv', qc, s) * q_dec
        s = chunk_dec * s + jnp.einsum('bmhd,bmhv->bhdv', kc * k_dec, vc)
        return s, o

    s, o = lax.scan(step, s0, (to_chunks(q, c), to_chunks(k, c), to_chunks(v, c)))
    return from_chunks(o), s


def ssd(x, dt, bm, cm, a, s0):
    l = x.shape[1]
    c = math.gcd(l, CHUNK)
    mask = jnp.tril(jnp.ones((c, c), dtype=bool))[None, :, :, None, None]

    def step(s, inp):
        xc, dtc, bc, cc = inp
        cum = jnp.cumsum(dtc * a, axis=1)
        seg = cum[:, :, None] - cum[:, None, :]
        lmat = jnp.exp(jnp.where(mask, seg, -jnp.inf))
        cb = jnp.einsum('bngk,bmgk->bnmg', cc, bc)
        w = lmat * cb[..., None] * dtc[:, None]
        y = jnp.einsum('bnmgr,bmgrp->bngrp', w, xc)
        y = y + jnp.exp(cum)[..., None] * jnp.einsum('bngk,bgrpk->bngrp', cc, s)
        wend = jnp.exp(cum[:, -1:] - cum) * dtc
        s = jnp.exp(cum[:, -1])[..., None, None] * s + jnp.einsum('bmgr,bmgrp,bmgk->bgrpk', wend, xc, bc)
        return s, y

    s, y = lax.scan(step, s0, (to_chunks(x, c), to_chunks(dt, c), to_chunks(bm, c), to_chunks(cm, c)))
    return from_chunks(y), s


def hier_moe(h, w_rg, w_re, w_gate, w_up, w_down):
    b, l, d = h.shape
    t = h.reshape(-1, d)
    n_tok = t.shape[0]
    glog = (t @ w_rg).astype(jnp.float32)
    gprob = jax.nn.softmax(glog, axis=-1)
    gidx = jnp.argmax(glog, axis=-1)
    gw = jnp.take_along_axis(gprob, gidx[:, None], axis=-1)[:, 0]
    elog = (t @ w_re).astype(jnp.float32).reshape(n_tok, N_EGROUPS, EXP_PER_GROUP)
    sel = jnp.take_along_axis(elog, gidx[:, None, None], axis=1)[:, 0]
    top_v, top_i = lax.top_k(sel, TOP_K)
    wts = jax.nn.softmax(top_v, axis=-1) * gw[:, None]
    flat_e = (gidx[:, None] * EXP_PER_GROUP + top_i).reshape(-1)
    flat_w = wts.reshape(-1)
    order = jnp.argsort(flat_e)
    tok = order // TOP_K
    gs = jnp.bincount(flat_e, length=N_EXPERTS).astype(jnp.int32)
    xs = t[tok]
    act = jax.nn.silu(lax.ragged_dot(xs, w_gate, gs)) * lax.ragged_dot(xs, w_up, gs)
    out = lax.ragged_dot(act, w_down, gs) * flat_w[order][:, None].astype(t.dtype)
    y = jnp.zeros_like(t).at[tok].add(out)
    return y.reshape(b, l, d)


def layer(x, pos, s_ret, s_ssm, s_conv, ln1_g, w_in, ret_gn_g, w_ret_out, conv_w, conv_b,
          dt_bias, a_log, d_skip, ssd_norm_g, w_ssd_out, w_o, ln2_g, w_rg, w_re, w_gate, w_up, w_down):
    b, l, _ = x.shape
    f32 = jnp.float32
    h = rmsnorm(x, ln1_g)
    proj = h @ w_in
    splits = [int(v) for v in np.cumsum(IN_SIZES)[:-1]]
    q, k, v, rg, z, xbc, dt_raw, g_ret, g_ssd = jnp.split(proj, splits, axis=-1)

    q = rotary(q.reshape(b, l, H_RET, DK_RET).astype(f32), pos)
    k = rotary(k.reshape(b, l, H_RET, DK_RET).astype(f32), pos) * (DK_RET ** -0.5)
    v = v.reshape(b, l, H_RET, DV_RET).astype(f32)
    o, s_ret_new = retention(q, k, v, s_ret.astype(f32))
    mu = jnp.mean(o, axis=-1, keepdims=True)
    var = jnp.mean(jnp.square(o - mu), axis=-1, keepdims=True)
    on = ((o - mu) * lax.rsqrt(var + EPS)).reshape(b, l, RET_W) * ret_gn_g.astype(f32)
    ret_y = (jax.nn.silu(rg.astype(f32)) * on).astype(x.dtype) @ w_ret_out

    xpad = jnp.concatenate([s_conv.astype(xbc.dtype), xbc], axis=1)
    conv = conv_b.astype(f32) + sum(xpad[:, j:j + l].astype(f32) * conv_w[j].astype(f32) for j in range(CONV_K))
    conv = jax.nn.silu(conv)
    new_conv = xpad[:, xpad.shape[1] - (CONV_K - 1):]
    xs, bm, cm = jnp.split(conv, [SSD_W, SSD_W + G_SSD * N_SSD], axis=-1)
    xs = xs.reshape(b, l, G_SSD, R_SSD, P_SSD)
    dt = jax.nn.softplus(dt_raw.astype(f32) + dt_bias.astype(f32)).reshape(b, l, G_SSD, R_SSD)
    a = -jnp.exp(a_log.astype(f32)).reshape(G_SSD, R_SSD)
    y, s_ssm_new = ssd(xs, dt, bm.reshape(b, l, G_SSD, N_SSD), cm.reshape(b, l, G_SSD, N_SSD), a,
                       s_ssm.astype(f32).reshape(b, G_SSD, R_SSD, P_SSD, N_SSD))
    y = y + d_skip.astype(f32).reshape(G_SSD, R_SSD)[..., None] * xs
    y = (y.reshape(b, l, SSD_W) * jax.nn.silu(z.astype(f32))).reshape(b, l, G_SSD, SSD_W // G_SSD)
    y = y * lax.rsqrt(jnp.mean(y * y, axis=-1, keepdims=True) + EPS)
    y = y.reshape(b, l, SSD_W) * ssd_norm_g.astype(f32)
    ssd_y = y.astype(x.dtype) @ w_ssd_out

    m = jax.nn.sigmoid(g_ret) * ret_y + jax.nn.sigmoid(g_ssd) * ssd_y
    x = x + m @ w_o
    x = x + hier_moe(rmsnorm(x, ln2_g), w_rg, w_re, w_gate, w_up, w_down)
    return (x, s_ret_new.astype(x.dtype),
            s_ssm_new.reshape(b, H_SSD, P_SSD, N_SSD).astype(x.dtype), new_conv.astype(x.dtype))


def trunk(x, start, s_ret, s_ssm, s_conv, ln1_g, w_in, ret_gn_g, w_ret_out, conv_w, conv_b,
          dt_bias, a_log, d_skip, ssd_norm_g, w_ssd_out, w_o, ln2_g, w_rg, w_re, w_gate, w_up,
          w_down, lnf_g):
    pos = start + jnp.arange(x.shape[1], dtype=jnp.float32)
    rets, ssms, convs = [], [], []
    for i in range(DEPTH):
        x, r, s, c = layer(x, pos, s_ret[i], s_ssm[i], s_conv[i], ln1_g[i], w_in[i], ret_gn_g[i],
                           w_ret_out[i], conv_w[i], conv_b[i], dt_bias[i], a_log[i], d_skip[i],
                           ssd_norm_g[i], w_ssd_out[i], w_o[i], ln2_g[i], w_rg[i], w_re[i],
                           w_gate[i], w_up[i], w_down[i])
        rets.append(r)
        ssms.append(s)
        convs.append(c)
    return rmsnorm(x, lnf_g), jnp.stack(rets), jnp.stack(ssms), jnp.stack(convs)


def setup_inputs(seed: int = 0) -> dict:
    key = jax.random.key(seed)
    ks = jax.random.split(key, 26)
    nrm = jax.random.normal
    f32 = jnp.float32
    dt0 = jnp.exp(jax.random.uniform(ks[10], (DEPTH, H_SSD), f32, math.log(1e-3), math.log(1e-1)))
    return {
        'x_prompt': nrm(ks[0], (BATCH, SEQ, D_MODEL), f32),
        'x_sample': nrm(ks[1], (DEC_BATCH, DEC_SEQ, D_MODEL), f32),
        'state_ret': 0.1 * nrm(ks[2], (DEPTH, DEC_BATCH, H_RET, DK_RET, DV_RET), f32),
        'state_ssm': 0.1 * nrm(ks[3], (DEPTH, DEC_BATCH, H_SSD, P_SSD, N_SSD), f32),
        'state_conv': nrm(ks[4], (DEPTH, DEC_BATCH, CONV_K - 1, CONV_DIM), f32),
        'ln1_g': 1.0 + 0.02 * nrm(ks[5], (DEPTH, D_MODEL), f32),
        'w_in': nrm(ks[6], (DEPTH, D_MODEL, IN_COLS), f32) * D_MODEL ** -0.5,
        'ret_gn_g': 1.0 + 0.02 * nrm(ks[7], (DEPTH, RET_W), f32),
        'w_ret_out': nrm(ks[8], (DEPTH, RET_W, D_MODEL), f32) * RET_W ** -0.5,
        'conv_w': nrm(ks[9], (DEPTH, CONV_K, CONV_DIM), f32) * CONV_K ** -0.5,
        'conv_b': 0.02 * nrm(ks[11], (DEPTH, CONV_DIM), f32),
        'dt_bias': dt0 + jnp.log(-jnp.expm1(-dt0)),
        'a_log': jnp.log(jax.random.uniform(ks[12], (DEPTH, H_SSD), f32, 1.0, 16.0)),
        'd_skip': 1.0 + 0.02 * nrm(ks[13], (DEPTH, H_SSD), f32),
        'ssd_norm_g': 1.0 + 0.02 * nrm(ks[14], (DEPTH, SSD_W), f32),
        'w_ssd_out': nrm(ks[15], (DEPTH, SSD_W, D_MODEL), f32) * SSD_W ** -0.5,
        'w_o': nrm(ks[16], (DEPTH, D_MODEL, D_MODEL), f32) * D_MODEL ** -0.5,
        'ln2_g': 1.0 + 0.02 * nrm(ks[17], (DEPTH, D_MODEL), f32),
        'w_router_group': nrm(ks[18], (DEPTH, D_MODEL, N_EGROUPS), f32) * D_MODEL ** -0.5,
        'w_router_expert': nrm(ks[19], (DEPTH, D_MODEL, N_EXPERTS), f32) * D_MODEL ** -0.5,
        'w_e_gate': nrm(ks[20], (DEPTH, N_EXPERTS, D_MODEL, D_FF), f32) * D_MODEL ** -0.5,
        'w_e_up': nrm(ks[21], (DEPTH, N_EXPERTS, D_MODEL, D_FF), f32) * D_MODEL ** -0.5,
        'w_e_down': nrm(ks[22], (DEPTH, N_EXPERTS, D_FF, D_MODEL), f32) * D_FF ** -0.5,
        'lnf_g': 1.0 + 0.02 * nrm(ks[23], (D_MODEL,), f32),
    }


def reference(x_prompt, x_sample, state_ret, state_ssm, state_conv, ln1_g, w_in, ret_gn_g, w_ret_out,
              conv_w, conv_b, dt_bias, a_log, d_skip, ssd_norm_g, w_ssd_out, w_o, ln2_g,
              w_router_group, w_router_expert, w_e_gate, w_e_up, w_e_down, lnf_g):
    dt = x_prompt.dtype
    z_ret = jnp.zeros((DEPTH, BATCH, H_RET, DK_RET, DV_RET), dt)
    z_ssm = jnp.zeros((DEPTH, BATCH, H_SSD, P_SSD, N_SSD), dt)
    z_conv = jnp.zeros((DEPTH, BATCH, CONV_K - 1, CONV_DIM), dt)
    y_prompt, p_ret, p_ssm, p_conv = trunk(
        x_prompt, 0.0, z_ret, z_ssm, z_conv, ln1_g, w_in, ret_gn_g, w_ret_out, conv_w, conv_b,
        dt_bias, a_log, d_skip, ssd_norm_g, w_ssd_out, w_o, ln2_g, w_router_group, w_router_expert,
        w_e_gate, w_e_up, w_e_down, lnf_g)
    y_sample, s_ret, s_ssm, s_conv = trunk(
        x_sample, float(PAST_LEN), state_ret, state_ssm, state_conv, ln1_g, w_in, ret_gn_g, w_ret_out,
        conv_w, conv_b, dt_bias, a_log, d_skip, ssd_norm_g, w_ssd_out, w_o, ln2_g, w_router_group,
        w_router_expert, w_e_gate, w_e_up, w_e_down, lnf_g)
    return (y_prompt, y_sample, p_ret, p_ssm, p_conv, s_ret, s_ssm, s_conv)
```

```python
import functools
import math

import jax
import jax.numpy as jnp
from jax import lax
from jax.experimental import pallas as pl
from jax.experimental.pallas import tpu as pltpu

F32 = jnp.float32
BF16 = jnp.bfloat16

D_MODEL = 1024
H_RET, DK_RET, DV_RET = 4, 128, 128
RET_W = H_RET * DV_RET
ROPE_BASE = 10000.0
H_SSD, P_SSD, G_SSD, N_SSD = 16, 64, 2, 64
R_SSD = H_SSD // G_SSD
SSD_W = H_SSD * P_SSD
GRP_W = SSD_W // G_SSD
CONV_K = 4
CONV_DIM = SSD_W + 2 * G_SSD * N_SSD
CHUNK = 128
N_EGROUPS, EXP_PER_GROUP = 4, 4
N_EXPERTS = N_EGROUPS * EXP_PER_GROUP
N_PAIRS = EXP_PER_GROUP * (EXP_PER_GROUP - 1) // 2
N_CLASSES = N_EGROUPS * N_PAIRS
D_FF = 512
EPS = 1e-6
IN_SIZES = [RET_W, RET_W, RET_W, RET_W, SSD_W, CONV_DIM, H_SSD, D_MODEL, D_MODEL]

LANE = 128
SUBLANE = 8
MIB = 1024 * 1024

ROW_TILE = 256
META_W = LANE
HW_W = D_MODEL + META_W
SAMPLE_SEQS_PER_STEP = 8


def _vmem_limit(block_bytes):
    return int(min(2 * block_bytes + 16 * MIB, 56 * MIB))


def _params(semantics, block_bytes):
    return pltpu.CompilerParams(dimension_semantics=semantics,
                                vmem_limit_bytes=_vmem_limit(block_bytes))


def _nbytes(shape, dtype):
    return math.prod(shape) * jnp.dtype(dtype).itemsize


def _full(shape):
    return pl.BlockSpec(shape, lambda *_: (0,) * len(shape))


def _dot(a, b):
    return jnp.dot(a, b, preferred_element_type=F32)


def _dot_nt(a, b):
    return lax.dot_general(a, b, (((1,), (1,)), ((), ())), preferred_element_type=F32)


def _dot_tn(a, b):
    return lax.dot_general(a, b, (((0,), (0,)), ((), ())), preferred_element_type=F32)


def _split3(a):
    hi = a.astype(BF16)
    r1 = a - hi.astype(F32)
    mid = r1.astype(BF16)
    lo = (r1 - mid.astype(F32)).astype(BF16)
    return hi, mid, lo


def _rmsnorm(x, g):
    r = lax.rsqrt(jnp.mean(x * x, axis=-1, keepdims=True) + EPS)
    return x * r * g


def _inproj_body(has_add, *refs):
    n_in = 8 if has_add else 7
    ins, outs = refs[:n_in], refs[n_in:]
    if has_add:
        x_ref, y_ref, g_ref = ins[:3]
        x = x_ref[...] + y_ref[...]
        outs[0][...] = x
        outs = outs[1:]
    else:
        x_ref, g_ref = ins[:2]
        x = x_ref[...]
    w_refs = ins[-5:]
    h = _rmsnorm(x, g_ref[...]).astype(BF16)
    for w_ref, o_ref in zip(w_refs, outs, strict=True):
        o_ref[...] = _dot(h, w_ref[...])


def _inproj(x, y, g, ws):
    t = x.shape[0]
    has_add = y is not None
    widths = [w.shape[1] for w in ws]
    row = lambda w: pl.BlockSpec((ROW_TILE, w), lambda i: (i, 0))
    in_specs = [row(D_MODEL)] * (2 if has_add else 1) + [_full((1, D_MODEL))]
    in_specs += [_full(w.shape) for w in ws]
    out_shape = [jax.ShapeDtypeStruct((t, w), F32) for w in widths]
    out_specs = [row(w) for w in widths]
    if has_add:
        out_shape = [jax.ShapeDtypeStruct((t, D_MODEL), F32)] + out_shape
        out_specs = [row(D_MODEL)] + out_specs
    block_bytes = (sum(_nbytes(w.shape, BF16) for w in ws)
                   + _nbytes((ROW_TILE, sum(widths) + 3 * D_MODEL), F32))
    args = ([x, y] if has_add else [x]) + [g] + list(ws)
    return pl.pallas_call(
        functools.partial(_inproj_body, has_add),
        grid=(t // ROW_TILE,), in_specs=in_specs, out_specs=out_specs, out_shape=out_shape,
        compiler_params=_params(("arbitrary",), block_bytes), name="inproj",
    )(*args)


def _retention_body(c, nb, nc, has_state, *refs):
    refs = list(refs)
    qkvr_ref, cos_ref, sin_ref, dmat_ref, qdec_ref, kdec_ref, cdec_ref, gn_ref = refs[:8]
    refs = refs[8:]
    if has_state:
        s0_ref = refs.pop(0)
    o_ref, ns_ref, st_scr = refs
    ci = pl.program_id(1)

    @pl.when(ci == 0)
    def _():
        if has_state:
            st_scr[...] = s0_ref[...]
        else:
            st_scr[...] = jnp.zeros(st_scr.shape, F32)

    for s in range(nb):
        rows = slice(s * c, (s + 1) * c)
        cosv, sinv = cos_ref[rows, :], sin_ref[rows, :]
        for h in range(H_RET):
            col = lambda j: slice(j * RET_W + h * DK_RET, j * RET_W + (h + 1) * DK_RET)
            q, k = qkvr_ref[rows, col(0)], qkvr_ref[rows, col(1)]
            v, rg = qkvr_ref[rows, col(2)], qkvr_ref[rows, col(3)]
            qr = q * cosv + pltpu.roll(q, DK_RET // 2, axis=1) * sinv
            kr = (k * cosv + pltpu.roll(k, DK_RET // 2, axis=1) * sinv) * (DK_RET ** -0.5)
            qb, kb, vb = qr.astype(BF16), kr.astype(BF16), v.astype(BF16)
            state = st_scr[s, h]
            sc = _dot_nt(qb, kb) * dmat_ref[h]
            o = _dot(sc.astype(BF16), vb) + _dot(qb, state.astype(BF16)) * qdec_ref[h]
            st_scr[s, h] = cdec_ref[h] * state + _dot_tn((kr * kdec_ref[h]).astype(BF16), vb)
            mu = jnp.mean(o, axis=-1, keepdims=True)
            d = o - mu
            var = jnp.mean(d * d, axis=-1, keepdims=True)
            on = d * lax.rsqrt(var + EPS) * gn_ref[:, h * DV_RET:(h + 1) * DV_RET]
            o_ref[rows, h * DV_RET:(h + 1) * DV_RET] = (jax.nn.silu(rg) * on).astype(o_ref.dtype)

    @pl.when(ci == nc - 1)
    def _():
        ns_ref[...] = st_scr[...]


def _retention(qkvr, start, gn_g, s0, batch, length):
    c = math.gcd(length, CHUNK)
    nc = length // c
    nb = 1 if nc > 1 else SAMPLE_SEQS_PER_STEP
    assert batch % nb == 0
    has_state = s0 is not None
    t = batch * length
    half = DK_RET // 2
    pos = start + jnp.arange(length, dtype=F32)
    inv = ROPE_BASE ** (-jnp.arange(half, dtype=F32) / half)
    ang = pos[:, None] * inv[None, :]
    cosv = jnp.tile(jnp.concatenate([jnp.cos(ang), jnp.cos(ang)], axis=1), (nb, 1))
    sinv = jnp.tile(jnp.concatenate([-jnp.sin(ang), jnp.sin(ang)], axis=1), (nb, 1))
    log_g = jnp.log1p(-jnp.exp2(-5.0 - jnp.arange(H_RET, dtype=F32)))
    idx = jnp.arange(c, dtype=F32)
    rel = idx[:, None] - idx[None, :]
    causal = rel >= 0
    dmat = jnp.where(causal[None], jnp.exp(jnp.where(causal, rel, 0.0)[None] * log_g[:, None, None]), 0.0)
    qdec = jnp.broadcast_to(jnp.exp((idx[None, :] + 1.0) * log_g[:, None])[:, :, None], (H_RET, c, LANE))
    kdec = jnp.broadcast_to(jnp.exp((c - 1.0 - idx[None, :]) * log_g[:, None])[:, :, None], (H_RET, c, LANE))
    cdec = jnp.exp(c * log_g)

    rows = nb * c
    state_spec = pl.BlockSpec((nb, H_RET, DK_RET, DV_RET), lambda b, ci: (b, 0, 0, 0))
    in_specs = [
        pl.BlockSpec((rows, 4 * RET_W), lambda b, ci: (b * nc + ci, 0)),
        pl.BlockSpec((rows, LANE), lambda b, ci: (ci, 0)),
        pl.BlockSpec((rows, LANE), lambda b, ci: (ci, 0)),
        _full((H_RET, c, c)), _full((H_RET, c, LANE)), _full((H_RET, c, LANE)),
        pl.BlockSpec(memory_space=pltpu.SMEM),
        _full((1, RET_W)),
    ]
    args = [qkvr, cosv, sinv, dmat, qdec, kdec, cdec, gn_g]
    if has_state:
        in_specs.append(state_spec)
        args.append(s0)
    block_bytes = (_nbytes((rows, 4 * RET_W + 2 * LANE + RET_W), F32)
                   + 3 * _nbytes((nb, H_RET, DK_RET, DV_RET), F32) + 3 * _nbytes((H_RET, c, LANE), F32))
    return pl.pallas_call(
        functools.partial(_retention_body, c, nb, nc, has_state),
        grid=(batch // nb, nc), in_specs=in_specs,
        out_specs=[pl.BlockSpec((rows, RET_W), lambda b, ci: (b * nc + ci, 0)), state_spec],
        out_shape=[jax.ShapeDtypeStruct((t, RET_W), BF16),
                   jax.ShapeDtypeStruct((batch, H_RET, DK_RET, DV_RET), F32)],
        scratch_shapes=[pltpu.VMEM((nb, H_RET, DK_RET, DV_RET), F32)],
        compiler_params=_params(("arbitrary", "arbitrary"), block_bytes), name="retention",
    )(*args)


def _ssd_body(c, nb, nc, has_state, *refs):
    refs = list(refs)
    (xbc_ref, z_ref, dt_ref, cw_ref, cb_ref, dtb_ref, a_ref, dsk_ref, ng_ref,
     tri_ref, eye_ref, exp_ref, sel_ref) = refs[:13]
    refs = refs[13:]
    if has_state:
        sconv_ref, sssm_ref = refs.pop(0), refs.pop(0)
    y_ref, nconv_ref, nssm_ref, ext_scr, st_scr = refs
    ci = pl.program_id(1)
    pad = SUBLANE
    hist = CONV_K - 1

    @pl.when(ci == 0)
    def _():
        ext_scr[:, 0:pad, :] = jnp.zeros((nb, pad, CONV_DIM), F32)
        if has_state:
            ext_scr[:, pad - hist:pad, :] = sconv_ref[...]
            st_scr[...] = sssm_ref[...]
        else:
            st_scr[...] = jnp.zeros(st_scr.shape, F32)

    causal = (lax.broadcasted_iota(jnp.int32, (c, c), 0) >= lax.broadcasted_iota(jnp.int32, (c, c), 1))
    left_half = (lax.broadcasted_iota(jnp.int32, (c, SSD_W), 1) % LANE) < P_SSD
    tri, eye, expand = tri_ref[...], eye_ref[...], exp_ref[...]

    for s in range(nb):
        rows = slice(s * c, (s + 1) * c)
        ext_scr[s, pad:pad + c, :] = xbc_ref[rows, :]
        acc = ext_scr[s, pad - hist:pad - hist + c, :] * cw_ref[0:1, :]
        for j in range(1, CONV_K):
            acc = acc + ext_scr[s, pad - hist + j:pad - hist + j + c, :] * cw_ref[j:j + 1, :]
        conv = jax.nn.silu(cb_ref[...] + acc)

        @pl.when(ci == nc - 1)
        def _():
            nconv_ref[s] = ext_scr[s, pad + c - hist:pad + c, :]

        ext_scr[s, 0:pad, :] = ext_scr[s, c:c + pad, :]

        xs = conv[:, :SSD_W]
        bm = conv[:, SSD_W:SSD_W + G_SSD * N_SSD]
        cm = conv[:, SSD_W + G_SSD * N_SSD:]
        dt = jax.nn.softplus(dt_ref[rows, :] + dtb_ref[...])
        dta = dt * a_ref[...]
        cum = sum(_dot(tri, p) for p in _split3(dta))
        cum_parts = _split3(cum)
        cum_t = sum(_dot_nt(eye, p) for p in cum_parts)
        cum_e = sum(_dot(p, expand) for p in cum_parts)
        dt_e = sum(_dot(p, expand) for p in _split3(dt))
        last_e = cum_e[c - 1:c, :]
        xdt = xs * dt_e
        xw = (xs * (jnp.exp(last_e - cum_e) * dt_e)).astype(BF16)
        x_l = jnp.where(left_half, xdt, 0.0).astype(BF16)
        x_r = jnp.where(left_half, 0.0, xdt).astype(BF16)

        y_parts, off_parts = [], []
        for g in range(G_SSD):
            bm_g = bm[:, g * N_SSD:(g + 1) * N_SSD].astype(BF16)
            cm_g = cm[:, g * N_SSD:(g + 1) * N_SSD].astype(BF16)
            cb = _dot_nt(cm_g, bm_g)
            state = st_scr[s, g]
            off_parts.append(_dot_nt(cm_g, state.astype(BF16)))
            decay = jnp.exp(jnp.sum(sel_ref[g] * cum[c - 1:c, :], axis=1, keepdims=True))
            st_scr[s, g] = decay * state + _dot_tn(xw[:, g * GRP_W:(g + 1) * GRP_W], bm_g)
            for jj in range(R_SSD // 2):
                j = g * (R_SSD // 2) + jj
                w = []
                for hd in (2 * j, 2 * j + 1):
                    seg = cum[:, hd:hd + 1] - cum_t[hd:hd + 1, :]
                    w.append((jnp.exp(jnp.where(causal, seg, -jnp.inf)) * cb).astype(BF16))
                cols = slice(j * LANE, (j + 1) * LANE)
                y_parts.append(_dot(w[0], x_l[:, cols]) + _dot(w[1], x_r[:, cols]))
        y = jnp.concatenate(y_parts, axis=1) + jnp.exp(cum_e) * jnp.concatenate(off_parts, axis=1)
        y = (y + dsk_ref[...] * xs) * jax.nn.silu(z_ref[rows, :])
        normed = []
        for g in range(G_SSD):
            yg = y[:, g * GRP_W:(g + 1) * GRP_W]
            normed.append(yg * lax.rsqrt(jnp.mean(yg * yg, axis=-1, keepdims=True) + EPS))
        y_ref[rows, :] = (jnp.concatenate(normed, axis=1) * ng_ref[...]).astype(y_ref.dtype)

    @pl.when(ci == nc - 1)
    def _():
        nssm_ref[...] = st_scr[...]


def _ssd(xbc, z, dt, lp, s_conv, s_ssm, batch, length):
    c = math.gcd(length, CHUNK)
    nc = length // c
    nb = 1 if nc > 1 else SAMPLE_SEQS_PER_STEP
    assert batch % nb == 0 and c % SUBLANE == 0
    has_state = s_conv is not None
    t = batch * length
    rows = nb * c
    tri = (jnp.arange(c)[:, None] >= jnp.arange(c)[None, :]).astype(BF16)
    eye = jnp.eye(LANE, dtype=BF16)
    head_of_ch = jnp.arange(SSD_W) // P_SSD
    expand = (jnp.arange(LANE)[:, None] == head_of_ch[None, :]).astype(BF16)
    sel = (head_of_ch[:, None] == jnp.arange(LANE)[None, :]).astype(F32).reshape(G_SSD, GRP_W, LANE)

    seq = lambda shape: pl.BlockSpec((nb,) + shape, lambda b, ci: (b,) + (0,) * len(shape))
    rowb = lambda w: pl.BlockSpec((rows, w), lambda b, ci: (b * nc + ci, 0))
    in_specs = [rowb(CONV_DIM), rowb(SSD_W), rowb(LANE),
                _full((CONV_K, CONV_DIM)), _full((1, CONV_DIM)), _full((1, LANE)), _full((1, LANE)),
                _full((1, SSD_W)), _full((1, SSD_W)),
                _full((c, c)), _full((LANE, LANE)), _full((LANE, SSD_W)), _full((G_SSD, GRP_W, LANE))]
    args = [xbc, z, dt, lp["conv_w"], lp["conv_b"], lp["dt_bias"], lp["a"], lp["d_skip"], lp["ssd_norm_g"],
            tri, eye, expand, sel]
    if has_state:
        in_specs += [seq((CONV_K - 1, CONV_DIM)), seq((G_SSD, GRP_W, N_SSD))]
        args += [s_conv, s_ssm.reshape(batch, G_SSD, GRP_W, N_SSD)]
    block_bytes = (_nbytes((rows, CONV_DIM + 2 * SSD_W + LANE), F32) + _nbytes((nb, c + SUBLANE, CONV_DIM), F32)
                   + 3 * _nbytes((nb, SSD_W, N_SSD), F32) + _nbytes((LANE, SSD_W), F32)
                   + 12 * _nbytes((c, SSD_W), F32))
    y, nconv, nssm = pl.pallas_call(
        functools.partial(_ssd_body, c, nb, nc, has_state),
        grid=(batch // nb, nc), in_specs=in_specs,
        out_specs=[rowb(SSD_W), seq((CONV_K - 1, CONV_DIM)), seq((G_SSD, GRP_W, N_SSD))],
        out_shape=[jax.ShapeDtypeStruct((t, SSD_W), BF16),
                   jax.ShapeDtypeStruct((batch, CONV_K - 1, CONV_DIM), F32),
                   jax.ShapeDtypeStruct((batch, G_SSD, GRP_W, N_SSD), F32)],
        scratch_shapes=[pltpu.VMEM((nb, c + SUBLANE, CONV_DIM), F32),
                        pltpu.VMEM((nb, G_SSD, GRP_W, N_SSD), F32)],
        compiler_params=_params(("arbitrary", "arbitrary"), block_bytes), name="ssd",
    )(*args)
    return y, nconv, nssm.reshape(batch, H_SSD, P_SSD, N_SSD)


def _first_argmax(vals):
    best = vals[0]
    for v in vals[1:]:
        best = jnp.maximum(best, v)
    idx = jnp.full(best.shape, len(vals) - 1, jnp.int32)
    for i in range(len(vals) - 2, -1, -1):
        idx = jnp.where(vals[i] == best, i, idx)
    return best, idx


def _merge_body(ret_ref, ssd_ref, gates_ref, x_ref, wr_ref, ws_ref, wo_ref, g2_ref, rhi_ref, rlo_ref, low_ref,
                x2_ref, hw_ref, route_ref, cnt_ref):
    ret_y = _dot(ret_ref[...], wr_ref[...])
    ssd_y = _dot(ssd_ref[...], ws_ref[...])
    m = (jax.nn.sigmoid(gates_ref[:, :D_MODEL]) * ret_y + jax.nn.sigmoid(gates_ref[:, D_MODEL:]) * ssd_y)
    x2 = x_ref[...] + _dot(m.astype(BF16), wo_ref[...])
    x2_ref[...] = x2
    h2 = _rmsnorm(x2, g2_ref[...])
    hw_ref[:, :D_MODEL] = h2

    h_hi = h2.astype(BF16)
    h_lo = (h2 - h_hi.astype(F32)).astype(BF16)
    logits = _dot(h_hi, rhi_ref[...]) + (_dot(h_hi, rlo_ref[...]) + _dot(h_lo, rhi_ref[...]))
    colv = lambda i: logits[:, i:i + 1]
    glog = [colv(i) for i in range(N_EGROUPS)]
    gmax, gidx = _first_argmax(glog)
    den = glog[0] * 0.0
    for v in glog:
        den = den + jnp.exp(v - gmax)
    gw = 1.0 / den
    sel = []
    for j in range(EXP_PER_GROUP):
        v = colv(N_EGROUPS + (N_EGROUPS - 1) * EXP_PER_GROUP + j)
        for g in range(N_EGROUPS - 2, -1, -1):
            v = jnp.where(gidx == g, colv(N_EGROUPS + g * EXP_PER_GROUP + j), v)
        sel.append(v)
    v1, i1 = _first_argmax(sel)
    v2, i2 = _first_argmax([jnp.where(i1 == j, -jnp.inf, sel[j]) for j in range(EXP_PER_GROUP)])
    e = jnp.exp(v2 - v1)
    w1 = gw / (1.0 + e)
    w2 = gw * e / (1.0 + e)
    first_low = i1 < i2
    lo = jnp.where(first_low, i1, i2)
    hi = jnp.where(first_low, i2, i1)
    pair = jnp.where(lo == 0, hi - 1, jnp.where(lo == 1, hi + 1, N_PAIRS - 1))
    cls = gidx * N_PAIRS + pair
    w_lo = jnp.where(first_low, w1, w2)
    w_hi = jnp.where(first_low, w2, w1)

    rows = logits.shape[0]
    lane = lax.broadcasted_iota(jnp.int32, (rows, META_W), 1)
    onehot = (lane == cls).astype(F32)
    before = _dot(low_ref[...], onehot.astype(BF16))
    rank = jnp.sum(onehot * before, axis=1, keepdims=True)
    meta = jnp.where(lane == 0, w_lo, jnp.where(lane == 1, w_hi, jnp.where(
        lane == 2, cls.astype(F32), jnp.where(lane == 3, rank, 0.0))))
    hw_ref[:, D_MODEL:] = meta
    route_ref[...] = meta
    cnt_ref[0] = jnp.sum(onehot, axis=0, keepdims=True)


def _merge(ret, ssd, gates, x, lp):
    t = x.shape[0]
    nt = t // ROW_TILE
    row = lambda w: pl.BlockSpec((ROW_TILE, w), lambda i: (i, 0))
    low = (jnp.arange(ROW_TILE)[:, None] > jnp.arange(ROW_TILE)[None, :]).astype(BF16)
    ws = [lp["w_ret_out"], lp["w_ssd_out"], lp["w_o"], lp["ln2_g"], lp["router_hi"], lp["router_lo"], low]
    block_bytes = (sum(_nbytes(w.shape, w.dtype) for w in ws)
                   + _nbytes((ROW_TILE, RET_W + SSD_W), BF16) + _nbytes((ROW_TILE, 5 * D_MODEL + 2 * HW_W), F32))
    return pl.pallas_call(
        _merge_body, grid=(nt,),
        in_specs=[row(RET_W), row(SSD_W), row(2 * D_MODEL), row(D_MODEL)] + [_full(w.shape) for w in ws],
        out_specs=[row(D_MODEL), row(HW_W), row(META_W), pl.BlockSpec((1, 1, META_W), lambda i: (i, 0, 0))],
        out_shape=[jax.ShapeDtypeStruct((t, D_MODEL), F32), jax.ShapeDtypeStruct((t, HW_W), F32),
                   jax.ShapeDtypeStruct((t, META_W), F32), jax.ShapeDtypeStruct((nt, 1, META_W), F32)],
        compiler_params=_params(("arbitrary",), block_bytes), name="merge_router",
    )(ret, ssd, gates, x, *ws)


def _moe_body(tile, tok_ref, lo_ref, hi_ref, nv_ref, nu_ref, hw_hbm, wgl_ref, wul_ref, wdl_ref,
              wgh_ref, wuh_ref, wdh_ref, y_hbm, buf, obuf, gsem, ssem):
    i = pl.program_id(0)
    n_used = nu_ref[0]
    slot = i % 2

    def gather_copy(t, r, sl):
        return pltpu.make_async_copy(hw_hbm.at[pl.ds(tok_ref[t * tile + r], 1)],
                                     buf.at[sl, pl.ds(r, 1)], gsem.at[sl])

    def scatter_copy(t, r, sl):
        return pltpu.make_async_copy(obuf.at[sl, pl.ds(r, 1)],
                                     y_hbm.at[pl.ds(tok_ref[t * tile + r], 1)], ssem.at[sl])

    def for_rows(n, fn):
        lax.fori_loop(0, n, lambda r, carry: (fn(r), carry)[1], 0)

    @pl.when(i == 0)
    def _():
        for_rows(tile, lambda r: gather_copy(0, r, 0).start())

    @pl.when(i + 1 < n_used)
    def _():
        for_rows(tile, lambda r: gather_copy(i + 1, r, 1 - slot).start())

    @pl.when(i < n_used)
    def _():
        for_rows(tile, lambda r: gather_copy(i, r, slot).wait())
        xw = buf[slot]
        xb = xw[:, :D_MODEL].astype(BF16)
        out = None
        for wg_ref, wu_ref, wd_ref, lane in ((wgl_ref, wul_ref, wdl_ref, 0), (wgh_ref, wuh_ref, wdh_ref, 1)):
            act = jax.nn.silu(_dot(xb, wg_ref[...])) * _dot(xb, wu_ref[...])
            part = _dot(act.astype(BF16), wd_ref[...]) * xw[:, D_MODEL + lane:D_MODEL + lane + 1]
            out = part if out is None else out + part
        obuf[slot] = out
        n_valid = nv_ref[i]

        @pl.when(i >= 1)
        def _():
            for_rows(nv_ref[i - 1], lambda r: scatter_copy(i - 1, r, 1 - slot).wait())

        for_rows(n_valid, lambda r: scatter_copy(i, r, slot).start())

        @pl.when(i == n_used - 1)
        def _():
            for_rows(n_valid, lambda r: scatter_copy(i, r, slot).wait())


def _moe(hw, route, counts, lp, tile):
    t = hw.shape[0]
    n_tiles = t // tile + N_CLASSES
    cnt = counts[:, 0, :N_CLASSES].astype(jnp.int32)
    tot = jnp.sum(cnt, axis=0)
    padded = (tot + tile - 1) // tile * tile
    ends = jnp.cumsum(padded)
    base = ends - padded
    tile_pref = jnp.cumsum(cnt, axis=0) - cnt
    cls = route[:, 2].astype(jnp.int32)
    rank = route[:, 3].astype(jnp.int32)
    pos = base[cls] + tile_pref[jnp.arange(t) // ROW_TILE, cls] + rank
    tok = jnp.zeros((n_tiles * tile,), jnp.int32).at[pos].set(jnp.arange(t, dtype=jnp.int32))
    n_used = (ends[-1] // tile).astype(jnp.int32)
    starts = jnp.minimum(jnp.arange(n_tiles, dtype=jnp.int32), n_used - 1) * tile
    tcls = jnp.minimum(jnp.searchsorted(ends, starts, side="right"), N_CLASSES - 1).astype(jnp.int32)
    n_valid = jnp.clip(tot[tcls] - (starts - base[tcls]), 0, tile).astype(jnp.int32)
    pair_lo = jnp.array([0, 0, 0, 1, 1, 2], jnp.int32)
    pair_hi = jnp.array([1, 2, 3, 2, 3, 3], jnp.int32)
    e_lo = (tcls // N_PAIRS) * EXP_PER_GROUP + pair_lo[tcls % N_PAIRS]
    e_hi = (tcls // N_PAIRS) * EXP_PER_GROUP + pair_hi[tcls % N_PAIRS]

    wspec = lambda shape, which: pl.BlockSpec(
        (None,) + shape, lambda i, tok, lo, hi, nv, nu: ((lo, hi)[which][i], 0, 0))
    in_specs = [pl.BlockSpec(memory_space=pl.ANY)]
    for which in (0, 1):
        in_specs += [wspec((D_MODEL, D_FF), which), wspec((D_MODEL, D_FF), which), wspec((D_FF, D_MODEL), which)]
    block_bytes = (6 * _nbytes((D_MODEL, D_FF), BF16) + _nbytes((tile, HW_W + D_MODEL), F32)
                   + 4 * _nbytes((tile, D_MODEL), F32))
    grid_spec = pltpu.PrefetchScalarGridSpec(
        num_scalar_prefetch=5, grid=(n_tiles,), in_specs=in_specs,
        out_specs=pl.BlockSpec(memory_space=pl.ANY),
        scratch_shapes=[pltpu.VMEM((2, tile, HW_W), F32), pltpu.VMEM((2, tile, D_MODEL), F32),
                        pltpu.SemaphoreType.DMA((2,)), pltpu.SemaphoreType.DMA((2,))])
    return pl.pallas_call(
        functools.partial(_moe_body, tile), grid_spec=grid_spec,
        out_shape=jax.ShapeDtypeStruct((t, D_MODEL), F32),
        compiler_params=_params(("arbitrary",), block_bytes), name="moe",
    )(tok, e_lo, e_hi, n_valid, n_used.reshape(1), hw,
      lp["w_gate"], lp["w_up"], lp["w_down"], lp["w_gate"], lp["w_up"], lp["w_down"])


def _final_body(x_ref, y_ref, g_ref, o_ref):
    o_ref[...] = _rmsnorm(x_ref[...] + y_ref[...], g_ref[...])


def _final_norm(x, y, g):
    t = x.shape[0]
    row = pl.BlockSpec((ROW_TILE, D_MODEL), lambda i: (i, 0))
    return pl.pallas_call(
        _final_body, grid=(t // ROW_TILE,), in_specs=[row, row, _full((1, D_MODEL))], out_specs=row,
        out_shape=jax.ShapeDtypeStruct((t, D_MODEL), F32),
        compiler_params=_params(("arbitrary",), 3 * _nbytes((ROW_TILE, D_MODEL), F32)), name="final_norm",
    )(x, y, g)


def _layer_params(i, ln1_g, w_in, ret_gn_g, w_ret_out, conv_w, conv_b, dt_bias, a_log, d_skip, ssd_norm_g,
                  w_ssd_out, w_o, ln2_g, w_rg, w_re, w_gate, w_up, w_down):
    offs = [0]
    for s in IN_SIZES:
        offs.append(offs[-1] + s)
    w = w_in[i]
    pad_heads = lambda v: jnp.pad(v.astype(F32), (0, LANE - H_SSD)).reshape(1, LANE)
    router = jnp.pad(jnp.concatenate([w_rg[i], w_re[i]], axis=1),
                     ((0, 0), (0, META_W - N_EGROUPS - N_EXPERTS)))
    router_hi = router.astype(BF16)
    return {
        "ln1_g": ln1_g[i].reshape(1, D_MODEL),
        "w_in": [w[:, offs[0]:offs[4]].astype(BF16),
                 w[:, offs[4]:offs[5]].astype(BF16),
                 w[:, offs[5]:offs[6]].astype(BF16),
                 jnp.pad(w[:, offs[6]:offs[7]], ((0, 0), (0, LANE - H_SSD))).astype(BF16),
                 w[:, offs[7]:offs[9]].astype(BF16)],
        "ret_gn_g": ret_gn_g[i].reshape(1, RET_W),
        "w_ret_out": w_ret_out[i].astype(BF16),
        "conv_w": conv_w[i], "conv_b": conv_b[i].reshape(1, CONV_DIM),
        "dt_bias": pad_heads(dt_bias[i]), "a": pad_heads(-jnp.exp(a_log[i].astype(F32))),
        "d_skip": jnp.repeat(d_skip[i].astype(F32), P_SSD).reshape(1, SSD_W),
        "ssd_norm_g": ssd_norm_g[i].reshape(1, SSD_W),
        "w_ssd_out": w_ssd_out[i].astype(BF16), "w_o": w_o[i].astype(BF16),
        "ln2_g": ln2_g[i].reshape(1, D_MODEL),
        "router_hi": router_hi, "router_lo": (router - router_hi.astype(F32)).astype(BF16),
        "w_gate": w_gate[i].astype(BF16), "w_up": w_up[i].astype(BF16), "w_down": w_down[i].astype(BF16),
    }


def _trunk(x, start, states, layers, lnf_g):
    batch, length, _ = x.shape
    t = batch * length
    moe_tile = 256 if t >= 4096 else 128
    x = x.reshape(t, D_MODEL)
    y = None
    rets, ssms, convs = [], [], []
    for i, lp in enumerate(layers):
        s_ret, s_ssm, s_conv = (None, None, None) if states is None else (s[i] for s in states)
        outs = _inproj(x, y, lp["ln1_g"], lp["w_in"])
        if y is not None:
            x, outs = outs[0], outs[1:]
        qkvr, z, xbc, dt, gates = outs
        ret, new_ret = _retention(qkvr, start, lp["ret_gn_g"], s_ret, batch, length)
        ssd, new_conv, new_ssm = _ssd(xbc, z, dt, lp, s_conv, s_ssm, batch, length)
        x, hw, route, counts = _merge(ret, ssd, gates, x, lp)
        y = _moe(hw, route, counts, lp, moe_tile)
        rets.append(new_ret)
        ssms.append(new_ssm)
        convs.append(new_conv)
    out = _final_norm(x, y, lnf_g.reshape(1, D_MODEL)).reshape(batch, length, D_MODEL)
    return out, jnp.stack(rets), jnp.stack(ssms), jnp.stack(convs)


def kernel(x_prompt, x_sample, state_ret, state_ssm, state_conv, ln1_g, w_in, ret_gn_g, w_ret_out, conv_w, conv_b, dt_bias, a_log, d_skip, ssd_norm_g, w_ssd_out, w_o, ln2_g, w_router_group, w_router_expert, w_e_gate, w_e_up, w_e_down, lnf_g):
    depth = w_in.shape[0]
    past_len = 16384.0
    layers = [_layer_params(i, ln1_g, w_in, ret_gn_g, w_ret_out, conv_w, conv_b, dt_bias, a_log, d_skip,
                            ssd_norm_g, w_ssd_out, w_o, ln2_g, w_router_group, w_router_expert,
                            w_e_gate, w_e_up, w_e_down) for i in range(depth)]
    y_p, p_ret, p_ssm, p_conv = _trunk(x_prompt, 0.0, None, layers, lnf_g)
    y_s, s_ret, s_ssm, s_conv = _trunk(x_sample, past_len, (state_ret, state_ssm, state_conv), layers, lnf_g)
    return (y_p, y_s, p_ret, p_ssm, p_conv, s_ret, s_ssm, s_conv)
```

```python
import functools
import math

import jax
import jax.numpy as jnp
from jax import lax
from jax.experimental import pallas as pl
from jax.experimental.pallas import tpu as pltpu

F32 = jnp.float32
BF16 = jnp.bfloat16

D_MODEL = 1024
H_RET, DK_RET, DV_RET = 4, 128, 128
RET_W = H_RET * DV_RET
ROPE_BASE = 10000.0
H_SSD, P_SSD, G_SSD, N_SSD = 16, 64, 2, 64
R_SSD = H_SSD // G_SSD
SSD_W = H_SSD * P_SSD
GRP_W = SSD_W // G_SSD
CONV_K = 4
CONV_DIM = SSD_W + 2 * G_SSD * N_SSD
CHUNK = 128
N_EGROUPS, EXP_PER_GROUP = 4, 4
N_EXPERTS = N_EGROUPS * EXP_PER_GROUP
N_PAIRS = EXP_PER_GROUP * (EXP_PER_GROUP - 1) // 2
N_CLASSES = N_EGROUPS * N_PAIRS
D_FF = 512
EPS = 1e-6
IN_SIZES = [RET_W, RET_W, RET_W, RET_W, SSD_W, CONV_DIM, H_SSD, D_MODEL, D_MODEL]

LANE = 128
SUBLANE = 8
MIB = 1024 * 1024

ROW_TILE = 256
META_W = LANE
HW_W = D_MODEL + META_W
SAMPLE_SEQS_PER_STEP = 8


def _vmem_limit(block_bytes):
    return int(min(2 * block_bytes + 16 * MIB, 56 * MIB))


def _params(semantics, block_bytes):
    return pltpu.CompilerParams(dimension_semantics=semantics,
                                vmem_limit_bytes=_vmem_limit(block_bytes))


def _nbytes(shape, dtype):
    return math.prod(shape) * jnp.dtype(dtype).itemsize


def _full(shape):
    return pl.BlockSpec(shape, lambda *_: (0,) * len(shape))


def _dot(a, b):
    return jnp.dot(a, b, preferred_element_type=F32)


def _dot_nt(a, b):
    return lax.dot_general(a, b, (((1,), (1,)), ((), ())), preferred_element_type=F32)


def _dot_tn(a, b):
    return lax.dot_general(a, b, (((0,), (0,)), ((), ())), preferred_element_type=F32)


def _split3(a):
    hi = a.astype(BF16)
    r1 = a - hi.astype(F32)
    mid = r1.astype(BF16)
    lo = (r1 - mid.astype(F32)).astype(BF16)
    return hi, mid, lo


def _rmsnorm(x, g):
    r = lax.rsqrt(jnp.mean(x * x, axis=-1, keepdims=True) + EPS)
    return x * r * g


def _inproj_body(has_add, *refs):
    n_in = 8 if has_add else 7
    ins, outs = refs[:n_in], refs[n_in:]
    if has_add:
        x_ref, y_ref, g_ref = ins[:3]
        x = x_ref[...] + y_ref[...]
        outs[0][...] = x
        outs = outs[1:]
    else:
        x_ref, g_ref = ins[:2]
        x = x_ref[...]
    w_refs = ins[-5:]
    h = _rmsnorm(x, g_ref[...]).astype(BF16)
    for w_ref, o_ref in zip(w_refs, outs, strict=True):
        o_ref[...] = _dot(h, w_ref[...])


def _inproj(x, y, g, ws):
    t = x.shape[0]
    has_add = y is not None
    widths = [w.shape[1] for w in ws]
    row = lambda w: pl.BlockSpec((ROW_TILE, w), lambda i: (i, 0))
    in_specs = [row(D_MODEL)] * (2 if has_add else 1) + [_full((1, D_MODEL))]
    in_specs += [_full(w.shape) for w in ws]
    out_shape = [jax.ShapeDtypeStruct((t, w), F32) for w in widths]
    out_specs = [row(w) for w in widths]
    if has_add:
        out_shape = [jax.ShapeDtypeStruct((t, D_MODEL), F32)] + out_shape
        out_specs = [row(D_MODEL)] + out_specs
    block_bytes = (sum(_nbytes(w.shape, BF16) for w in ws)
                   + _nbytes((ROW_TILE, sum(widths) + 3 * D_MODEL), F32))
    args = ([x, y] if has_add else [x]) + [g] + list(ws)
    return pl.pallas_call(
        functools.partial(_inproj_body, has_add),
        grid=(t // ROW_TILE,), in_specs=in_specs, out_specs=out_specs, out_shape=out_shape,
        compiler_params=_params(("arbitrary",), block_bytes), name="inproj",
    )(*args)


def _retention_body(c, nb, nc, has_state, *refs):
    refs = list(refs)
    qkvr_ref, cos_ref, sin_ref, dmat_ref, qdec_ref, kdec_ref, cdec_ref, gn_ref = refs[:8]
    refs = refs[8:]
    if has_state:
        s0_ref = refs.pop(0)
    o_ref, ns_ref, st_scr = refs
    ci = pl.program_id(1)

    @pl.when(ci == 0)
    def _():
        if has_state:
            st_scr[...] = s0_ref[...]
        else:
            st_scr[...] = jnp.zeros(st_scr.shape, F32)

    for s in range(nb):
        rows = slice(s * c, (s + 1) * c)
        cosv, sinv = cos_ref[rows, :], sin_ref[rows, :]
        for h in range(H_RET):
            col = lambda j: slice(j * RET_W + h * DK_RET, j * RET_W + (h + 1) * DK_RET)
            q, k = qkvr_ref[rows, col(0)], qkvr_ref[rows, col(1)]
            v, rg = qkvr_ref[rows, col(2)], qkvr_ref[rows, col(3)]
            qr = q * cosv + pltpu.roll(q, DK_RET // 2, axis=1) * sinv
            kr = (k * cosv + pltpu.roll(k, DK_RET // 2, axis=1) * sinv) * (DK_RET ** -0.5)
            qb, kb, vb = qr.astype(BF16), kr.astype(BF16), v.astype(BF16)
            state = st_scr[s, h]
            sc = _dot_nt(qb, kb) * dmat_ref[h]
            o = _dot(sc.astype(BF16), vb) + _dot(qb, state.astype(BF16)) * qdec_ref[h]
            st_scr[s, h] = cdec_ref[h] * state + _dot_tn((kr * kdec_ref[h]).astype(BF16), vb)
            mu = jnp.mean(o, axis=-1, keepdims=True)
            d = o - mu
            var = jnp.mean(d * d, axis=-1, keepdims=True)
            on = d * lax.rsqrt(var + EPS) * gn_ref[:, h * DV_RET:(h + 1) * DV_RET]
            o_ref[rows, h * DV_RET:(h + 1) * DV_RET] = (jax.nn.silu(rg) * on).astype(o_ref.dtype)

    @pl.when(ci == nc - 1)
    def _():
        ns_ref[...] = st_scr[...]


def _retention(qkvr, start, gn_g, s0, batch, length):
    c = math.gcd(length, CHUNK)
    nc = length // c
    nb = 1 if nc > 1 else SAMPLE_SEQS_PER_STEP
    assert batch % nb == 0
    has_state = s0 is not None
    t = batch * length
    half = DK_RET // 2
    pos = start + jnp.arange(length, dtype=F32)
    inv = ROPE_BASE ** (-jnp.arange(half, dtype=F32) / half)
    ang = pos[:, None] * inv[None, :]
    cosv = jnp.tile(jnp.concatenate([jnp.cos(ang), jnp.cos(ang)], axis=1), (nb, 1))
    sinv = jnp.tile(jnp.concatenate([-jnp.sin(ang), jnp.sin(ang)], axis=1), (nb, 1))
    log_g = jnp.log1p(-jnp.exp2(-5.0 - jnp.arange(H_RET, dtype=F32)))
    idx = jnp.arange(c, dtype=F32)
    rel = idx[:, None] - idx[None, :]
    causal = rel >= 0
    dmat = jnp.where(causal[None], jnp.exp(jnp.where(causal, rel, 0.0)[None] * log_g[:, None, None]), 0.0)
    qdec = jnp.broadcast_to(jnp.exp((idx[None, :] + 1.0) * log_g[:, None])[:, :, None], (H_RET, c, LANE))
    kdec = jnp.broadcast_to(jnp.exp((c - 1.0 - idx[None, :]) * log_g[:, None])[:, :, None], (H_RET, c, LANE))
    cdec = jnp.exp(c * log_g)

    rows = nb * c
    state_spec = pl.BlockSpec((nb, H_RET, DK_RET, DV_RET), lambda b, ci: (b, 0, 0, 0))
    in_specs = [
        pl.BlockSpec((rows, 4 * RET_W), lambda b, ci: (b * nc + ci, 0)),
        pl.BlockSpec((rows, LANE), lambda b, ci: (ci, 0)),
        pl.BlockSpec((rows, LANE), lambda b, ci: (ci, 0)),
        _full((H_RET, c, c)), _full((H_RET, c, LANE)), _full((H_RET, c, LANE)),
        pl.BlockSpec(memory_space=pltpu.SMEM),
        _full((1, RET_W)),
    ]
    args = [qkvr, cosv, sinv, dmat, qdec, kdec, cdec, gn_g]
    if has_state:
        in_specs.append(state_spec)
        args.append(s0)
    block_bytes = (_nbytes((rows, 4 * RET_W + 2 * LANE + RET_W), F32)
                   + 3 * _nbytes((nb, H_RET, DK_RET, DV_RET), F32) + 3 * _nbytes((H_RET, c, LANE), F32))
    return pl.pallas_call(
        functools.partial(_retention_body, c, nb, nc, has_state),
        grid=(batch // nb, nc), in_specs=in_specs,
        out_specs=[pl.BlockSpec((rows, RET_W), lambda b, ci: (b * nc + ci, 0)), state_spec],
        out_shape=[jax.ShapeDtypeStruct((t, RET_W), BF16),
                   jax.ShapeDtypeStruct((batch, H_RET, DK_RET, DV_RET), F32)],
        scratch_shapes=[pltpu.VMEM((nb, H_RET, DK_RET, DV_RET), F32)],
        compiler_params=_params(("arbitrary", "arbitrary"), block_bytes), name="retention",
    )(*args)


def _ssd_body(c, nb, nc, has_state, *refs):
    refs = list(refs)
    (xbc_ref, z_ref, dt_ref, cw_ref, cb_ref, dtb_ref, a_ref, dsk_ref, ng_ref,
     tri_ref, eye_ref, exp_ref, sel_ref) = refs[:13]
    refs = refs[13:]
    if has_state:
        sconv_ref, sssm_ref = refs.pop(0), refs.pop(0)
    y_ref, nconv_ref, nssm_ref, ext_scr, st_scr = refs
    ci = pl.program_id(1)
    pad = SUBLANE
    hist = CONV_K - 1

    @pl.when(ci == 0)
    def _():
        ext_scr[:, 0:pad, :] = jnp.zeros((nb, pad, CONV_DIM), F32)
        if has_state:
            ext_scr[:, pad - hist:pad, :] = sconv_ref[...]
            st_scr[...] = sssm_ref[...]
        else:
            st_scr[...] = jnp.zeros(st_scr.shape, F32)

    causal = (lax.broadcasted_iota(jnp.int32, (c, c), 0) >= lax.broadcasted_iota(jnp.int32, (c, c), 1))
    left_half = (lax.broadcasted_iota(jnp.int32, (c, SSD_W), 1) % LANE) < P_SSD
    tri, eye, expand = tri_ref[...], eye_ref[...], exp_ref[...]

    for s in range(nb):
        rows = slice(s * c, (s + 1) * c)
        ext_scr[s, pad:pad + c, :] = xbc_ref[rows, :]
        acc = ext_scr[s, pad - hist:pad - hist + c, :] * cw_ref[0:1, :]
        for j in range(1, CONV_K):
            acc = acc + ext_scr[s, pad - hist + j:pad - hist + j + c, :] * cw_ref[j:j + 1, :]
        conv = jax.nn.silu(cb_ref[...] + acc)

        @pl.when(ci == nc - 1)
        def _():
            nconv_ref[s] = ext_scr[s, pad + c - hist:pad + c, :]

        ext_scr[s, 0:pad, :] = ext_scr[s, c:c + pad, :]

        xs = conv[:, :SSD_W]
        bm = conv[:, SSD_W:SSD_W + G_SSD * N_SSD]
        cm = conv[:, SSD_W + G_SSD * N_SSD:]
        dt = jax.nn.softplus(dt_ref[rows, :] + dtb_ref[...])
        dta = dt * a_ref[...]
        cum = sum(_dot(tri, p) for p in _split3(dta))
        cum_parts = _split3(cum)
        cum_t = sum(_dot_nt(eye, p) for p in cum_parts)
        cum_e = sum(_dot(p, expand) for p in cum_parts)
        dt_e = sum(_dot(p, expand) for p in _split3(dt))
        last_e = cum_e[c - 1:c, :]
        xdt = xs * dt_e
        xw = (xs * (jnp.exp(last_e - cum_e) * dt_e)).astype(BF16)
        x_l = jnp.where(left_half, xdt, 0.0).astype(BF16)
        x_r = jnp.where(left_half, 0.0, xdt).astype(BF16)

        y_parts, off_parts = [], []
        for g in range(G_SSD):
            bm_g = bm[:, g * N_SSD:(g + 1) * N_SSD].astype(BF16)
            cm_g = cm[:, g * N_SSD:(g + 1) * N_SSD].astype(BF16)
            cb = _dot_nt(cm_g, bm_g)
            state = st_scr[s, g]
            off_parts.append(_dot_nt(cm_g, state.astype(BF16)))
            decay = jnp.exp(jnp.sum(sel_ref[g] * cum[c - 1:c, :], axis=1, keepdims=True))
            st_scr[s, g] = decay * state + _dot_tn(xw[:, g * GRP_W:(g + 1) * GRP_W], bm_g)
            for jj in range(R_SSD // 2):
                j = g * (R_SSD // 2) + jj
                w = []
                for hd in (2 * j, 2 * j + 1):
                    seg = cum[:, hd:hd + 1] - cum_t[hd:hd + 1, :]
                    w.append((jnp.exp(jnp.where(causal, seg, -jnp.inf)) * cb).astype(BF16))
                cols = slice(j * LANE, (j + 1) * LANE)
                y_parts.append(_dot(w[0], x_l[:, cols]) + _dot(w[1], x_r[:, cols]))
        y = jnp.concatenate(y_parts, axis=1) + jnp.exp(cum_e) * jnp.concatenate(off_parts, axis=1)
        y = (y + dsk_ref[...] * xs) * jax.nn.silu(z_ref[rows, :])
        normed = []
        for g in range(G_SSD):
            yg = y[:, g * GRP_W:(g + 1) * GRP_W]
            normed.append(yg * lax.rsqrt(jnp.mean(yg * yg, axis=-1, keepdims=True) + EPS))
        y_ref[rows, :] = (jnp.concatenate(normed, axis=1) * ng_ref[...]).astype(y_ref.dtype)

    @pl.when(ci == nc - 1)
    def _():
        nssm_ref[...] = st_scr[...]


def _ssd(xbc, z, dt, lp, s_conv, s_ssm, batch, length):
    c = math.gcd(length, CHUNK)
    nc = length // c
    nb = 1 if nc > 1 else SAMPLE_SEQS_PER_STEP
    assert batch % nb == 0 and c % SUBLANE == 0
    has_state = s_conv is not None
    t = batch * length
    rows = nb * c
    tri = (jnp.arange(c)[:, None] >= jnp.arange(c)[None, :]).astype(BF16)
    eye = jnp.eye(LANE, dtype=BF16)
    head_of_ch = jnp.arange(SSD_W) // P_SSD
    expand = (jnp.arange(LANE)[:, None] == head_of_ch[None, :]).astype(BF16)
    sel = (head_of_ch[:, None] == jnp.arange(LANE)[None, :]).astype(F32).reshape(G_SSD, GRP_W, LANE)

    seq = lambda shape: pl.BlockSpec((nb,) + shape, lambda b, ci: (b,) + (0,) * len(shape))
    rowb = lambda w: pl.BlockSpec((rows, w), lambda b, ci: (b * nc + ci, 0))
    in_specs = [rowb(CONV_DIM), rowb(SSD_W), rowb(LANE),
                _full((CONV_K, CONV_DIM)), _full((1, CONV_DIM)), _full((1, LANE)), _full((1, LANE)),
                _full((1, SSD_W)), _full((1, SSD_W)),
                _full((c, c)), _full((LANE, LANE)), _full((LANE, SSD_W)), _full((G_SSD, GRP_W, LANE))]
    args = [xbc, z, dt, lp["conv_w"], lp["conv_b"], lp["dt_bias"], lp["a"], lp["d_skip"], lp["ssd_norm_g"],
            tri, eye, expand, sel]
    if has_state:
        in_specs += [seq((CONV_K - 1, CONV_DIM)), seq((G_SSD, GRP_W, N_SSD))]
        args += [s_conv, s_ssm.reshape(batch, G_SSD, GRP_W, N_SSD)]
    block_bytes = (_nbytes((rows, CONV_DIM + 2 * SSD_W + LANE), F32) + _nbytes((nb, c + SUBLANE, CONV_DIM), F32)
                   + 3 * _nbytes((nb, SSD_W, N_SSD), F32) + _nbytes((LANE, SSD_W), F32)
                   + 12 * _nbytes((c, SSD_W), F32))
    y, nconv, nssm = pl.pallas_call(
        functools.partial(_ssd_body, c, nb, nc, has_state),
        grid=(batch // nb, nc), in_specs=in_specs,
        out_specs=[rowb(SSD_W), seq((CONV_K - 1, CONV_DIM)), seq((G_SSD, GRP_W, N_SSD))],
        out_shape=[jax.ShapeDtypeStruct((t, SSD_W), BF16),
                   jax.ShapeDtypeStruct((batch, CONV_K - 1, CONV_DIM), F32),
                   jax.ShapeDtypeStruct((batch, G_SSD, GRP_W, N_SSD), F32)],
        scratch_shapes=[pltpu.VMEM((nb, c + SUBLANE, CONV_DIM), F32),
                        pltpu.VMEM((nb, G_SSD, GRP_W, N_SSD), F32)],
        compiler_params=_params(("arbitrary", "arbitrary"), block_bytes), name="ssd",
    )(*args)
    return y, nconv, nssm.reshape(batch, H_SSD, P_SSD, N_SSD)


def _first_argmax(vals):
    best = vals[0]
    for v in vals[1:]:
        best = jnp.maximum(best, v)
    idx = jnp.full(best.shape, len(vals) - 1, jnp.int32)
    for i in range(len(vals) - 2, -1, -1):
        idx = jnp.where(vals[i] == best, i, idx)
    return best, idx


def _merge_body(ret_ref, ssd_ref, gates_ref, x_ref, wr_ref, ws_ref, wo_ref, g2_ref, rhi_ref, rlo_ref, low_ref,
                x2_ref, hw_ref, route_ref, cnt_ref):
    ret_y = _dot(ret_ref[...], wr_ref[...])
    ssd_y = _dot(ssd_ref[...], ws_ref[...])
    m = (jax.nn.sigmoid(gates_ref[:, :D_MODEL]) * ret_y + jax.nn.sigmoid(gates_ref[:, D_MODEL:]) * ssd_y)
    x2 = x_ref[...] + _dot(m.astype(BF16), wo_ref[...])
    x2_ref[...] = x2
    h2 = _rmsnorm(x2, g2_ref[...])
    hw_ref[:, :D_MODEL] = h2

    h_hi = h2.astype(BF16)
    h_lo = (h2 - h_hi.astype(F32)).astype(BF16)
    logits = _dot(h_hi, rhi_ref[...]) + (_dot(h_hi, rlo_ref[...]) + _dot(h_lo, rhi_ref[...]))
    colv = lambda i: logits[:, i:i + 1]
    glog = [colv(i) for i in range(N_EGROUPS)]
    gmax, gidx = _first_argmax(glog)
    den = glog[0] * 0.0
    for v in glog:
        den = den + jnp.exp(v - gmax)
    gw = 1.0 / den
    sel = []
    for j in range(EXP_PER_GROUP):
        v = colv(N_EGROUPS + (N_EGROUPS - 1) * EXP_PER_GROUP + j)
        for g in range(N_EGROUPS - 2, -1, -1):
            v = jnp.where(gidx == g, colv(N_EGROUPS + g * EXP_PER_GROUP + j), v)
        sel.append(v)
    v1, i1 = _first_argmax(sel)
    v2, i2 = _first_argmax([jnp.where(i1 == j, -jnp.inf, sel[j]) for j in range(EXP_PER_GROUP)])
    e = jnp.exp(v2 - v1)
    w1 = gw / (1.0 + e)
    w2 = gw * e / (1.0 + e)
    first_low = i1 < i2
    lo = jnp.where(first_low, i1, i2)
    hi = jnp.where(first_low, i2, i1)
    pair = jnp.where(lo == 0, hi - 1, jnp.where(lo == 1, hi + 1, N_PAIRS - 1))
    cls = gidx * N_PAIRS + pair
    w_lo = jnp.where(first_low, w1, w2)
    w_hi = jnp.where(first_low, w2, w1)

    rows = logits.shape[0]
    lane = lax.broadcasted_iota(jnp.int32, (rows, META_W), 1)
    onehot = (lane == cls).astype(F32)
    before = _dot(low_ref[...], onehot.astype(BF16))
    rank = jnp.sum(onehot * before, axis=1, keepdims=True)
    meta = jnp.where(lane == 0, w_lo, jnp.where(lane == 1, w_hi, jnp.where(
        lane == 2, cls.astype(F32), jnp.where(lane == 3, rank, 0.0))))
    hw_ref[:, D_MODEL:] = meta
    route_ref[...] = meta
    cnt_ref[0] = jnp.sum(onehot, axis=0, keepdims=True)


def _merge(ret, ssd, gates, x, lp):
    t = x.shape[0]
    nt = t // ROW_TILE
    row = lambda w: pl.BlockSpec((ROW_TILE, w), lambda i: (i, 0))
    low = (jnp.arange(ROW_TILE)[:, None] > jnp.arange(ROW_TILE)[None, :]).astype(BF16)
    ws = [lp["w_ret_out"], lp["w_ssd_out"], lp["w_o"], lp["ln2_g"], lp["router_hi"], lp["router_lo"], low]
    block_bytes = (sum(_nbytes(w.shape, w.dtype) for w in ws)
                   + _nbytes((ROW_TILE, RET_W + SSD_W), BF16) + _nbytes((ROW_TILE, 5 * D_MODEL + 2 * HW_W), F32))
    return pl.pallas_call(
        _merge_body, grid=(nt,),
        in_specs=[row(RET_W), row(SSD_W), row(2 * D_MODEL), row(D_MODEL)] + [_full(w.shape) for w in ws],
        out_specs=[row(D_MODEL), row(HW_W), row(META_W), pl.BlockSpec((1, 1, META_W), lambda i: (i, 0, 0))],
        out_shape=[jax.ShapeDtypeStruct((t, D_MODEL), F32), jax.ShapeDtypeStruct((t, HW_W), F32),
                   jax.ShapeDtypeStruct((t, META_W), F32), jax.ShapeDtypeStruct((nt, 1, META_W), F32)],
        compiler_params=_params(("arbitrary",), block_bytes), name="merge_router",
    )(ret, ssd, gates, x, *ws)


PLAN_ROWS = LANE
TAB_ROWS = SUBLANE
TAB_LO, TAB_HI, TAB_USED = 0, 1, 2
BYTE = 256.0


def _row_of(col_vals):
    pick = (lax.broadcasted_iota(jnp.int32, (TAB_ROWS, LANE), 0)
            == lax.broadcasted_iota(jnp.int32, (TAB_ROWS, LANE), 1)).astype(BF16)
    hi = jnp.floor(col_vals * (1.0 / BYTE))
    lo = col_vals - hi * BYTE
    return BYTE * _dot_nt(pick, hi.astype(BF16)) + _dot_nt(pick, lo.astype(BF16))


def _plan_body(tile, n_row_tiles, route_ref, cnt_ref, low_ref, upp_ref, pos_ref, tab_ref, offs_scr):
    cnt = cnt_ref[...]
    tot = jnp.sum(cnt, axis=0, keepdims=True)
    cls_tiles = jnp.floor((tot + (tile - 1.0)) * (1.0 / tile))
    first_tile = _dot(jnp.broadcast_to(cls_tiles, (SUBLANE, LANE)).astype(BF16), upp_ref[...])[0:1, :]
    base = first_tile * tile
    ends = base + cls_tiles * tile
    offs_scr[...] = base + _dot(low_ref[...], cnt.astype(BF16))

    def per_row_tile(j, carry):
        meta = route_ref[pl.ds(j * ROW_TILE, ROW_TILE), :]
        lane = lax.broadcasted_iota(jnp.int32, (ROW_TILE, META_W), 1)
        onehot = lane.astype(F32) == meta[:, 2:3]
        pos = jnp.sum(jnp.where(onehot, offs_scr[pl.ds(j, 1), :], 0.0), axis=1, keepdims=True) + meta[:, 3:4]
        rows = _row_of(jnp.where(lane == 0, pos, 0.0))
        pos_ref[j] = rows[0:1, :].astype(jnp.int32)
        return carry

    lax.fori_loop(0, n_row_tiles, per_row_tile, 0)

    lane = lax.broadcasted_iota(jnp.int32, (PLAN_ROWS, LANE), 1)
    tile_idx = lax.broadcasted_iota(jnp.int32, (PLAN_ROWS, LANE), 0).astype(F32)[:, 0:1]
    n_used = jnp.sum(cls_tiles, axis=1, keepdims=True)
    start = jnp.minimum(tile_idx, n_used - 1.0) * tile
    tcls = jnp.sum(jnp.where((lane < N_CLASSES) & (ends <= start), 1.0, 0.0), axis=1, keepdims=True)
    tcls = jnp.minimum(tcls, N_CLASSES - 1.0)
    group = sum(jnp.where(tcls >= k * N_PAIRS, 1.0, 0.0) for k in range(1, N_EGROUPS))
    pair = tcls - group * N_PAIRS
    p_lo = jnp.where(pair >= 3, 1.0, 0.0) + jnp.where(pair >= 5, 1.0, 0.0)
    p_hi = jnp.where(pair == 0, 1.0, jnp.where((pair == 1) | (pair == 3), 2.0, 3.0))
    table = jnp.where(lane == TAB_LO, group * EXP_PER_GROUP + p_lo,
                      jnp.where(lane == TAB_HI, group * EXP_PER_GROUP + p_hi,
                                jnp.where(lane == TAB_USED, n_used, 0.0)))
    tab_ref[...] = _row_of(table).astype(jnp.int32)


def _plan(route, counts, tile):
    t = route.shape[0]
    nrt = t // ROW_TILE
    assert nrt <= PLAN_ROWS and t // tile + N_CLASSES <= PLAN_ROWS
    cnt = jnp.pad(counts[:, 0, :], ((0, PLAN_ROWS - nrt), (0, 0)))
    idx = jnp.arange(PLAN_ROWS)
    low = (idx[:, None] > idx[None, :]).astype(BF16)
    upp = (idx[:, None] < idx[None, :]).astype(BF16)
    pos, tab = pl.pallas_call(
        functools.partial(_plan_body, tile, nrt),
        out_shape=[jax.ShapeDtypeStruct((nrt, 1, ROW_TILE), jnp.int32),
                   jax.ShapeDtypeStruct((TAB_ROWS, PLAN_ROWS), jnp.int32)],
        scratch_shapes=[pltpu.VMEM((PLAN_ROWS, LANE), F32)],
        compiler_params=pltpu.CompilerParams(vmem_limit_bytes=_vmem_limit(_nbytes(route.shape, F32))),
        name="moe_plan",
    )(route, cnt, low, upp)
    return pos.reshape(t), tab


def _moe_body(tile, n_tok, pos_ref, tab_ref, hw_hbm, wgl_ref, wul_ref, wdl_ref, wgh_ref, wuh_ref, wdh_ref,
              y_hbm, buf, obuf, tok_smem, gsem, ssem):
    i = pl.program_id(0)
    n_used = tab_ref[TAB_USED, 0]
    slot = i % 2
    other = 1 - slot
    n_points = 16
    per_point = tile // n_points

    def gather_start(block, r, sl):
        src = jnp.minimum(tok_smem[block * tile + r], n_tok - 1)
        pltpu.make_async_copy(hw_hbm.at[pl.ds(src, 1)], buf.at[sl, pl.ds(r, 1)], gsem.at[sl]).start()

    def scatter_start(block, r, sl):
        dst = tok_smem[block * tile + r]
        pltpu.make_async_copy(obuf.at[sl, pl.ds(r, 1)], y_hbm.at[pl.ds(dst, 1)], ssem.at[sl]).start()

    def gather_wait(sl):
        pltpu.make_async_copy(hw_hbm.at[pl.ds(0, tile)], buf.at[sl], gsem.at[sl]).wait()

    def scatter_wait(sl):
        pltpu.make_async_copy(obuf.at[sl], y_hbm.at[pl.ds(0, tile)], ssem.at[sl]).wait()

    def for_rows(n, fn):
        lax.fori_loop(0, n, lambda r, carry: (fn(r), carry)[1], 0, unroll=8)

    @pl.when(i < n_used)
    def _():
        @pl.when(i == 0)
        def _():
            def spare(k):
                tok_smem[k] = n_tok + lax.bitwise_and(k, tile - 1)
            for_rows(tok_smem.shape[0], spare)

            def place(t):
                tok_smem[tile + pos_ref[t]] = t
            for_rows(n_tok, place)
            obuf[1] = jnp.zeros((tile, D_MODEL), F32)
            for_rows(tile, lambda r: gather_start(1, r, 0))

        gather_wait(slot)
        nxt = jnp.minimum(i + 1, n_used - 1)
        issued = [0]

        def issue_point():
            for r in range(issued[0] * per_point, (issued[0] + 1) * per_point):
                gather_start(nxt + 1, r, other)
                scatter_start(i, r, other)
            issued[0] += 1

        xw = buf[slot]
        xb = xw[:, :D_MODEL].astype(BF16)
        half = D_FF // 2
        acts = []
        for wg_ref, wu_ref in ((wgl_ref, wul_ref), (wgh_ref, wuh_ref)):
            parts = []
            for n in range(2):
                gate = _dot(xb, wg_ref[:, n * half:(n + 1) * half])
                issue_point()
                up = _dot(xb, wu_ref[:, n * half:(n + 1) * half])
                issue_point()
                parts.append((jax.nn.silu(gate) * up).astype(BF16))
            acts.append(jnp.concatenate(parts, axis=1))
        quarter = D_MODEL // 4
        for n in range(4):
            cols = slice(n * quarter, (n + 1) * quarter)
            d_lo = _dot(acts[0], wdl_ref[:, cols])
            issue_point()
            d_hi = _dot(acts[1], wdh_ref[:, cols])
            issue_point()
            obuf[slot, :, cols] = d_lo * xw[:, D_MODEL:D_MODEL + 1] + d_hi * xw[:, D_MODEL + 1:D_MODEL + 2]
        assert issued[0] == n_points
        scatter_wait(other)

        @pl.when(i == n_used - 1)
        def _():
            for_rows(tile, lambda r: scatter_start(i + 1, r, slot))
            scatter_wait(slot)
            gather_wait(other)


def _moe(hw, pos, tab, lp, tile):
    t = hw.shape[0]
    n_tiles = t // tile + N_CLASSES
    wspec = lambda shape, row: pl.BlockSpec((None,) + shape, lambda i, pos, tab: (tab[row, i], 0, 0))
    in_specs = [pl.BlockSpec(memory_space=pl.ANY)]
    for row in (TAB_LO, TAB_HI):
        in_specs += [wspec((D_MODEL, D_FF), row), wspec((D_MODEL, D_FF), row), wspec((D_FF, D_MODEL), row)]
    block_bytes = (6 * _nbytes((D_MODEL, D_FF), BF16) + _nbytes((tile, HW_W + D_MODEL), F32)
                   + 4 * _nbytes((tile, D_MODEL), F32))
    grid_spec = pltpu.PrefetchScalarGridSpec(
        num_scalar_prefetch=2, grid=(n_tiles,), in_specs=in_specs,
        out_specs=pl.BlockSpec(memory_space=pl.ANY),
        scratch_shapes=[pltpu.VMEM((2, tile, HW_W), F32), pltpu.VMEM((2, tile, D_MODEL), F32),
                        pltpu.SMEM(((n_tiles + 1) * tile,), jnp.int32),
                        pltpu.SemaphoreType.DMA((2,)), pltpu.SemaphoreType.DMA((2,))])
    return pl.pallas_call(
        functools.partial(_moe_body, tile, t), grid_spec=grid_spec,
        out_shape=jax.ShapeDtypeStruct((t + tile, D_MODEL), F32),
        compiler_params=_params(("arbitrary",), block_bytes), name="moe",
    )(pos, tab, hw, lp["w_gate"], lp["w_up"], lp["w_down"], lp["w_gate"], lp["w_up"], lp["w_down"])


def _final_body(x_ref, y_ref, g_ref, o_ref):
    o_ref[...] = _rmsnorm(x_ref[...] + y_ref[...], g_ref[...])


def _final_norm(x, y, g):
    t = x.shape[0]
    row = pl.BlockSpec((ROW_TILE, D_MODEL), lambda i: (i, 0))
    return pl.pallas_call(
        _final_body, grid=(t // ROW_TILE,), in_specs=[row, row, _full((1, D_MODEL))], out_specs=row,
        out_shape=jax.ShapeDtypeStruct((t, D_MODEL), F32),
        compiler_params=_params(("arbitrary",), 3 * _nbytes((ROW_TILE, D_MODEL), F32)), name="final_norm",
    )(x, y, g)


def _layer_params(i, ln1_g, w_in, ret_gn_g, w_ret_out, conv_w, conv_b, dt_bias, a_log, d_skip, ssd_norm_g,
                  w_ssd_out, w_o, ln2_g, w_rg, w_re, w_gate, w_up, w_down):
    offs = [0]
    for s in IN_SIZES:
        offs.append(offs[-1] + s)
    w = w_in[i]
    pad_heads = lambda v: jnp.pad(v.astype(F32), (0, LANE - H_SSD)).reshape(1, LANE)
    router = jnp.pad(jnp.concatenate([w_rg[i], w_re[i]], axis=1),
                     ((0, 0), (0, META_W - N_EGROUPS - N_EXPERTS)))
    router_hi = router.astype(BF16)
    return {
        "ln1_g": ln1_g[i].reshape(1, D_MODEL),
        "w_in": [w[:, offs[0]:offs[4]].astype(BF16),
                 w[:, offs[4]:offs[5]].astype(BF16),
                 w[:, offs[5]:offs[6]].astype(BF16),
                 jnp.pad(w[:, offs[6]:offs[7]], ((0, 0), (0, LANE - H_SSD))).astype(BF16),
                 w[:, offs[7]:offs[9]].astype(BF16)],
        "ret_gn_g": ret_gn_g[i].reshape(1, RET_W),
        "w_ret_out": w_ret_out[i].astype(BF16),
        "conv_w": conv_w[i], "conv_b": conv_b[i].reshape(1, CONV_DIM),
        "dt_bias": pad_heads(dt_bias[i]), "a": pad_heads(-jnp.exp(a_log[i].astype(F32))),
        "d_skip": jnp.repeat(d_skip[i].astype(F32), P_SSD).reshape(1, SSD_W),
        "ssd_norm_g": ssd_norm_g[i].reshape(1, SSD_W),
        "w_ssd_out": w_ssd_out[i].astype(BF16), "w_o": w_o[i].astype(BF16),
        "ln2_g": ln2_g[i].reshape(1, D_MODEL),
        "router_hi": router_hi, "router_lo": (router - router_hi.astype(F32)).astype(BF16),
        "w_gate": w_gate[i].astype(BF16), "w_up": w_up[i].astype(BF16), "w_down": w_down[i].astype(BF16),
    }


def _trunk(x, start, states, layers, lnf_g):
    batch, length, _ = x.shape
    t = batch * length
    moe_tile = ROW_TILE
    x = x.reshape(t, D_MODEL)
    y = None
    rets, ssms, convs = [], [], []
    for i, lp in enumerate(layers):
        s_ret, s_ssm, s_conv = (None, None, None) if states is None else (s[i] for s in states)
        outs = _inproj(x, y, lp["ln1_g"], lp["w_in"])
        if y is not None:
            x, outs = outs[0], outs[1:]
        qkvr, z, xbc, dt, gates = outs
        ret, new_ret = _retention(qkvr, start, lp["ret_gn_g"], s_ret, batch, length)
        ssd, new_conv, new_ssm = _ssd(xbc, z, dt, lp, s_conv, s_ssm, batch, length)
        x, hw, route, counts = _merge(ret, ssd, gates, x, lp)
        pos, tab = _plan(route, counts, moe_tile)
        y = _moe(hw, pos, tab, lp, moe_tile)
        rets.append(new_ret)
        ssms.append(new_ssm)
        convs.append(new_conv)
    out = _final_norm(x, y, lnf_g.reshape(1, D_MODEL)).reshape(batch, length, D_MODEL)
    return out, jnp.stack(rets), jnp.stack(ssms), jnp.stack(convs)


def kernel(x_prompt, x_sample, state_ret, state_ssm, state_conv, ln1_g, w_in, ret_gn_g, w_ret_out, conv_w, conv_b, dt_bias, a_log, d_skip, ssd_norm_g, w_ssd_out, w_o, ln2_g, w_router_group, w_router_expert, w_e_gate, w_e_up, w_e_down, lnf_g):
    depth = w_in.shape[0]
    past_len = 16384.0
    layers = [_layer_params(i, ln1_g, w_in, ret_gn_g, w_ret_out, conv_w, conv_b, dt_bias, a_log, d_skip,
                            ssd_norm_g, w_ssd_out, w_o, ln2_g, w_router_group, w_router_expert,
                            w_e_gate, w_e_up, w_e_down) for i in range(depth)]
    y_p, p_ret, p_ssm, p_conv = _trunk(x_prompt, 0.0, None, layers, lnf_g)
    y_s, s_ret, s_ssm, s_conv = _trunk(x_sample, past_len, (state_ret, state_ssm, state_conv), layers, lnf_g)
    return (y_p, y_s, p_ret, p_ssm, p_conv, s_ret, s_ssm, s_conv)
```

```python
import functools
import math

import jax
import jax.numpy as jnp
from jax import lax
from jax.experimental import pallas as pl
from jax.experimental.pallas import tpu as pltpu

F32 = jnp.float32
BF16 = jnp.bfloat16

D_MODEL = 1024
H_RET, DK_RET, DV_RET = 4, 128, 128
RET_W = H_RET * DV_RET
ROPE_BASE = 10000.0
H_SSD, P_SSD, G_SSD, N_SSD = 16, 64, 2, 64
R_SSD = H_SSD // G_SSD
SSD_W = H_SSD * P_SSD
GRP_W = SSD_W // G_SSD
CONV_K = 4
CONV_DIM = SSD_W + 2 * G_SSD * N_SSD
CHUNK = 128
N_EGROUPS, EXP_PER_GROUP = 4, 4
N_EXPERTS = N_EGROUPS * EXP_PER_GROUP
N_PAIRS = EXP_PER_GROUP * (EXP_PER_GROUP - 1) // 2
N_CLASSES = N_EGROUPS * N_PAIRS
D_FF = 512
EPS = 1e-6
IN_SIZES = [RET_W, RET_W, RET_W, RET_W, SSD_W, CONV_DIM, H_SSD, D_MODEL, D_MODEL]

LANE = 128
SUBLANE = 8
MIB = 1024 * 1024

ROW_TILE = 256
META_W = LANE
TOKEN_TILE_ROWS = D_MODEL // LANE
assert TOKEN_TILE_ROWS == SUBLANE
PACKED_ROWS = D_MODEL // (2 * LANE)
HIGH_HALF = 0xFFFF0000
SAMPLE_SEQS_PER_STEP = 8


def _vmem_limit(block_bytes):
    return int(min(2 * block_bytes + 16 * MIB, 56 * MIB))


def _params(semantics, block_bytes):
    return pltpu.CompilerParams(dimension_semantics=semantics,
                                vmem_limit_bytes=_vmem_limit(block_bytes))


def _nbytes(shape, dtype):
    return math.prod(shape) * jnp.dtype(dtype).itemsize


def _full(shape):
    return pl.BlockSpec(shape, lambda *_: (0,) * len(shape))


def _dot(a, b):
    return jnp.dot(a, b, preferred_element_type=F32)


def _dot_nt(a, b):
    return lax.dot_general(a, b, (((1,), (1,)), ((), ())), preferred_element_type=F32)


def _dot_tn(a, b):
    return lax.dot_general(a, b, (((0,), (0,)), ((), ())), preferred_element_type=F32)


def _split3(a):
    hi = a.astype(BF16)
    r1 = a - hi.astype(F32)
    mid = r1.astype(BF16)
    lo = (r1 - mid.astype(F32)).astype(BF16)
    return hi, mid, lo


def _rmsnorm(x, g):
    r = lax.rsqrt(jnp.mean(x * x, axis=-1, keepdims=True) + EPS)
    return x * r * g


def _token_rows(ref, rows):
    return jnp.concatenate([ref[pl.ds(k, rows, stride=SUBLANE), :] for k in range(TOKEN_TILE_ROWS)], axis=1)


def _token_spec(row_offset):
    first = row_offset // ROW_TILE
    return pl.BlockSpec((ROW_TILE * TOKEN_TILE_ROWS, LANE), lambda i: (i + first, 0))


def _inproj_body(has_add, *refs):
    n_in = 8 if has_add else 7
    ins, outs = refs[:n_in], refs[n_in:]
    if has_add:
        x_ref, y_ref, g_ref = ins[:3]
        x = x_ref[...] + _token_rows(y_ref, ROW_TILE)
        outs[0][...] = x
        outs = outs[1:]
    else:
        x_ref, g_ref = ins[:2]
        x = x_ref[...]
    w_refs = ins[-5:]
    h = _rmsnorm(x, g_ref[...]).astype(BF16)
    for w_ref, o_ref in zip(w_refs, outs, strict=True):
        o_ref[...] = _dot(h, w_ref[...])


def _inproj(x, y, y_offset, g, ws):
    t = x.shape[0]
    has_add = y is not None
    widths = [w.shape[1] for w in ws]
    row = lambda w: pl.BlockSpec((ROW_TILE, w), lambda i: (i, 0))
    in_specs = [row(D_MODEL)] + ([_token_spec(y_offset)] if has_add else []) + [_full((1, D_MODEL))]
    in_specs += [_full(w.shape) for w in ws]
    out_shape = [jax.ShapeDtypeStruct((t, w), F32) for w in widths]
    out_specs = [row(w) for w in widths]
    if has_add:
        out_shape = [jax.ShapeDtypeStruct((t, D_MODEL), F32)] + out_shape
        out_specs = [row(D_MODEL)] + out_specs
    block_bytes = (sum(_nbytes(w.shape, BF16) for w in ws)
                   + _nbytes((ROW_TILE, sum(widths) + 3 * D_MODEL), F32))
    args = ([x, y] if has_add else [x]) + [g] + list(ws)
    return pl.pallas_call(
        functools.partial(_inproj_body, has_add),
        grid=(t // ROW_TILE,), in_specs=in_specs, out_specs=out_specs, out_shape=out_shape,
        compiler_params=_params(("arbitrary",), block_bytes), name="inproj",
    )(*args)


def _retention_body(c, nb, nc, has_state, *refs):
    refs = list(refs)
    qkvr_ref, cos_ref, sin_ref, dmat_ref, qdec_ref, kdec_ref, cdec_ref, gn_ref = refs[:8]
    refs = refs[8:]
    if has_state:
        s0_ref = refs.pop(0)
    o_ref, ns_ref, st_scr = refs
    ci = pl.program_id(1)

    @pl.when(ci == 0)
    def _():
        if has_state:
            st_scr[...] = s0_ref[...]
        else:
            st_scr[...] = jnp.zeros(st_scr.shape, F32)

    for s in range(nb):
        rows = slice(s * c, (s + 1) * c)
        cosv, sinv = cos_ref[rows, :], sin_ref[rows, :]
        for h in range(H_RET):
            col = lambda j: slice(j * RET_W + h * DK_RET, j * RET_W + (h + 1) * DK_RET)
            q, k = qkvr_ref[rows, col(0)], qkvr_ref[rows, col(1)]
            v, rg = qkvr_ref[rows, col(2)], qkvr_ref[rows, col(3)]
            qr = q * cosv + pltpu.roll(q, DK_RET // 2, axis=1) * sinv
            kr = (k * cosv + pltpu.roll(k, DK_RET // 2, axis=1) * sinv) * (DK_RET ** -0.5)
            qb, kb, vb = qr.astype(BF16), kr.astype(BF16), v.astype(BF16)
            state = st_scr[s, h]
            sc = _dot_nt(qb, kb) * dmat_ref[h]
            o = _dot(sc.astype(BF16), vb) + _dot(qb, state.astype(BF16)) * qdec_ref[h]
            st_scr[s, h] = cdec_ref[h] * state + _dot_tn((kr * kdec_ref[h]).astype(BF16), vb)
            mu = jnp.mean(o, axis=-1, keepdims=True)
            d = o - mu
            var = jnp.mean(d * d, axis=-1, keepdims=True)
            on = d * lax.rsqrt(var + EPS) * gn_ref[:, h * DV_RET:(h + 1) * DV_RET]
            o_ref[rows, h * DV_RET:(h + 1) * DV_RET] = (jax.nn.silu(rg) * on).astype(o_ref.dtype)

    @pl.when(ci == nc - 1)
    def _():
        ns_ref[...] = st_scr[...]


def _retention(qkvr, start, gn_g, s0, batch, length):
    c = math.gcd(length, CHUNK)
    nc = length // c
    nb = 1 if nc > 1 else SAMPLE_SEQS_PER_STEP
    assert batch % nb == 0
    has_state = s0 is not None
    t = batch * length
    half = DK_RET // 2
    pos = start + jnp.arange(length, dtype=F32)
    inv = ROPE_BASE ** (-jnp.arange(half, dtype=F32) / half)
    ang = pos[:, None] * inv[None, :]
    cosv = jnp.tile(jnp.concatenate([jnp.cos(ang), jnp.cos(ang)], axis=1), (nb, 1))
    sinv = jnp.tile(jnp.concatenate([-jnp.sin(ang), jnp.sin(ang)], axis=1), (nb, 1))
    log_g = jnp.log1p(-jnp.exp2(-5.0 - jnp.arange(H_RET, dtype=F32)))
    idx = jnp.arange(c, dtype=F32)
    rel = idx[:, None] - idx[None, :]
    causal = rel >= 0
    dmat = jnp.where(causal[None], jnp.exp(jnp.where(causal, rel, 0.0)[None] * log_g[:, None, None]), 0.0)
    qdec = jnp.broadcast_to(jnp.exp((idx[None, :] + 1.0) * log_g[:, None])[:, :, None], (H_RET, c, LANE))
    kdec = jnp.broadcast_to(jnp.exp((c - 1.0 - idx[None, :]) * log_g[:, None])[:, :, None], (H_RET, c, LANE))
    cdec = jnp.exp(c * log_g)

    rows = nb * c
    state_spec = pl.BlockSpec((nb, H_RET, DK_RET, DV_RET), lambda b, ci: (b, 0, 0, 0))
    in_specs = [
        pl.BlockSpec((rows, 4 * RET_W), lambda b, ci: (b * nc + ci, 0)),
        pl.BlockSpec((rows, LANE), lambda b, ci: (ci, 0)),
        pl.BlockSpec((rows, LANE), lambda b, ci: (ci, 0)),
        _full((H_RET, c, c)), _full((H_RET, c, LANE)), _full((H_RET, c, LANE)),
        pl.BlockSpec(memory_space=pltpu.SMEM),
        _full((1, RET_W)),
    ]
    args = [qkvr, cosv, sinv, dmat, qdec, kdec, cdec, gn_g]
    if has_state:
        in_specs.append(state_spec)
        args.append(s0)
    block_bytes = (_nbytes((rows, 4 * RET_W + 2 * LANE + RET_W), F32)
                   + 3 * _nbytes((nb, H_RET, DK_RET, DV_RET), F32) + 3 * _nbytes((H_RET, c, LANE), F32))
    return pl.pallas_call(
        functools.partial(_retention_body, c, nb, nc, has_state),
        grid=(batch // nb, nc), in_specs=in_specs,
        out_specs=[pl.BlockSpec((rows, RET_W), lambda b, ci: (b * nc + ci, 0)), state_spec],
        out_shape=[jax.ShapeDtypeStruct((t, RET_W), BF16),
                   jax.ShapeDtypeStruct((batch, H_RET, DK_RET, DV_RET), F32)],
        scratch_shapes=[pltpu.VMEM((nb, H_RET, DK_RET, DV_RET), F32)],
        compiler_params=_params(("arbitrary", "arbitrary"), block_bytes), name="retention",
    )(*args)


def _ssd_body(c, nb, nc, has_state, *refs):
    refs = list(refs)
    (xbc_ref, z_ref, dt_ref, cw_ref, cb_ref, dtb_ref, a_ref, dsk_ref, ng_ref,
     tri_ref, eye_ref, exp_ref, sel_ref) = refs[:13]
    refs = refs[13:]
    if has_state:
        sconv_ref, sssm_ref = refs.pop(0), refs.pop(0)
    y_ref, nconv_ref, nssm_ref, ext_scr, st_scr = refs
    ci = pl.program_id(1)
    pad = SUBLANE
    hist = CONV_K - 1

    @pl.when(ci == 0)
    def _():
        ext_scr[:, 0:pad, :] = jnp.zeros((nb, pad, CONV_DIM), F32)
        if has_state:
            ext_scr[:, pad - hist:pad, :] = sconv_ref[...]
            st_scr[...] = sssm_ref[...]
        else:
            st_scr[...] = jnp.zeros(st_scr.shape, F32)

    causal = (lax.broadcasted_iota(jnp.int32, (c, c), 0) >= lax.broadcasted_iota(jnp.int32, (c, c), 1))
    left_half = (lax.broadcasted_iota(jnp.int32, (c, SSD_W), 1) % LANE) < P_SSD
    tri, eye, expand = tri_ref[...], eye_ref[...], exp_ref[...]

    for s in range(nb):
        rows = slice(s * c, (s + 1) * c)
        ext_scr[s, pad:pad + c, :] = xbc_ref[rows, :]
        acc = ext_scr[s, pad - hist:pad - hist + c, :] * cw_ref[0:1, :]
        for j in range(1, CONV_K):
            acc = acc + ext_scr[s, pad - hist + j:pad - hist + j + c, :] * cw_ref[j:j + 1, :]
        conv = jax.nn.silu(cb_ref[...] + acc)

        @pl.when(ci == nc - 1)
        def _():
            nconv_ref[s] = ext_scr[s, pad + c - hist:pad + c, :]

        ext_scr[s, 0:pad, :] = ext_scr[s, c:c + pad, :]

        xs = conv[:, :SSD_W]
        bm = conv[:, SSD_W:SSD_W + G_SSD * N_SSD]
        cm = conv[:, SSD_W + G_SSD * N_SSD:]
        dt = jax.nn.softplus(dt_ref[rows, :] + dtb_ref[...])
        dta = dt * a_ref[...]
        cum = sum(_dot(tri, p) for p in _split3(dta))
        cum_parts = _split3(cum)
        cum_t = sum(_dot_nt(eye, p) for p in cum_parts)
        cum_e = sum(_dot(p, expand) for p in cum_parts)
        dt_e = sum(_dot(p, expand) for p in _split3(dt))
        last_e = cum_e[c - 1:c, :]
        xdt = xs * dt_e
        xw = (xs * (jnp.exp(last_e - cum_e) * dt_e)).astype(BF16)
        x_l = jnp.where(left_half, xdt, 0.0).astype(BF16)
        x_r = jnp.where(left_half, 0.0, xdt).astype(BF16)

        y_parts, off_parts = [], []
        for g in range(G_SSD):
            bm_g = bm[:, g * N_SSD:(g + 1) * N_SSD].astype(BF16)
            cm_g = cm[:, g * N_SSD:(g + 1) * N_SSD].astype(BF16)
            cb = _dot_nt(cm_g, bm_g)
            state = st_scr[s, g]
            off_parts.append(_dot_nt(cm_g, state.astype(BF16)))
            decay = jnp.exp(jnp.sum(sel_ref[g] * cum[c - 1:c, :], axis=1, keepdims=True))
            st_scr[s, g] = decay * state + _dot_tn(xw[:, g * GRP_W:(g + 1) * GRP_W], bm_g)
            for jj in range(R_SSD // 2):
                j = g * (R_SSD // 2) + jj
                w = []
                for hd in (2 * j, 2 * j + 1):
                    seg = cum[:, hd:hd + 1] - cum_t[hd:hd + 1, :]
                    w.append((jnp.exp(jnp.where(causal, seg, -jnp.inf)) * cb).astype(BF16))
                cols = slice(j * LANE, (j + 1) * LANE)
                y_parts.append(_dot(w[0], x_l[:, cols]) + _dot(w[1], x_r[:, cols]))
        y = jnp.concatenate(y_parts, axis=1) + jnp.exp(cum_e) * jnp.concatenate(off_parts, axis=1)
        y = (y + dsk_ref[...] * xs) * jax.nn.silu(z_ref[rows, :])
        normed = []
        for g in range(G_SSD):
            yg = y[:, g * GRP_W:(g + 1) * GRP_W]
            normed.append(yg * lax.rsqrt(jnp.mean(yg * yg, axis=-1, keepdims=True) + EPS))
        y_ref[rows, :] = (jnp.concatenate(normed, axis=1) * ng_ref[...]).astype(y_ref.dtype)

    @pl.when(ci == nc - 1)
    def _():
        nssm_ref[...] = st_scr[...]


def _ssd(xbc, z, dt, lp, s_conv, s_ssm, batch, length):
    c = math.gcd(length, CHUNK)
    nc = length // c
    nb = 1 if nc > 1 else SAMPLE_SEQS_PER_STEP
    assert batch % nb == 0 and c % SUBLANE == 0
    has_state = s_conv is not None
    t = batch * length
    rows = nb * c
    tri = (jnp.arange(c)[:, None] >= jnp.arange(c)[None, :]).astype(BF16)
    eye = jnp.eye(LANE, dtype=BF16)
    head_of_ch = jnp.arange(SSD_W) // P_SSD
    expand = (jnp.arange(LANE)[:, None] == head_of_ch[None, :]).astype(BF16)
    sel = (head_of_ch[:, None] == jnp.arange(LANE)[None, :]).astype(F32).reshape(G_SSD, GRP_W, LANE)

    seq = lambda shape: pl.BlockSpec((nb,) + shape, lambda b, ci: (b,) + (0,) * len(shape))
    rowb = lambda w: pl.BlockSpec((rows, w), lambda b, ci: (b * nc + ci, 0))
    in_specs = [rowb(CONV_DIM), rowb(SSD_W), rowb(LANE),
                _full((CONV_K, CONV_DIM)), _full((1, CONV_DIM)), _full((1, LANE)), _full((1, LANE)),
                _full((1, SSD_W)), _full((1, SSD_W)),
                _full((c, c)), _full((LANE, LANE)), _full((LANE, SSD_W)), _full((G_SSD, GRP_W, LANE))]
    args = [xbc, z, dt, lp["conv_w"], lp["conv_b"], lp["dt_bias"], lp["a"], lp["d_skip"], lp["ssd_norm_g"],
            tri, eye, expand, sel]
    if has_state:
        in_specs += [seq((CONV_K - 1, CONV_DIM)), seq((G_SSD, GRP_W, N_SSD))]
        args += [s_conv, s_ssm.reshape(batch, G_SSD, GRP_W, N_SSD)]
    block_bytes = (_nbytes((rows, CONV_DIM + 2 * SSD_W + LANE), F32) + _nbytes((nb, c + SUBLANE, CONV_DIM), F32)
                   + 3 * _nbytes((nb, SSD_W, N_SSD), F32) + _nbytes((LANE, SSD_W), F32)
                   + 12 * _nbytes((c, SSD_W), F32))
    y, nconv, nssm = pl.pallas_call(
        functools.partial(_ssd_body, c, nb, nc, has_state),
        grid=(batch // nb, nc), in_specs=in_specs,
        out_specs=[rowb(SSD_W), seq((CONV_K - 1, CONV_DIM)), seq((G_SSD, GRP_W, N_SSD))],
        out_shape=[jax.ShapeDtypeStruct((t, SSD_W), BF16),
                   jax.ShapeDtypeStruct((batch, CONV_K - 1, CONV_DIM), F32),
                   jax.ShapeDtypeStruct((batch, G_SSD, GRP_W, N_SSD), F32)],
        scratch_shapes=[pltpu.VMEM((nb, c + SUBLANE, CONV_DIM), F32),
                        pltpu.VMEM((nb, G_SSD, GRP_W, N_SSD), F32)],
        compiler_params=_params(("arbitrary", "arbitrary"), block_bytes), name="ssd",
    )(*args)
    return y, nconv, nssm.reshape(batch, H_SSD, P_SSD, N_SSD)


def _first_argmax(vals):
    best = vals[0]
    for v in vals[1:]:
        best = jnp.maximum(best, v)
    idx = jnp.full(best.shape, len(vals) - 1, jnp.int32)
    for i in range(len(vals) - 2, -1, -1):
        idx = jnp.where(vals[i] == best, i, idx)
    return best, idx


def _merge_body(bounds, *refs):
    n_groups = len(bounds)
    acts, refs = refs[:4 * n_groups], refs[4 * n_groups:]
    weights, refs = refs[:7], refs[7:]
    x2_refs, shared = refs[:n_groups], refs[n_groups:]
    i = pl.program_id(0)
    for gi, (first, count) in enumerate(bounds):
        @pl.when((i >= first) & (i < first + count))
        def _():
            _merge_tile(*acts[4 * gi:4 * gi + 4], *weights, x2_refs[gi], *shared)


def _merge_tile(ret_ref, ssd_ref, gates_ref, x_ref, wr_ref, ws_ref, wo_ref, g2_ref, rhi_ref, rlo_ref, low_ref,
                x2_ref, hw_ref, route_ref, cnt_ref):
    ret_y = _dot(ret_ref[...], wr_ref[...])
    ssd_y = _dot(ssd_ref[...], ws_ref[...])
    m = (jax.nn.sigmoid(gates_ref[:, :D_MODEL]) * ret_y + jax.nn.sigmoid(gates_ref[:, D_MODEL:]) * ssd_y)
    x2 = x_ref[...] + _dot(m.astype(BF16), wo_ref[...])
    x2_ref[...] = x2
    h2 = _rmsnorm(x2, g2_ref[...])
    bits = pltpu.bitcast(h2.astype(BF16).astype(F32), jnp.uint32)
    half = D_MODEL // 2
    word = lax.bitwise_or(lax.bitwise_and(bits[:, half:], jnp.uint32(HIGH_HALF)),
                          lax.shift_right_logical(bits[:, :half], jnp.uint32(16)))
    n_rows = h2.shape[0]
    for k in range(PACKED_ROWS):
        hw_ref[pl.ds(k, n_rows, stride=SUBLANE), :] = word[:, k * LANE:(k + 1) * LANE]

    h_hi = h2.astype(BF16)
    h_lo = (h2 - h_hi.astype(F32)).astype(BF16)
    logits = _dot(h_hi, rhi_ref[...]) + (_dot(h_hi, rlo_ref[...]) + _dot(h_lo, rhi_ref[...]))
    colv = lambda i: logits[:, i:i + 1]
    glog = [colv(i) for i in range(N_EGROUPS)]
    gmax, gidx = _first_argmax(glog)
    den = glog[0] * 0.0
    for v in glog:
        den = den + jnp.exp(v - gmax)
    gw = 1.0 / den
    sel = []
    for j in range(EXP_PER_GROUP):
        v = colv(N_EGROUPS + (N_EGROUPS - 1) * EXP_PER_GROUP + j)
        for g in range(N_EGROUPS - 2, -1, -1):
            v = jnp.where(gidx == g, colv(N_EGROUPS + g * EXP_PER_GROUP + j), v)
        sel.append(v)
    v1, i1 = _first_argmax(sel)
    v2, i2 = _first_argmax([jnp.where(i1 == j, -jnp.inf, sel[j]) for j in range(EXP_PER_GROUP)])
    e = jnp.exp(v2 - v1)
    w1 = gw / (1.0 + e)
    w2 = gw * e / (1.0 + e)
    first_low = i1 < i2
    lo = jnp.where(first_low, i1, i2)
    hi = jnp.where(first_low, i2, i1)
    pair = jnp.where(lo == 0, hi - 1, jnp.where(lo == 1, hi + 1, N_PAIRS - 1))
    cls = gidx * N_PAIRS + pair
    w_lo = jnp.where(first_low, w1, w2)
    w_hi = jnp.where(first_low, w2, w1)

    rows = logits.shape[0]
    lane = lax.broadcasted_iota(jnp.int32, (rows, META_W), 1)
    onehot = (lane == cls).astype(F32)
    before = _dot(low_ref[...], onehot.astype(BF16))
    rank = jnp.sum(onehot * before, axis=1, keepdims=True)
    meta = jnp.where(lane == 0, w_lo, jnp.where(lane == 1, w_hi, jnp.where(
        lane == 2, cls.astype(F32), jnp.where(lane == 3, rank, 0.0))))
    meta_bits = pltpu.bitcast(meta, jnp.uint32)
    for k in range(PACKED_ROWS, TOKEN_TILE_ROWS):
        hw_ref[pl.ds(k, n_rows, stride=SUBLANE), :] = meta_bits
    route_ref[...] = meta
    cnt_ref[0] = jnp.sum(onehot, axis=0, keepdims=True)


def _merge(group_acts, lp):
    counts = [a[3].shape[0] // ROW_TILE for a in group_acts]
    bounds = tuple((sum(counts[:gi]), counts[gi]) for gi in range(len(counts)))
    n_tiles = sum(counts)
    total_rows = n_tiles * ROW_TILE
    low = (jnp.arange(ROW_TILE)[:, None] > jnp.arange(ROW_TILE)[None, :]).astype(BF16)
    ws = [lp["w_ret_out"], lp["w_ssd_out"], lp["w_o"], lp["ln2_g"], lp["router_hi"], lp["router_lo"], low]
    block_bytes = (sum(_nbytes(w.shape, w.dtype) for w in ws)
                   + len(counts) * (_nbytes((ROW_TILE, RET_W + SSD_W), BF16) + _nbytes((ROW_TILE, 4 * D_MODEL), F32))
                   + _nbytes((ROW_TILE, 3 * D_MODEL), F32))

    def group_row(width, first, count):
        return pl.BlockSpec((ROW_TILE, width), lambda i: (jnp.clip(i - first, 0, count - 1), 0))

    in_specs, args, x2_specs, x2_shapes = [], [], [], []
    for (first, count), acts in zip(bounds, group_acts, strict=True):
        in_specs += [group_row(w, first, count) for w in (RET_W, SSD_W, 2 * D_MODEL, D_MODEL)]
        args += list(acts)
        x2_specs.append(group_row(D_MODEL, first, count))
        x2_shapes.append(jax.ShapeDtypeStruct((count * ROW_TILE, D_MODEL), F32))
    in_specs += [_full(w.shape) for w in ws]
    args += ws
    outs = pl.pallas_call(
        functools.partial(_merge_body, bounds), grid=(n_tiles,), in_specs=in_specs,
        out_specs=x2_specs + [_token_spec(0), pl.BlockSpec((ROW_TILE, META_W), lambda i: (i, 0)),
                              pl.BlockSpec((1, 1, META_W), lambda i: (i, 0, 0))],
        out_shape=x2_shapes + [jax.ShapeDtypeStruct((total_rows * TOKEN_TILE_ROWS, LANE), jnp.uint32),
                               jax.ShapeDtypeStruct((total_rows, META_W), F32),
                               jax.ShapeDtypeStruct((n_tiles, 1, META_W), F32)],
        compiler_params=_params(("arbitrary",), block_bytes), name="merge_router",
    )(*args)
    return outs[:len(counts)], outs[len(counts):]


PLAN_ROWS = LANE
TAB_ROWS = SUBLANE
TAB_LO, TAB_HI, TAB_USED = 0, 1, 2
BYTE = 256.0


def _row_of(col_vals):
    pick = (lax.broadcasted_iota(jnp.int32, (TAB_ROWS, LANE), 0)
            == lax.broadcasted_iota(jnp.int32, (TAB_ROWS, LANE), 1)).astype(BF16)
    hi = jnp.floor(col_vals * (1.0 / BYTE))
    lo = col_vals - hi * BYTE
    return BYTE * _dot_nt(pick, hi.astype(BF16)) + _dot_nt(pick, lo.astype(BF16))


def _plan_body(tile, n_row_tiles, route_ref, cnt_ref, low_ref, upp_ref, pos_ref, tab_ref, offs_scr):
    cnt = cnt_ref[...]
    tot = jnp.sum(cnt, axis=0, keepdims=True)
    cls_tiles = jnp.floor((tot + (tile - 1.0)) * (1.0 / tile))
    first_tile = _dot(jnp.broadcast_to(cls_tiles, (SUBLANE, LANE)).astype(BF16), upp_ref[...])[0:1, :]
    base = first_tile * tile
    ends = base + cls_tiles * tile
    offs_scr[...] = base + _dot(low_ref[...], cnt.astype(BF16))

    def per_row_tile(j, carry):
        meta = route_ref[pl.ds(j * ROW_TILE, ROW_TILE), :]
        lane = lax.broadcasted_iota(jnp.int32, (ROW_TILE, META_W), 1)
        onehot = lane.astype(F32) == meta[:, 2:3]
        pos = jnp.sum(jnp.where(onehot, offs_scr[pl.ds(j, 1), :], 0.0), axis=1, keepdims=True) + meta[:, 3:4]
        rows = _row_of(jnp.where(lane == 0, pos, 0.0))
        pos_ref[j] = rows[0:1, :].astype(jnp.int32)
        return carry

    lax.fori_loop(0, n_row_tiles, per_row_tile, 0)

    lane = lax.broadcasted_iota(jnp.int32, (PLAN_ROWS, LANE), 1)
    tile_idx = lax.broadcasted_iota(jnp.int32, (PLAN_ROWS, LANE), 0).astype(F32)[:, 0:1]
    n_used = jnp.sum(cls_tiles, axis=1, keepdims=True)
    start = jnp.minimum(tile_idx, n_used - 1.0) * tile
    tcls = jnp.sum(jnp.where((lane < N_CLASSES) & (ends <= start), 1.0, 0.0), axis=1, keepdims=True)
    tcls = jnp.minimum(tcls, N_CLASSES - 1.0)
    group = sum(jnp.where(tcls >= k * N_PAIRS, 1.0, 0.0) for k in range(1, N_EGROUPS))
    pair = tcls - group * N_PAIRS
    p_lo = jnp.where(pair >= 3, 1.0, 0.0) + jnp.where(pair >= 5, 1.0, 0.0)
    p_hi = jnp.where(pair == 0, 1.0, jnp.where((pair == 1) | (pair == 3), 2.0, 3.0))
    table = jnp.where(lane == TAB_LO, group * EXP_PER_GROUP + p_lo,
                      jnp.where(lane == TAB_HI, group * EXP_PER_GROUP + p_hi,
                                jnp.where(lane == TAB_USED, n_used, 0.0)))
    tab_ref[...] = _row_of(table).astype(jnp.int32)


def _plan(route, counts, tile):
    t = route.shape[0]
    nrt = t // ROW_TILE
    assert nrt <= PLAN_ROWS and t // tile + N_CLASSES <= PLAN_ROWS
    cnt = jnp.pad(counts[:, 0, :], ((0, PLAN_ROWS - nrt), (0, 0)))
    idx = jnp.arange(PLAN_ROWS)
    low = (idx[:, None] > idx[None, :]).astype(BF16)
    upp = (idx[:, None] < idx[None, :]).astype(BF16)
    pos, tab = pl.pallas_call(
        functools.partial(_plan_body, tile, nrt),
        out_shape=[jax.ShapeDtypeStruct((nrt, 1, ROW_TILE), jnp.int32),
                   jax.ShapeDtypeStruct((TAB_ROWS, PLAN_ROWS), jnp.int32)],
        scratch_shapes=[pltpu.VMEM((PLAN_ROWS, LANE), F32)],
        compiler_params=pltpu.CompilerParams(vmem_limit_bytes=_vmem_limit(_nbytes(route.shape, F32))),
        name="moe_plan",
    )(route, cnt, low, upp)
    return pos.reshape(t), tab


def _moe_body(tile, n_tok, pos_ref, tab_ref, hw_hbm, wgl_ref, wul_ref, wdl_ref, wgh_ref, wuh_ref, wdh_ref,
              y_hbm, buf, obuf, tok_smem, gsem, ssem):
    i = pl.program_id(0)
    n_used = tab_ref[TAB_USED, 0]
    slot = i % 2
    other = 1 - slot
    n_points = 16
    per_point = tile // n_points

    tr = TOKEN_TILE_ROWS

    def token_tile(ref, token):
        start = token * tr if isinstance(token, int) else pl.multiple_of(token * tr, tr)
        return ref.at[pl.ds(start, tr)]

    def gather_start(block, r, sl):
        src = jnp.minimum(tok_smem[block * tile + r], n_tok - 1)
        pltpu.make_async_copy(token_tile(hw_hbm, src), token_tile(buf.at[sl], r), gsem.at[sl]).start()

    def scatter_start(block, r, sl):
        dst = tok_smem[block * tile + r]
        pltpu.make_async_copy(token_tile(obuf.at[sl], r), token_tile(y_hbm, dst), ssem.at[sl]).start()

    def gather_wait(sl):
        pltpu.make_async_copy(hw_hbm.at[pl.ds(0, tile * tr)], buf.at[sl], gsem.at[sl]).wait()

    def scatter_wait(sl):
        pltpu.make_async_copy(obuf.at[sl], y_hbm.at[pl.ds(0, tile * tr)], ssem.at[sl]).wait()

    def for_rows(n, fn):
        lax.fori_loop(0, n, lambda r, carry: (fn(r), carry)[1], 0, unroll=8)

    @pl.when(i < n_used)
    def _():
        @pl.when(i == 0)
        def _():
            def spare(k):
                tok_smem[k] = n_tok + lax.bitwise_and(k, tile - 1)
            for_rows(tok_smem.shape[0], spare)

            def place(t):
                tok_smem[tile + pos_ref[t]] = t
            for_rows(n_tok, place)
            obuf[1] = jnp.zeros(obuf.shape[1:], F32)
            for_rows(tile, lambda r: gather_start(1, r, 0))

        gather_wait(slot)
        nxt = jnp.minimum(i + 1, n_used - 1)
        issued = [0]

        def issue_point():
            for r in range(issued[0] * per_point, (issued[0] + 1) * per_point):
                gather_start(nxt + 1, r, other)
                scatter_start(i, r, other)
            issued[0] += 1

        def tile_row(k):
            return buf[slot, pl.ds(k, tile, stride=tr), :]

        words = [tile_row(k) for k in range(PACKED_ROWS)]
        low = [pltpu.bitcast(lax.shift_left(w, jnp.uint32(16)), F32).astype(BF16) for w in words]
        high = [pltpu.bitcast(lax.bitwise_and(w, jnp.uint32(HIGH_HALF)), F32).astype(BF16) for w in words]
        xb = jnp.concatenate(low + high, axis=1)
        meta = pltpu.bitcast(tile_row(PACKED_ROWS), F32)
        w_lo, w_hi = meta[:, 0:1], meta[:, 1:2]
        half = D_FF // 2
        acts = []
        for wg_ref, wu_ref in ((wgl_ref, wul_ref), (wgh_ref, wuh_ref)):
            parts = []
            for n in range(2):
                gate = _dot(xb, wg_ref[:, n * half:(n + 1) * half])
                issue_point()
                up = _dot(xb, wu_ref[:, n * half:(n + 1) * half])
                issue_point()
                parts.append((jax.nn.silu(gate) * up).astype(BF16))
            acts.append(jnp.concatenate(parts, axis=1))
        quarter = D_MODEL // 4
        for n in range(4):
            cols = slice(n * quarter, (n + 1) * quarter)
            d_lo = _dot(acts[0], wdl_ref[:, cols])
            issue_point()
            d_hi = _dot(acts[1], wdh_ref[:, cols])
            issue_point()
            out = d_lo * w_lo + d_hi * w_hi
            for kk in range(quarter // LANE):
                obuf[slot, pl.ds(n * (quarter // LANE) + kk, tile, stride=tr), :] = out[:, kk * LANE:(kk + 1) * LANE]
        assert issued[0] == n_points
        scatter_wait(other)

        @pl.when(i == n_used - 1)
        def _():
            for_rows(tile, lambda r: scatter_start(i + 1, r, slot))
            scatter_wait(slot)
            gather_wait(other)


def _moe(hw, pos, tab, lp, tile):
    t = hw.shape[0] // TOKEN_TILE_ROWS
    n_tiles = t // tile + N_CLASSES
    wspec = lambda shape, row: pl.BlockSpec((None,) + shape, lambda i, pos, tab: (tab[row, i], 0, 0))
    in_specs = [pl.BlockSpec(memory_space=pl.ANY)]
    for row in (TAB_LO, TAB_HI):
        in_specs += [wspec((D_MODEL, D_FF), row), wspec((D_MODEL, D_FF), row), wspec((D_FF, D_MODEL), row)]
    block_bytes = 6 * _nbytes((D_MODEL, D_FF), BF16) + 6 * _nbytes((tile, D_MODEL), F32)
    tile_rows = tile * TOKEN_TILE_ROWS
    grid_spec = pltpu.PrefetchScalarGridSpec(
        num_scalar_prefetch=2, grid=(n_tiles,), in_specs=in_specs,
        out_specs=pl.BlockSpec(memory_space=pl.ANY),
        scratch_shapes=[pltpu.VMEM((2, tile_rows, LANE), jnp.uint32), pltpu.VMEM((2, tile_rows, LANE), F32),
                        pltpu.SMEM(((n_tiles + 1) * tile,), jnp.int32),
                        pltpu.SemaphoreType.DMA((2,)), pltpu.SemaphoreType.DMA((2,))])
    return pl.pallas_call(
        functools.partial(_moe_body, tile, t), grid_spec=grid_spec,
        out_shape=jax.ShapeDtypeStruct(((t + tile) * TOKEN_TILE_ROWS, LANE), F32),
        compiler_params=_params(("arbitrary",), block_bytes), name="moe",
    )(pos, tab, hw, lp["w_gate"], lp["w_up"], lp["w_down"], lp["w_gate"], lp["w_up"], lp["w_down"])


def _final_body(x_ref, y_ref, g_ref, o_ref):
    o_ref[...] = _rmsnorm(x_ref[...] + _token_rows(y_ref, ROW_TILE), g_ref[...])


def _final_norm(x, y, y_offset, g):
    t = x.shape[0]
    row = pl.BlockSpec((ROW_TILE, D_MODEL), lambda i: (i, 0))
    return pl.pallas_call(
        _final_body, grid=(t // ROW_TILE,), in_specs=[row, _token_spec(y_offset), _full((1, D_MODEL))],
        out_specs=row, out_shape=jax.ShapeDtypeStruct((t, D_MODEL), F32),
        compiler_params=_params(("arbitrary",), 3 * _nbytes((ROW_TILE, D_MODEL), F32)), name="final_norm",
    )(x, y, g)


def _layer_params(i, ln1_g, w_in, ret_gn_g, w_ret_out, conv_w, conv_b, dt_bias, a_log, d_skip, ssd_norm_g,
                  w_ssd_out, w_o, ln2_g, w_rg, w_re, w_gate, w_up, w_down):
    offs = [0]
    for s in IN_SIZES:
        offs.append(offs[-1] + s)
    w = w_in[i]
    pad_heads = lambda v: jnp.pad(v.astype(F32), (0, LANE - H_SSD)).reshape(1, LANE)
    router = jnp.pad(jnp.concatenate([w_rg[i], w_re[i]], axis=1),
                     ((0, 0), (0, META_W - N_EGROUPS - N_EXPERTS)))
    router_hi = router.astype(BF16)
    return {
        "ln1_g": ln1_g[i].reshape(1, D_MODEL),
        "w_in": [w[:, offs[0]:offs[4]].astype(BF16),
                 w[:, offs[4]:offs[5]].astype(BF16),
                 w[:, offs[5]:offs[6]].astype(BF16),
                 jnp.pad(w[:, offs[6]:offs[7]], ((0, 0), (0, LANE - H_SSD))).astype(BF16),
                 w[:, offs[7]:offs[9]].astype(BF16)],
        "ret_gn_g": ret_gn_g[i].reshape(1, RET_W),
        "w_ret_out": w_ret_out[i].astype(BF16),
        "conv_w": conv_w[i], "conv_b": conv_b[i].reshape(1, CONV_DIM),
        "dt_bias": pad_heads(dt_bias[i]), "a": pad_heads(-jnp.exp(a_log[i].astype(F32))),
        "d_skip": jnp.repeat(d_skip[i].astype(F32), P_SSD).reshape(1, SSD_W),
        "ssd_norm_g": ssd_norm_g[i].reshape(1, SSD_W),
        "w_ssd_out": w_ssd_out[i].astype(BF16), "w_o": w_o[i].astype(BF16),
        "ln2_g": ln2_g[i].reshape(1, D_MODEL),
        "router_hi": router_hi, "router_lo": (router - router_hi.astype(F32)).astype(BF16),
        "w_gate": w_gate[i].astype(BF16), "w_up": w_up[i].astype(BF16), "w_down": w_down[i].astype(BF16),
    }


class _Group:
    def __init__(self, x, start, states, row_offset):
        self.batch, self.length, _ = x.shape
        self.x = x.reshape(self.batch * self.length, D_MODEL)
        self.start, self.states, self.row_offset = start, states, row_offset
        self.rets, self.ssms, self.convs = [], [], []


def kernel(x_prompt, x_sample, state_ret, state_ssm, state_conv, ln1_g, w_in, ret_gn_g, w_ret_out, conv_w, conv_b, dt_bias, a_log, d_skip, ssd_norm_g, w_ssd_out, w_o, ln2_g, w_router_group, w_router_expert, w_e_gate, w_e_up, w_e_down, lnf_g):
    depth = w_in.shape[0]
    past_len = 16384.0
    layers = [_layer_params(i, ln1_g, w_in, ret_gn_g, w_ret_out, conv_w, conv_b, dt_bias, a_log, d_skip,
                            ssd_norm_g, w_ssd_out, w_o, ln2_g, w_router_group, w_router_expert,
                            w_e_gate, w_e_up, w_e_down) for i in range(depth)]
    n_prompt = x_prompt.shape[0] * x_prompt.shape[1]
    groups = [_Group(x_prompt, 0.0, None, 0),
              _Group(x_sample, past_len, (state_ret, state_ssm, state_conv), n_prompt)]
    y = None
    for i, lp in enumerate(layers):
        group_acts = []
        for grp in groups:
            s_ret, s_ssm, s_conv = (None, None, None) if grp.states is None else (s[i] for s in grp.states)
            outs = _inproj(grp.x, y, grp.row_offset, lp["ln1_g"], lp["w_in"])
            if y is not None:
                grp.x, outs = outs[0], outs[1:]
            qkvr, z, xbc, dt, gates = outs
            ret, new_ret = _retention(qkvr, grp.start, lp["ret_gn_g"], s_ret, grp.batch, grp.length)
            ssd, new_conv, new_ssm = _ssd(xbc, z, dt, lp, s_conv, s_ssm, grp.batch, grp.length)
            grp.rets.append(new_ret)
            grp.ssms.append(new_ssm)
            grp.convs.append(new_conv)
            group_acts.append((ret, ssd, gates, grp.x))
        new_x, (hw, route, counts) = _merge(group_acts, lp)
        for grp, x2 in zip(groups, new_x, strict=True):
            grp.x = x2
        pos, tab = _plan(route, counts, ROW_TILE)
        y = _moe(hw, pos, tab, lp, ROW_TILE)
    outs = []
    for grp in groups:
        out = _final_norm(grp.x, y, grp.row_offset, lnf_g.reshape(1, D_MODEL))
        outs.append(out.reshape(grp.batch, grp.length, D_MODEL))
    states = []
    for grp in groups:
        states += [jnp.stack(grp.rets), jnp.stack(grp.ssms), jnp.stack(grp.convs)]
    return tuple(outs + states)
```

```python
import functools
import math

import jax
import jax.numpy as jnp
from jax import lax
from jax.experimental import pallas as pl
from jax.experimental.pallas import tpu as pltpu

F32 = jnp.float32
BF16 = jnp.bfloat16

D_MODEL = 1024
H_RET, DK_RET, DV_RET = 4, 128, 128
RET_W = H_RET * DV_RET
ROPE_BASE = 10000.0
H_SSD, P_SSD, G_SSD, N_SSD = 16, 64, 2, 64
R_SSD = H_SSD // G_SSD
SSD_W = H_SSD * P_SSD
GRP_W = SSD_W // G_SSD
CONV_K = 4
CONV_DIM = SSD_W + 2 * G_SSD * N_SSD
CHUNK = 128
N_EGROUPS, EXP_PER_GROUP = 4, 4
N_EXPERTS = N_EGROUPS * EXP_PER_GROUP
N_PAIRS = EXP_PER_GROUP * (EXP_PER_GROUP - 1) // 2
N_CLASSES = N_EGROUPS * N_PAIRS
D_FF = 512
EPS = 1e-6
IN_SIZES = [RET_W, RET_W, RET_W, RET_W, SSD_W, CONV_DIM, H_SSD, D_MODEL, D_MODEL]

LANE = 128
SUBLANE = 8
MIB = 1024 * 1024

ROW_TILE = 256
META_W = LANE
TOKEN_TILE_ROWS = D_MODEL // LANE
assert TOKEN_TILE_ROWS == SUBLANE
PACKED_ROWS = D_MODEL // (2 * LANE)
HIGH_HALF = 0xFFFF0000
SAMPLE_SEQS_PER_STEP = 8


def _vmem_limit(block_bytes):
    return int(min(2 * block_bytes + 16 * MIB, 56 * MIB))


def _params(semantics, block_bytes):
    return pltpu.CompilerParams(dimension_semantics=semantics,
                                vmem_limit_bytes=_vmem_limit(block_bytes))


def _nbytes(shape, dtype):
    return math.prod(shape) * jnp.dtype(dtype).itemsize


def _full(shape):
    return pl.BlockSpec(shape, lambda *_: (0,) * len(shape))


def _dot(a, b):
    return jnp.dot(a, b, preferred_element_type=F32)


def _dot_nt(a, b):
    return lax.dot_general(a, b, (((1,), (1,)), ((), ())), preferred_element_type=F32)


def _dot_tn(a, b):
    return lax.dot_general(a, b, (((0,), (0,)), ((), ())), preferred_element_type=F32)


def _split3(a):
    hi = a.astype(BF16)
    r1 = a - hi.astype(F32)
    mid = r1.astype(BF16)
    lo = (r1 - mid.astype(F32)).astype(BF16)
    return hi, mid, lo


def _rmsnorm(x, g):
    r = lax.rsqrt(jnp.mean(x * x, axis=-1, keepdims=True) + EPS)
    return x * r * g


def _token_rows(ref, rows):
    return jnp.concatenate([ref[pl.ds(k, rows, stride=SUBLANE), :] for k in range(TOKEN_TILE_ROWS)], axis=1)


def _token_spec(row_offset):
    first = row_offset // ROW_TILE
    return pl.BlockSpec((ROW_TILE * TOKEN_TILE_ROWS, LANE), lambda i: (i + first, 0))


def _inproj_body(has_add, *refs):
    n_in = 8 if has_add else 7
    ins, outs = refs[:n_in], refs[n_in:]
    if has_add:
        x_ref, y_ref, g_ref = ins[:3]
        x = x_ref[...] + _token_rows(y_ref, ROW_TILE)
        outs[0][...] = x
        outs = outs[1:]
    else:
        x_ref, g_ref = ins[:2]
        x = x_ref[...]
    w_refs = ins[-5:]
    h = _rmsnorm(x, g_ref[...]).astype(BF16)
    for w_ref, o_ref in zip(w_refs, outs, strict=True):
        o_ref[...] = _dot(h, w_ref[...])


def _inproj(x, y, y_offset, g, ws):
    t = x.shape[0]
    has_add = y is not None
    widths = [w.shape[1] for w in ws]
    row = lambda w: pl.BlockSpec((ROW_TILE, w), lambda i: (i, 0))
    in_specs = [row(D_MODEL)] + ([_token_spec(y_offset)] if has_add else []) + [_full((1, D_MODEL))]
    in_specs += [_full(w.shape) for w in ws]
    out_shape = [jax.ShapeDtypeStruct((t, w), F32) for w in widths]
    out_specs = [row(w) for w in widths]
    if has_add:
        out_shape = [jax.ShapeDtypeStruct((t, D_MODEL), F32)] + out_shape
        out_specs = [row(D_MODEL)] + out_specs
    block_bytes = (sum(_nbytes(w.shape, BF16) for w in ws)
                   + _nbytes((ROW_TILE, sum(widths) + 3 * D_MODEL), F32))
    args = ([x, y] if has_add else [x]) + [g] + list(ws)
    return pl.pallas_call(
        functools.partial(_inproj_body, has_add),
        grid=(t // ROW_TILE,), in_specs=in_specs, out_specs=out_specs, out_shape=out_shape,
        compiler_params=_params(("arbitrary",), block_bytes), name="inproj",
    )(*args)


def _retention_body(c, nb, nc, has_state, *refs):
    refs = list(refs)
    qkvr_ref, cos_ref, sin_ref, dmat_ref, qdec_ref, kdec_ref, cdec_ref, gn_ref = refs[:8]
    refs = refs[8:]
    if has_state:
        s0_ref = refs.pop(0)
    o_ref, ns_ref, st_scr = refs
    ci = pl.program_id(1)

    @pl.when(ci == 0)
    def _():
        if has_state:
            st_scr[...] = s0_ref[...]
        else:
            st_scr[...] = jnp.zeros(st_scr.shape, F32)

    for s in range(nb):
        rows = slice(s * c, (s + 1) * c)
        cosv, sinv = cos_ref[rows, :], sin_ref[rows, :]
        for h in range(H_RET):
            col = lambda j: slice(j * RET_W + h * DK_RET, j * RET_W + (h + 1) * DK_RET)
            q, k = qkvr_ref[rows, col(0)], qkvr_ref[rows, col(1)]
            v, rg = qkvr_ref[rows, col(2)], qkvr_ref[rows, col(3)]
            qr = q * cosv + pltpu.roll(q, DK_RET // 2, axis=1) * sinv
            kr = (k * cosv + pltpu.roll(k, DK_RET // 2, axis=1) * sinv) * (DK_RET ** -0.5)
            qb, kb, vb = qr.astype(BF16), kr.astype(BF16), v.astype(BF16)
            state = st_scr[s, h]
            sc = _dot_nt(qb, kb) * dmat_ref[h]
            o = _dot(sc.astype(BF16), vb) + _dot(qb, state.astype(BF16)) * qdec_ref[h]
            st_scr[s, h] = cdec_ref[h] * state + _dot_tn((kr * kdec_ref[h]).astype(BF16), vb)
            mu = jnp.mean(o, axis=-1, keepdims=True)
            d = o - mu
            var = jnp.mean(d * d, axis=-1, keepdims=True)
            on = d * lax.rsqrt(var + EPS) * gn_ref[:, h * DV_RET:(h + 1) * DV_RET]
            o_ref[rows, h * DV_RET:(h + 1) * DV_RET] = (jax.nn.silu(rg) * on).astype(o_ref.dtype)

    @pl.when(ci == nc - 1)
    def _():
        ns_ref[...] = st_scr[...]


def _retention(qkvr, start, gn_g, s0, batch, length):
    c = math.gcd(length, CHUNK)
    nc = length // c
    nb = 1 if nc > 1 else SAMPLE_SEQS_PER_STEP
    assert batch % nb == 0
    has_state = s0 is not None
    t = batch * length
    half = DK_RET // 2
    pos = start + jnp.arange(length, dtype=F32)
    inv = ROPE_BASE ** (-jnp.arange(half, dtype=F32) / half)
    ang = pos[:, None] * inv[None, :]
    cosv = jnp.tile(jnp.concatenate([jnp.cos(ang), jnp.cos(ang)], axis=1), (nb, 1))
    sinv = jnp.tile(jnp.concatenate([-jnp.sin(ang), jnp.sin(ang)], axis=1), (nb, 1))
    log_g = jnp.log1p(-jnp.exp2(-5.0 - jnp.arange(H_RET, dtype=F32)))
    idx = jnp.arange(c, dtype=F32)
    rel = idx[:, None] - idx[None, :]
    causal = rel >= 0
    dmat = jnp.where(causal[None], jnp.exp(jnp.where(causal, rel, 0.0)[None] * log_g[:, None, None]), 0.0)
    qdec = jnp.broadcast_to(jnp.exp((idx[None, :] + 1.0) * log_g[:, None])[:, :, None], (H_RET, c, LANE))
    kdec = jnp.broadcast_to(jnp.exp((c - 1.0 - idx[None, :]) * log_g[:, None])[:, :, None], (H_RET, c, LANE))
    cdec = jnp.exp(c * log_g)

    rows = nb * c
    state_spec = pl.BlockSpec((nb, H_RET, DK_RET, DV_RET), lambda b, ci: (b, 0, 0, 0))
    in_specs = [
        pl.BlockSpec((rows, 4 * RET_W), lambda b, ci: (b * nc + ci, 0)),
        pl.BlockSpec((rows, LANE), lambda b, ci: (ci, 0)),
        pl.BlockSpec((rows, LANE), lambda b, ci: (ci, 0)),
        _full((H_RET, c, c)), _full((H_RET, c, LANE)), _full((H_RET, c, LANE)),
        pl.BlockSpec(memory_space=pltpu.SMEM),
        _full((1, RET_W)),
    ]
    args = [qkvr, cosv, sinv, dmat, qdec, kdec, cdec, gn_g]
    if has_state:
        in_specs.append(state_spec)
        args.append(s0)
    block_bytes = (_nbytes((rows, 4 * RET_W + 2 * LANE + RET_W), F32)
                   + 3 * _nbytes((nb, H_RET, DK_RET, DV_RET), F32) + 3 * _nbytes((H_RET, c, LANE), F32))
    return pl.pallas_call(
        functools.partial(_retention_body, c, nb, nc, has_state),
        grid=(batch // nb, nc), in_specs=in_specs,
        out_specs=[pl.BlockSpec((rows, RET_W), lambda b, ci: (b * nc + ci, 0)), state_spec],
        out_shape=[jax.ShapeDtypeStruct((t, RET_W), BF16),
                   jax.ShapeDtypeStruct((batch, H_RET, DK_RET, DV_RET), F32)],
        scratch_shapes=[pltpu.VMEM((nb, H_RET, DK_RET, DV_RET), F32)],
        compiler_params=_params(("arbitrary", "arbitrary"), block_bytes), name="retention",
    )(*args)


def _ssd_body(c, nb, nc, has_state, *refs):
    refs = list(refs)
    (xbc_ref, z_ref, dt_ref, cw_ref, cb_ref, dtb_ref, a_ref, dsk_ref, ng_ref,
     tri_ref, eye_ref, exp_ref, sel_ref) = refs[:13]
    refs = refs[13:]
    if has_state:
        sconv_ref, sssm_ref = refs.pop(0), refs.pop(0)
    y_ref, nconv_ref, nssm_ref, ext_scr, st_scr = refs
    ci = pl.program_id(1)
    pad = SUBLANE
    hist = CONV_K - 1

    @pl.when(ci == 0)
    def _():
        ext_scr[:, 0:pad, :] = jnp.zeros((nb, pad, CONV_DIM), F32)
        if has_state:
            ext_scr[:, pad - hist:pad, :] = sconv_ref[...]
            st_scr[...] = sssm_ref[...]
        else:
            st_scr[...] = jnp.zeros(st_scr.shape, F32)

    causal = (lax.broadcasted_iota(jnp.int32, (c, c), 0) >= lax.broadcasted_iota(jnp.int32, (c, c), 1))
    left_half = (lax.broadcasted_iota(jnp.int32, (c, SSD_W), 1) % LANE) < P_SSD
    tri, eye, expand = tri_ref[...], eye_ref[...], exp_ref[...]

    for s in range(nb):
        rows = slice(s * c, (s + 1) * c)
        ext_scr[s, pad:pad + c, :] = xbc_ref[rows, :]
        acc = ext_scr[s, pad - hist:pad - hist + c, :] * cw_ref[0:1, :]
        for j in range(1, CONV_K):
            acc = acc + ext_scr[s, pad - hist + j:pad - hist + j + c, :] * cw_ref[j:j + 1, :]
        conv = jax.nn.silu(cb_ref[...] + acc)

        @pl.when(ci == nc - 1)
        def _():
            nconv_ref[s] = ext_scr[s, pad + c - hist:pad + c, :]

        ext_scr[s, 0:pad, :] = ext_scr[s, c:c + pad, :]

        xs = conv[:, :SSD_W]
        bm = conv[:, SSD_W:SSD_W + G_SSD * N_SSD]
        cm = conv[:, SSD_W + G_SSD * N_SSD:]
        dt = jax.nn.softplus(dt_ref[rows, :] + dtb_ref[...])
        dta = dt * a_ref[...]
        cum = sum(_dot(tri, p) for p in _split3(dta))
        cum_parts = _split3(cum)
        cum_t = sum(_dot_nt(eye, p) for p in cum_parts)
        cum_e = sum(_dot(p, expand) for p in cum_parts)
        dt_e = sum(_dot(p, expand) for p in _split3(dt))
        last_e = cum_e[c - 1:c, :]
        xdt = xs * dt_e
        xw = (xs * (jnp.exp(last_e - cum_e) * dt_e)).astype(BF16)
        x_l = jnp.where(left_half, xdt, 0.0).astype(BF16)
        x_r = jnp.where(left_half, 0.0, xdt).astype(BF16)

        y_parts, off_parts = [], []
        for g in range(G_SSD):
            bm_g = bm[:, g * N_SSD:(g + 1) * N_SSD].astype(BF16)
            cm_g = cm[:, g * N_SSD:(g + 1) * N_SSD].astype(BF16)
            cb = _dot_nt(cm_g, bm_g)
            state = st_scr[s, g]
            off_parts.append(_dot_nt(cm_g, state.astype(BF16)))
            decay = jnp.exp(jnp.sum(sel_ref[g] * cum[c - 1:c, :], axis=1, keepdims=True))
            st_scr[s, g] = decay * state + _dot_tn(xw[:, g * GRP_W:(g + 1) * GRP_W], bm_g)
            for jj in range(R_SSD // 2):
                j = g * (R_SSD // 2) + jj
                w = []
                for hd in (2 * j, 2 * j + 1):
                    seg = cum[:, hd:hd + 1] - cum_t[hd:hd + 1, :]
                    w.append((jnp.exp(jnp.where(causal, seg, -jnp.inf)) * cb).astype(BF16))
                cols = slice(j * LANE, (j + 1) * LANE)
                y_parts.append(_dot(w[0], x_l[:, cols]) + _dot(w[1], x_r[:, cols]))
        y = jnp.concatenate(y_parts, axis=1) + jnp.exp(cum_e) * jnp.concatenate(off_parts, axis=1)
        y = (y + dsk_ref[...] * xs) * jax.nn.silu(z_ref[rows, :])
        normed = []
        for g in range(G_SSD):
            yg = y[:, g * GRP_W:(g + 1) * GRP_W]
            normed.append(yg * lax.rsqrt(jnp.mean(yg * yg, axis=-1, keepdims=True) + EPS))
        y_ref[rows, :] = (jnp.concatenate(normed, axis=1) * ng_ref[...]).astype(y_ref.dtype)

    @pl.when(ci == nc - 1)
    def _():
        nssm_ref[...] = st_scr[...]


def _ssd(xbc, z, dt, lp, s_conv, s_ssm, batch, length):
    c = math.gcd(length, CHUNK)
    nc = length // c
    nb = 1 if nc > 1 else SAMPLE_SEQS_PER_STEP
    assert batch % nb == 0 and c % SUBLANE == 0
    has_state = s_conv is not None
    t = batch * length
    rows = nb * c
    tri = (jnp.arange(c)[:, None] >= jnp.arange(c)[None, :]).astype(BF16)
    eye = jnp.eye(LANE, dtype=BF16)
    head_of_ch = jnp.arange(SSD_W) // P_SSD
    expand = (jnp.arange(LANE)[:, None] == head_of_ch[None, :]).astype(BF16)
    sel = (head_of_ch[:, None] == jnp.arange(LANE)[None, :]).astype(F32).reshape(G_SSD, GRP_W, LANE)

    seq = lambda shape: pl.BlockSpec((nb,) + shape, lambda b, ci: (b,) + (0,) * len(shape))
    rowb = lambda w: pl.BlockSpec((rows, w), lambda b, ci: (b * nc + ci, 0))
    in_specs = [rowb(CONV_DIM), rowb(SSD_W), rowb(LANE),
                _full((CONV_K, CONV_DIM)), _full((1, CONV_DIM)), _full((1, LANE)), _full((1, LANE)),
                _full((1, SSD_W)), _full((1, SSD_W)),
                _full((c, c)), _full((LANE, LANE)), _full((LANE, SSD_W)), _full((G_SSD, GRP_W, LANE))]
    args = [xbc, z, dt, lp["conv_w"], lp["conv_b"], lp["dt_bias"], lp["a"], lp["d_skip"], lp["ssd_norm_g"],
            tri, eye, expand, sel]
    if has_state:
        in_specs += [seq((CONV_K - 1, CONV_DIM)), seq((G_SSD, GRP_W, N_SSD))]
        args += [s_conv, s_ssm.reshape(batch, G_SSD, GRP_W, N_SSD)]
    block_bytes = (_nbytes((rows, CONV_DIM + 2 * SSD_W + LANE), F32) + _nbytes((nb, c + SUBLANE, CONV_DIM), F32)
                   + 3 * _nbytes((nb, SSD_W, N_SSD), F32) + _nbytes((LANE, SSD_W), F32)
                   + 12 * _nbytes((c, SSD_W), F32))
    y, nconv, nssm = pl.pallas_call(
        functools.partial(_ssd_body, c, nb, nc, has_state),
        grid=(batch // nb, nc), in_specs=in_specs,
        out_specs=[rowb(SSD_W), seq((CONV_K - 1, CONV_DIM)), seq((G_SSD, GRP_W, N_SSD))],
        out_shape=[jax.ShapeDtypeStruct((t, SSD_W), BF16),
                   jax.ShapeDtypeStruct((batch, CONV_K - 1, CONV_DIM), F32),
                   jax.ShapeDtypeStruct((batch, G_SSD, GRP_W, N_SSD), F32)],
        scratch_shapes=[pltpu.VMEM((nb, c + SUBLANE, CONV_DIM), F32),
                        pltpu.VMEM((nb, G_SSD, GRP_W, N_SSD), F32)],
        compiler_params=_params(("arbitrary", "arbitrary"), block_bytes), name="ssd",
    )(*args)
    return y, nconv, nssm.reshape(batch, H_SSD, P_SSD, N_SSD)


def _first_argmax(vals):
    best = vals[0]
    for v in vals[1:]:
        best = jnp.maximum(best, v)
    idx = jnp.full(best.shape, len(vals) - 1, jnp.int32)
    for i in range(len(vals) - 2, -1, -1):
        idx = jnp.where(vals[i] == best, i, idx)
    return best, idx


def _merge_body(bounds, *refs):
    n_groups = len(bounds)
    acts, refs = refs[:4 * n_groups], refs[4 * n_groups:]
    weights, refs = refs[:7], refs[7:]
    x2_refs, shared = refs[:n_groups], refs[n_groups:]
    i = pl.program_id(0)
    for gi, (first, count) in enumerate(bounds):
        @pl.when((i >= first) & (i < first + count))
        def _():
            _merge_tile(*acts[4 * gi:4 * gi + 4], *weights, x2_refs[gi], *shared)


def _merge_tile(ret_ref, ssd_ref, gates_ref, x_ref, wr_ref, ws_ref, wo_ref, g2_ref, rhi_ref, rlo_ref, low_ref,
                x2_ref, hw_ref, route_ref, cnt_ref):
    ret_y = _dot(ret_ref[...], wr_ref[...])
    ssd_y = _dot(ssd_ref[...], ws_ref[...])
    m = (jax.nn.sigmoid(gates_ref[:, :D_MODEL]) * ret_y + jax.nn.sigmoid(gates_ref[:, D_MODEL:]) * ssd_y)
    x2 = x_ref[...] + _dot(m.astype(BF16), wo_ref[...])
    x2_ref[...] = x2
    h2 = _rmsnorm(x2, g2_ref[...])
    bits = pltpu.bitcast(h2.astype(BF16).astype(F32), jnp.uint32)
    half = D_MODEL // 2
    word = lax.bitwise_or(lax.bitwise_and(bits[:, half:], jnp.uint32(HIGH_HALF)),
                          lax.shift_right_logical(bits[:, :half], jnp.uint32(16)))
    n_rows = h2.shape[0]
    for k in range(PACKED_ROWS):
        hw_ref[pl.ds(k, n_rows, stride=SUBLANE), :] = word[:, k * LANE:(k + 1) * LANE]

    h_hi = h2.astype(BF16)
    h_lo = (h2 - h_hi.astype(F32)).astype(BF16)
    logits = _dot(h_hi, rhi_ref[...]) + (_dot(h_hi, rlo_ref[...]) + _dot(h_lo, rhi_ref[...]))
    colv = lambda i: logits[:, i:i + 1]
    glog = [colv(i) for i in range(N_EGROUPS)]
    gmax, gidx = _first_argmax(glog)
    den = glog[0] * 0.0
    for v in glog:
        den = den + jnp.exp(v - gmax)
    gw = 1.0 / den
    sel = []
    for j in range(EXP_PER_GROUP):
        v = colv(N_EGROUPS + (N_EGROUPS - 1) * EXP_PER_GROUP + j)
        for g in range(N_EGROUPS - 2, -1, -1):
            v = jnp.where(gidx == g, colv(N_EGROUPS + g * EXP_PER_GROUP + j), v)
        sel.append(v)
    v1, i1 = _first_argmax(sel)
    v2, i2 = _first_argmax([jnp.where(i1 == j, -jnp.inf, sel[j]) for j in range(EXP_PER_GROUP)])
    e = jnp.exp(v2 - v1)
    w1 = gw / (1.0 + e)
    w2 = gw * e / (1.0 + e)
    first_low = i1 < i2
    lo = jnp.where(first_low, i1, i2)
    hi = jnp.where(first_low, i2, i1)
    pair = jnp.where(lo == 0, hi - 1, jnp.where(lo == 1, hi + 1, N_PAIRS - 1))
    cls = gidx * N_PAIRS + pair
    w_lo = jnp.where(first_low, w1, w2)
    w_hi = jnp.where(first_low, w2, w1)

    rows = logits.shape[0]
    lane = lax.broadcasted_iota(jnp.int32, (rows, META_W), 1)
    onehot = (lane == cls).astype(F32)
    before = _dot(low_ref[...], onehot.astype(BF16))
    rank = jnp.sum(onehot * before, axis=1, keepdims=True)
    meta = jnp.where(lane == 0, w_lo, jnp.where(lane == 1, w_hi, jnp.where(
        lane == 2, cls.astype(F32), jnp.where(lane == 3, rank, 0.0))))
    meta_bits = pltpu.bitcast(meta, jnp.uint32)
    for k in range(PACKED_ROWS, TOKEN_TILE_ROWS):
        hw_ref[pl.ds(k, n_rows, stride=SUBLANE), :] = meta_bits
    route_ref[...] = meta
    cnt_ref[0] = jnp.sum(onehot, axis=0, keepdims=True)


def _merge(group_acts, lp):
    counts = [a[3].shape[0] // ROW_TILE for a in group_acts]
    bounds = tuple((sum(counts[:gi]), counts[gi]) for gi in range(len(counts)))
    n_tiles = sum(counts)
    total_rows = n_tiles * ROW_TILE
    low = (jnp.arange(ROW_TILE)[:, None] > jnp.arange(ROW_TILE)[None, :]).astype(BF16)
    ws = [lp["w_ret_out"], lp["w_ssd_out"], lp["w_o"], lp["ln2_g"], lp["router_hi"], lp["router_lo"], low]
    block_bytes = (sum(_nbytes(w.shape, w.dtype) for w in ws)
                   + len(counts) * (_nbytes((ROW_TILE, RET_W + SSD_W), BF16) + _nbytes((ROW_TILE, 4 * D_MODEL), F32))
                   + _nbytes((ROW_TILE, 3 * D_MODEL), F32))

    def group_row(width, first, count):
        return pl.BlockSpec((ROW_TILE, width), lambda i: (jnp.clip(i - first, 0, count - 1), 0))

    in_specs, args, x2_specs, x2_shapes = [], [], [], []
    for (first, count), acts in zip(bounds, group_acts, strict=True):
        in_specs += [group_row(w, first, count) for w in (RET_W, SSD_W, 2 * D_MODEL, D_MODEL)]
        args += list(acts)
        x2_specs.append(group_row(D_MODEL, first, count))
        x2_shapes.append(jax.ShapeDtypeStruct((count * ROW_TILE, D_MODEL), F32))
    in_specs += [_full(w.shape) for w in ws]
    args += ws
    outs = pl.pallas_call(
        functools.partial(_merge_body, bounds), grid=(n_tiles,), in_specs=in_specs,
        out_specs=x2_specs + [_token_spec(0), pl.BlockSpec((ROW_TILE, META_W), lambda i: (i, 0)),
                              pl.BlockSpec((1, 1, META_W), lambda i: (i, 0, 0))],
        out_shape=x2_shapes + [jax.ShapeDtypeStruct((total_rows * TOKEN_TILE_ROWS, LANE), jnp.uint32),
                               jax.ShapeDtypeStruct((total_rows, META_W), F32),
                               jax.ShapeDtypeStruct((n_tiles, 1, META_W), F32)],
        compiler_params=_params(("arbitrary",), block_bytes), name="merge_router",
    )(*args)
    return outs[:len(counts)], outs[len(counts):]


PLAN_ROWS = LANE
TAB_ROWS = SUBLANE
TAB_LO, TAB_HI, TAB_USED = 0, 1, 2
BYTE = 256.0


def _row_of(col_vals):
    pick = (lax.broadcasted_iota(jnp.int32, (TAB_ROWS, LANE), 0)
            == lax.broadcasted_iota(jnp.int32, (TAB_ROWS, LANE), 1)).astype(BF16)
    hi = jnp.floor(col_vals * (1.0 / BYTE))
    lo = col_vals - hi * BYTE
    return BYTE * _dot_nt(pick, hi.astype(BF16)) + _dot_nt(pick, lo.astype(BF16))


def _plan_body(tile, n_row_tiles, route_ref, cnt_ref, low_ref, upp_ref, pos_ref, tab_ref, offs_scr):
    cnt = cnt_ref[...]
    tot = jnp.sum(cnt, axis=0, keepdims=True)
    cls_tiles = jnp.floor((tot + (tile - 1.0)) * (1.0 / tile))
    first_tile = _dot(jnp.broadcast_to(cls_tiles, (SUBLANE, LANE)).astype(BF16), upp_ref[...])[0:1, :]
    base = first_tile * tile
    ends = base + cls_tiles * tile
    offs_scr[...] = base + _dot(low_ref[...], cnt.astype(BF16))

    def per_row_tile(j, carry):
        meta = route_ref[pl.ds(j * ROW_TILE, ROW_TILE), :]
        lane = lax.broadcasted_iota(jnp.int32, (ROW_TILE, META_W), 1)
        onehot = lane.astype(F32) == meta[:, 2:3]
        pos = jnp.sum(jnp.where(onehot, offs_scr[pl.ds(j, 1), :], 0.0), axis=1, keepdims=True) + meta[:, 3:4]
        rows = _row_of(jnp.where(lane == 0, pos, 0.0))
        pos_ref[j] = rows[0:1, :].astype(jnp.int32)
        return carry

    lax.fori_loop(0, n_row_tiles, per_row_tile, 0)

    lane = lax.broadcasted_iota(jnp.int32, (PLAN_ROWS, LANE), 1)
    tile_idx = lax.broadcasted_iota(jnp.int32, (PLAN_ROWS, LANE), 0).astype(F32)[:, 0:1]
    n_used = jnp.sum(cls_tiles, axis=1, keepdims=True)
    start = jnp.minimum(tile_idx, n_used - 1.0) * tile
    tcls = jnp.sum(jnp.where((lane < N_CLASSES) & (ends <= start), 1.0, 0.0), axis=1, keepdims=True)
    tcls = jnp.minimum(tcls, N_CLASSES - 1.0)
    group = sum(jnp.where(tcls >= k * N_PAIRS, 1.0, 0.0) for k in range(1, N_EGROUPS))
    pair = tcls - group * N_PAIRS
    p_lo = jnp.where(pair >= 3, 1.0, 0.0) + jnp.where(pair >= 5, 1.0, 0.0)
    p_hi = jnp.where(pair == 0, 1.0, jnp.where((pair == 1) | (pair == 3), 2.0, 3.0))
    table = jnp.where(lane == TAB_LO, group * EXP_PER_GROUP + p_lo,
                      jnp.where(lane == TAB_HI, group * EXP_PER_GROUP + p_hi,
                                jnp.where(lane == TAB_USED, n_used, 0.0)))
    tab_ref[...] = _row_of(table).astype(jnp.int32)


def _plan(route, counts, tile):
    t = route.shape[0]
    nrt = t // ROW_TILE
    assert nrt <= PLAN_ROWS and t // tile + N_CLASSES <= PLAN_ROWS
    cnt = jnp.pad(counts[:, 0, :], ((0, PLAN_ROWS - nrt), (0, 0)))
    idx = jnp.arange(PLAN_ROWS)
    low = (idx[:, None] > idx[None, :]).astype(BF16)
    upp = (idx[:, None] < idx[None, :]).astype(BF16)
    pos, tab = pl.pallas_call(
        functools.partial(_plan_body, tile, nrt),
        out_shape=[jax.ShapeDtypeStruct((nrt, 1, ROW_TILE), jnp.int32),
                   jax.ShapeDtypeStruct((TAB_ROWS, PLAN_ROWS), jnp.int32)],
        scratch_shapes=[pltpu.VMEM((PLAN_ROWS, LANE), F32)],
        compiler_params=pltpu.CompilerParams(vmem_limit_bytes=_vmem_limit(_nbytes(route.shape, F32))),
        name="moe_plan",
    )(route, cnt, low, upp)
    return pos.reshape(t), tab


def _moe_body(tile, n_tok, pos_ref, tab_ref, hw_hbm, spare_hbm, wgl_ref, wul_ref, wdl_ref, wgh_ref, wuh_ref,
              wdh_ref, y_hbm, buf, obuf, tok_smem, gsem, ssem, isem):
    i = pl.program_id(0)
    n_used = tab_ref[TAB_USED, 0]
    slot = i % 2
    other = 1 - slot
    n_points = 16
    per_point = tile // n_points

    tr = TOKEN_TILE_ROWS

    def token_tile(ref, token):
        start = token * tr if isinstance(token, int) else pl.multiple_of(token * tr, tr)
        return ref.at[pl.ds(start, tr)]

    def dma_thread(r):
        return r % 2 if isinstance(r, int) else 0

    def gather_start(block, r, sl):
        src = jnp.minimum(tok_smem[block * tile + r], n_tok - 1)
        pltpu.make_async_copy(token_tile(hw_hbm, src), token_tile(buf.at[sl], r), gsem.at[sl]).start(dma_thread(r))

    def scatter_start(block, r, sl):
        dst = tok_smem[block * tile + r]
        pltpu.make_async_copy(token_tile(obuf.at[sl], r), token_tile(y_hbm, dst), ssem.at[sl]).start(dma_thread(r))

    def gather_wait(sl):
        pltpu.make_async_copy(hw_hbm.at[pl.ds(0, tile * tr)], buf.at[sl], gsem.at[sl]).wait()

    def scatter_wait(sl):
        pltpu.make_async_copy(obuf.at[sl], y_hbm.at[pl.ds(0, tile * tr)], ssem.at[sl]).wait()

    def for_rows(n, fn):
        lax.fori_loop(0, n, lambda r, carry: (fn(r), carry)[1], 0, unroll=8)

    @pl.when(i < n_used)
    def _():
        @pl.when(i == 0)
        def _():
            fill = pltpu.make_async_copy(spare_hbm, tok_smem, isem)
            fill.start()
            fill.wait()

            def place(t):
                tok_smem[tile + pos_ref[t]] = t
            for_rows(n_tok, place)
            obuf[1] = jnp.zeros(obuf.shape[1:], F32)
            for_rows(tile, lambda r: gather_start(1, r, 0))

        gather_wait(slot)
        nxt = jnp.minimum(i + 1, n_used - 1)
        issued = [0]

        def issue_point():
            for r in range(issued[0] * per_point, (issued[0] + 1) * per_point):
                gather_start(nxt + 1, r, other)
                scatter_start(i, r, other)
            issued[0] += 1

        def tile_row(k):
            return buf[slot, pl.ds(k, tile, stride=tr), :]

        words = [tile_row(k) for k in range(PACKED_ROWS)]
        low = [pltpu.bitcast(lax.shift_left(w, jnp.uint32(16)), F32).astype(BF16) for w in words]
        high = [pltpu.bitcast(lax.bitwise_and(w, jnp.uint32(HIGH_HALF)), F32).astype(BF16) for w in words]
        xb = jnp.concatenate(low + high, axis=1)
        meta = pltpu.bitcast(tile_row(PACKED_ROWS), F32)
        w_lo, w_hi = meta[:, 0:1], meta[:, 1:2]
        half = D_FF // 2
        acts = []
        for wg_ref, wu_ref in ((wgl_ref, wul_ref), (wgh_ref, wuh_ref)):
            parts = []
            for n in range(2):
                gate = _dot(xb, wg_ref[:, n * half:(n + 1) * half])
                issue_point()
                up = _dot(xb, wu_ref[:, n * half:(n + 1) * half])
                issue_point()
                parts.append((jax.nn.silu(gate) * up).astype(BF16))
            acts.append(jnp.concatenate(parts, axis=1))
        quarter = D_MODEL // 4
        for n in range(4):
            cols = slice(n * quarter, (n + 1) * quarter)
            d_lo = _dot(acts[0], wdl_ref[:, cols])
            issue_point()
            d_hi = _dot(acts[1], wdh_ref[:, cols])
            issue_point()
            out = d_lo * w_lo + d_hi * w_hi
            for kk in range(quarter // LANE):
                obuf[slot, pl.ds(n * (quarter // LANE) + kk, tile, stride=tr), :] = out[:, kk * LANE:(kk + 1) * LANE]
        assert issued[0] == n_points
        scatter_wait(other)

        @pl.when(i == n_used - 1)
        def _():
            for_rows(tile, lambda r: scatter_start(i + 1, r, slot))
            scatter_wait(slot)
            gather_wait(other)


def _moe(hw, pos, tab, lp, tile):
    t = hw.shape[0] // TOKEN_TILE_ROWS
    n_tiles = t // tile + N_CLASSES
    wspec = lambda shape, row: pl.BlockSpec((None,) + shape, lambda i, pos, tab: (tab[row, i], 0, 0))
    in_specs = [pl.BlockSpec(memory_space=pl.ANY)] * 2
    spare = t + jnp.arange((n_tiles + 1) * tile, dtype=jnp.int32) % tile
    for row in (TAB_LO, TAB_HI):
        in_specs += [wspec((D_MODEL, D_FF), row), wspec((D_MODEL, D_FF), row), wspec((D_FF, D_MODEL), row)]
    block_bytes = 6 * _nbytes((D_MODEL, D_FF), BF16) + 6 * _nbytes((tile, D_MODEL), F32)
    tile_rows = tile * TOKEN_TILE_ROWS
    grid_spec = pltpu.PrefetchScalarGridSpec(
        num_scalar_prefetch=2, grid=(n_tiles,), in_specs=in_specs,
        out_specs=pl.BlockSpec(memory_space=pl.ANY),
        scratch_shapes=[pltpu.VMEM((2, tile_rows, LANE), jnp.uint32), pltpu.VMEM((2, tile_rows, LANE), F32),
                        pltpu.SMEM(((n_tiles + 1) * tile,), jnp.int32),
                        pltpu.SemaphoreType.DMA((2,)), pltpu.SemaphoreType.DMA((2,)),
                        pltpu.SemaphoreType.DMA(())])
    return pl.pallas_call(
        functools.partial(_moe_body, tile, t), grid_spec=grid_spec,
        out_shape=jax.ShapeDtypeStruct(((t + tile) * TOKEN_TILE_ROWS, LANE), F32),
        compiler_params=_params(("arbitrary",), block_bytes), name="moe",
    )(pos, tab, hw, spare, lp["w_gate"], lp["w_up"], lp["w_down"], lp["w_gate"], lp["w_up"], lp["w_down"])


def _final_body(x_ref, y_ref, g_ref, o_ref):
    o_ref[...] = _rmsnorm(x_ref[...] + _token_rows(y_ref, ROW_TILE), g_ref[...])


def _final_norm(x, y, y_offset, g):
    t = x.shape[0]
    row = pl.BlockSpec((ROW_TILE, D_MODEL), lambda i: (i, 0))
    return pl.pallas_call(
        _final_body, grid=(t // ROW_TILE,), in_specs=[row, _token_spec(y_offset), _full((1, D_MODEL))],
        out_specs=row, out_shape=jax.ShapeDtypeStruct((t, D_MODEL), F32),
        compiler_params=_params(("arbitrary",), 3 * _nbytes((ROW_TILE, D_MODEL), F32)), name="final_norm",
    )(x, y, g)


def _layer_params(i, ln1_g, w_in, ret_gn_g, w_ret_out, conv_w, conv_b, dt_bias, a_log, d_skip, ssd_norm_g,
                  w_ssd_out, w_o, ln2_g, w_rg, w_re, w_gate, w_up, w_down):
    offs = [0]
    for s in IN_SIZES:
        offs.append(offs[-1] + s)
    w = w_in[i]
    pad_heads = lambda v: jnp.pad(v.astype(F32), (0, LANE - H_SSD)).reshape(1, LANE)
    router = jnp.pad(jnp.concatenate([w_rg[i], w_re[i]], axis=1),
                     ((0, 0), (0, META_W - N_EGROUPS - N_EXPERTS)))
    router_hi = router.astype(BF16)
    return {
        "ln1_g": ln1_g[i].reshape(1, D_MODEL),
        "w_in": [w[:, offs[0]:offs[4]].astype(BF16),
                 w[:, offs[4]:offs[5]].astype(BF16),
                 w[:, offs[5]:offs[6]].astype(BF16),
                 jnp.pad(w[:, offs[6]:offs[7]], ((0, 0), (0, LANE - H_SSD))).astype(BF16),
                 w[:, offs[7]:offs[9]].astype(BF16)],
        "ret_gn_g": ret_gn_g[i].reshape(1, RET_W),
        "w_ret_out": w_ret_out[i].astype(BF16),
        "conv_w": conv_w[i], "conv_b": conv_b[i].reshape(1, CONV_DIM),
        "dt_bias": pad_heads(dt_bias[i]), "a": pad_heads(-jnp.exp(a_log[i].astype(F32))),
        "d_skip": jnp.repeat(d_skip[i].astype(F32), P_SSD).reshape(1, SSD_W),
        "ssd_norm_g": ssd_norm_g[i].reshape(1, SSD_W),
        "w_ssd_out": w_ssd_out[i].astype(BF16), "w_o": w_o[i].astype(BF16),
        "ln2_g": ln2_g[i].reshape(1, D_MODEL),
        "router_hi": router_hi, "router_lo": (router - router_hi.astype(F32)).astype(BF16),
        "w_gate": w_gate[i].astype(BF16), "w_up": w_up[i].astype(BF16), "w_down": w_down[i].astype(BF16),
    }


class _Group:
    def __init__(self, x, start, states, row_offset):
        self.batch, self.length, _ = x.shape
        self.x = x.reshape(self.batch * self.length, D_MODEL)
        self.start, self.states, self.row_offset = start, states, row_offset
        self.rets, self.ssms, self.convs = [], [], []


def kernel(x_prompt, x_sample, state_ret, state_ssm, state_conv, ln1_g, w_in, ret_gn_g, w_ret_out, conv_w, conv_b, dt_bias, a_log, d_skip, ssd_norm_g, w_ssd_out, w_o, ln2_g, w_router_group, w_router_expert, w_e_gate, w_e_up, w_e_down, lnf_g):
    depth = w_in.shape[0]
    past_len = 16384.0
    layers = [_layer_params(i, ln1_g, w_in, ret_gn_g, w_ret_out, conv_w, conv_b, dt_bias, a_log, d_skip,
                            ssd_norm_g, w_ssd_out, w_o, ln2_g, w_router_group, w_router_expert,
                            w_e_gate, w_e_up, w_e_down) for i in range(depth)]
    n_prompt = x_prompt.shape[0] * x_prompt.shape[1]
    groups = [_Group(x_prompt, 0.0, None, 0),
              _Group(x_sample, past_len, (state_ret, state_ssm, state_conv), n_prompt)]
    y = None
    for i, lp in enumerate(layers):
        group_acts = []
        for grp in groups:
            s_ret, s_ssm, s_conv = (None, None, None) if grp.states is None else (s[i] for s in grp.states)
            outs = _inproj(grp.x, y, grp.row_offset, lp["ln1_g"], lp["w_in"])
            if y is not None:
                grp.x, outs = outs[0], outs[1:]
            qkvr, z, xbc, dt, gates = outs
            ret, new_ret = _retention(qkvr, grp.start, lp["ret_gn_g"], s_ret, grp.batch, grp.length)
            ssd, new_conv, new_ssm = _ssd(xbc, z, dt, lp, s_conv, s_ssm, grp.batch, grp.length)
            grp.rets.append(new_ret)
            grp.ssms.append(new_ssm)
            grp.convs.append(new_conv)
            group_acts.append((ret, ssd, gates, grp.x))
        new_x, (hw, route, counts) = _merge(group_acts, lp)
        for grp, x2 in zip(groups, new_x, strict=True):
            grp.x = x2
        pos, tab = _plan(route, counts, ROW_TILE)
        y = _moe(hw, pos, tab, lp, ROW_TILE)
    outs = []
    for grp in groups:
        out = _final_norm(grp.x, y, grp.row_offset, lnf_g.reshape(1, D_MODEL))
        outs.append(out.reshape(grp.batch, grp.length, D_MODEL))
    states = []
    for grp in groups:
        states += [jnp.stack(grp.rets), jnp.stack(grp.ssms), jnp.stack(grp.convs)]
    return tuple(outs + states)
```

```python
import functools
import math

import jax
import jax.numpy as jnp
from jax import lax
from jax.experimental import pallas as pl
from jax.experimental.pallas import tpu as pltpu

F32 = jnp.float32
BF16 = jnp.bfloat16

D_MODEL = 1024
H_RET, DK_RET, DV_RET = 4, 128, 128
RET_W = H_RET * DV_RET
ROPE_BASE = 10000.0
H_SSD, P_SSD, G_SSD, N_SSD = 16, 64, 2, 64
R_SSD = H_SSD // G_SSD
SSD_W = H_SSD * P_SSD
GRP_W = SSD_W // G_SSD
CONV_K = 4
CONV_DIM = SSD_W + 2 * G_SSD * N_SSD
CHUNK = 128
N_EGROUPS, EXP_PER_GROUP = 4, 4
N_EXPERTS = N_EGROUPS * EXP_PER_GROUP
N_PAIRS = EXP_PER_GROUP * (EXP_PER_GROUP - 1) // 2
N_CLASSES = N_EGROUPS * N_PAIRS
D_FF = 512
EPS = 1e-6
IN_SIZES = [RET_W, RET_W, RET_W, RET_W, SSD_W, CONV_DIM, H_SSD, D_MODEL, D_MODEL]

LANE = 128
SUBLANE = 8
MIB = 1024 * 1024

ROW_TILE = 256
META_W = LANE
TOKEN_TILE_ROWS = D_MODEL // LANE
assert TOKEN_TILE_ROWS == SUBLANE
PACKED_ROWS = D_MODEL // (2 * LANE)
HIGH_HALF = 0xFFFF0000
SAMPLE_SEQS_PER_STEP = 8


def _vmem_limit(block_bytes):
    return int(min(2 * block_bytes + 16 * MIB, 56 * MIB))


def _params(semantics, block_bytes):
    return pltpu.CompilerParams(dimension_semantics=semantics,
                                vmem_limit_bytes=_vmem_limit(block_bytes))


def _nbytes(shape, dtype):
    return math.prod(shape) * jnp.dtype(dtype).itemsize


def _full(shape):
    return pl.BlockSpec(shape, lambda *_: (0,) * len(shape))


def _dot(a, b):
    return jnp.dot(a, b, preferred_element_type=F32)


def _dot_nt(a, b):
    return lax.dot_general(a, b, (((1,), (1,)), ((), ())), preferred_element_type=F32)


def _dot_tn(a, b):
    return lax.dot_general(a, b, (((0,), (0,)), ((), ())), preferred_element_type=F32)


def _split3(a):
    hi = a.astype(BF16)
    r1 = a - hi.astype(F32)
    mid = r1.astype(BF16)
    lo = (r1 - mid.astype(F32)).astype(BF16)
    return hi, mid, lo


def _rmsnorm(x, g):
    r = lax.rsqrt(jnp.mean(x * x, axis=-1, keepdims=True) + EPS)
    return x * r * g


def _token_rows(ref, rows):
    return jnp.concatenate([ref[pl.ds(k, rows, stride=SUBLANE), :] for k in range(TOKEN_TILE_ROWS)], axis=1)


def _token_spec(row_offset):
    first = row_offset // ROW_TILE
    return pl.BlockSpec((ROW_TILE * TOKEN_TILE_ROWS, LANE), lambda i: (i + first, 0))


def _inproj_body(has_add, *refs):
    n_in = 8 if has_add else 7
    ins, outs = refs[:n_in], refs[n_in:]
    if has_add:
        x_ref, y_ref, g_ref = ins[:3]
        x = x_ref[...] + _token_rows(y_ref, ROW_TILE)
        outs[0][...] = x
        outs = outs[1:]
    else:
        x_ref, g_ref = ins[:2]
        x = x_ref[...]
    w_refs = ins[-5:]
    h = _rmsnorm(x, g_ref[...]).astype(BF16)
    for w_ref, o_ref in zip(w_refs, outs, strict=True):
        o_ref[...] = _dot(h, w_ref[...])


def _inproj(x, y, y_offset, g, ws):
    t = x.shape[0]
    has_add = y is not None
    widths = [w.shape[1] for w in ws]
    row = lambda w: pl.BlockSpec((ROW_TILE, w), lambda i: (i, 0))
    in_specs = [row(D_MODEL)] + ([_token_spec(y_offset)] if has_add else []) + [_full((1, D_MODEL))]
    in_specs += [_full(w.shape) for w in ws]
    out_shape = [jax.ShapeDtypeStruct((t, w), F32) for w in widths]
    out_specs = [row(w) for w in widths]
    if has_add:
        out_shape = [jax.ShapeDtypeStruct((t, D_MODEL), F32)] + out_shape
        out_specs = [row(D_MODEL)] + out_specs
    block_bytes = (sum(_nbytes(w.shape, BF16) for w in ws)
                   + _nbytes((ROW_TILE, sum(widths) + 3 * D_MODEL), F32))
    args = ([x, y] if has_add else [x]) + [g] + list(ws)
    return pl.pallas_call(
        functools.partial(_inproj_body, has_add),
        grid=(t // ROW_TILE,), in_specs=in_specs, out_specs=out_specs, out_shape=out_shape,
        compiler_params=_params(("arbitrary",), block_bytes), name="inproj",
    )(*args)


def _retention_body(c, nb, nc, has_state, *refs):
    refs = list(refs)
    qkvr_ref, cos_ref, sin_ref, dmat_ref, qdec_ref, kdec_ref, cdec_ref, gn_ref = refs[:8]
    refs = refs[8:]
    if has_state:
        s0_ref = refs.pop(0)
    o_ref, ns_ref, st_scr = refs
    ci = pl.program_id(1)

    @pl.when(ci == 0)
    def _():
        if has_state:
            st_scr[...] = s0_ref[...]
        else:
            st_scr[...] = jnp.zeros(st_scr.shape, F32)

    for s in range(nb):
        rows = slice(s * c, (s + 1) * c)
        cosv, sinv = cos_ref[rows, :], sin_ref[rows, :]
        for h in range(H_RET):
            col = lambda j: slice(j * RET_W + h * DK_RET, j * RET_W + (h + 1) * DK_RET)
            q, k = qkvr_ref[rows, col(0)], qkvr_ref[rows, col(1)]
            v, rg = qkvr_ref[rows, col(2)], qkvr_ref[rows, col(3)]
            qr = q * cosv + pltpu.roll(q, DK_RET // 2, axis=1) * sinv
            kr = (k * cosv + pltpu.roll(k, DK_RET // 2, axis=1) * sinv) * (DK_RET ** -0.5)
            qb, kb, vb = qr.astype(BF16), kr.astype(BF16), v.astype(BF16)
            state = st_scr[s, h]
            sc = _dot_nt(qb, kb) * dmat_ref[h]
            o = _dot(sc.astype(BF16), vb) + _dot(qb, state.astype(BF16)) * qdec_ref[h]
            st_scr[s, h] = cdec_ref[h] * state + _dot_tn((kr * kdec_ref[h]).astype(BF16), vb)
            mu = jnp.mean(o, axis=-1, keepdims=True)
            d = o - mu
            var = jnp.mean(d * d, axis=-1, keepdims=True)
            on = d * lax.rsqrt(var + EPS) * gn_ref[:, h * DV_RET:(h + 1) * DV_RET]
            o_ref[rows, h * DV_RET:(h + 1) * DV_RET] = (jax.nn.silu(rg) * on).astype(o_ref.dtype)

    @pl.when(ci == nc - 1)
    def _():
        ns_ref[...] = st_scr[...]


def _retention(qkvr, start, gn_g, s0, batch, length):
    c = math.gcd(length, CHUNK)
    nc = length // c
    nb = 1 if nc > 1 else SAMPLE_SEQS_PER_STEP
    assert batch % nb == 0
    has_state = s0 is not None
    t = batch * length
    half = DK_RET // 2
    pos = start + jnp.arange(length, dtype=F32)
    inv = ROPE_BASE ** (-jnp.arange(half, dtype=F32) / half)
    ang = pos[:, None] * inv[None, :]
    cosv = jnp.tile(jnp.concatenate([jnp.cos(ang), jnp.cos(ang)], axis=1), (nb, 1))
    sinv = jnp.tile(jnp.concatenate([-jnp.sin(ang), jnp.sin(ang)], axis=1), (nb, 1))
    log_g = jnp.log1p(-jnp.exp2(-5.0 - jnp.arange(H_RET, dtype=F32)))
    idx = jnp.arange(c, dtype=F32)
    rel = idx[:, None] - idx[None, :]
    causal = rel >= 0
    dmat = jnp.where(causal[None], jnp.exp(jnp.where(causal, rel, 0.0)[None] * log_g[:, None, None]), 0.0)
    qdec = jnp.broadcast_to(jnp.exp((idx[None, :] + 1.0) * log_g[:, None])[:, :, None], (H_RET, c, LANE))
    kdec = jnp.broadcast_to(jnp.exp((c - 1.0 - idx[None, :]) * log_g[:, None])[:, :, None], (H_RET, c, LANE))
    cdec = jnp.exp(c * log_g)

    rows = nb * c
    state_spec = pl.BlockSpec((nb, H_RET, DK_RET, DV_RET), lambda b, ci: (b, 0, 0, 0))
    in_specs = [
        pl.BlockSpec((rows, 4 * RET_W), lambda b, ci: (b * nc + ci, 0)),
        pl.BlockSpec((rows, LANE), lambda b, ci: (ci, 0)),
        pl.BlockSpec((rows, LANE), lambda b, ci: (ci, 0)),
        _full((H_RET, c, c)), _full((H_RET, c, LANE)), _full((H_RET, c, LANE)),
        pl.BlockSpec(memory_space=pltpu.SMEM),
        _full((1, RET_W)),
    ]
    args = [qkvr, cosv, sinv, dmat, qdec, kdec, cdec, gn_g]
    if has_state:
        in_specs.append(state_spec)
        args.append(s0)
    block_bytes = (_nbytes((rows, 4 * RET_W + 2 * LANE + RET_W), F32)
                   + 3 * _nbytes((nb, H_RET, DK_RET, DV_RET), F32) + 3 * _nbytes((H_RET, c, LANE), F32))
    return pl.pallas_call(
        functools.partial(_retention_body, c, nb, nc, has_state),
        grid=(batch // nb, nc), in_specs=in_specs,
        out_specs=[pl.BlockSpec((rows, RET_W), lambda b, ci: (b * nc + ci, 0)), state_spec],
        out_shape=[jax.ShapeDtypeStruct((t, RET_W), BF16),
                   jax.ShapeDtypeStruct((batch, H_RET, DK_RET, DV_RET), F32)],
        scratch_shapes=[pltpu.VMEM((nb, H_RET, DK_RET, DV_RET), F32)],
        compiler_params=_params(("arbitrary", "arbitrary"), block_bytes), name="retention",
    )(*args)


def _ssd_body(c, nb, nc, has_state, *refs):
    refs = list(refs)
    (xbc_ref, z_ref, dt_ref, cw_ref, cb_ref, dtb_ref, a_ref, dsk_ref, ng_ref,
     tri_ref, eye_ref, exp_ref, sel_ref) = refs[:13]
    refs = refs[13:]
    if has_state:
        sconv_ref, sssm_ref = refs.pop(0), refs.pop(0)
    y_ref, nconv_ref, nssm_ref, ext_scr, st_scr = refs
    ci = pl.program_id(1)
    pad = SUBLANE
    hist = CONV_K - 1

    @pl.when(ci == 0)
    def _():
        ext_scr[:, 0:pad, :] = jnp.zeros((nb, pad, CONV_DIM), F32)
        if has_state:
            ext_scr[:, pad - hist:pad, :] = sconv_ref[...]
            st_scr[...] = sssm_ref[...]
        else:
            st_scr[...] = jnp.zeros(st_scr.shape, F32)

    causal = (lax.broadcasted_iota(jnp.int32, (c, c), 0) >= lax.broadcasted_iota(jnp.int32, (c, c), 1))
    left_half = (lax.broadcasted_iota(jnp.int32, (c, SSD_W), 1) % LANE) < P_SSD
    tri, eye, expand = tri_ref[...], eye_ref[...], exp_ref[...]

    for s in range(nb):
        rows = slice(s * c, (s + 1) * c)
        ext_scr[s, pad:pad + c, :] = xbc_ref[rows, :]
        acc = ext_scr[s, pad - hist:pad - hist + c, :] * cw_ref[0:1, :]
        for j in range(1, CONV_K):
            acc = acc + ext_scr[s, pad - hist + j:pad - hist + j + c, :] * cw_ref[j:j + 1, :]
        conv = jax.nn.silu(cb_ref[...] + acc)

        @pl.when(ci == nc - 1)
        def _():
            nconv_ref[s] = ext_scr[s, pad + c - hist:pad + c, :]

        ext_scr[s, 0:pad, :] = ext_scr[s, c:c + pad, :]

        xs = conv[:, :SSD_W]
        bm = conv[:, SSD_W:SSD_W + G_SSD * N_SSD]
        cm = conv[:, SSD_W + G_SSD * N_SSD:]
        dt = jax.nn.softplus(dt_ref[rows, :] + dtb_ref[...])
        dta = dt * a_ref[...]
        cum = sum(_dot(tri, p) for p in _split3(dta))
        cum_parts = _split3(cum)
        cum_t = sum(_dot_nt(eye, p) for p in cum_parts)
        cum_e = sum(_dot(p, expand) for p in cum_parts)
        dt_e = sum(_dot(p, expand) for p in _split3(dt))
        last_e = cum_e[c - 1:c, :]
        xdt = xs * dt_e
        xw = (xs * (jnp.exp(last_e - cum_e) * dt_e)).astype(BF16)
        x_l = jnp.where(left_half, xdt, 0.0).astype(BF16)
        x_r = jnp.where(left_half, 0.0, xdt).astype(BF16)

        y_parts, off_parts = [], []
        for g in range(G_SSD):
            bm_g = bm[:, g * N_SSD:(g + 1) * N_SSD].astype(BF16)
            cm_g = cm[:, g * N_SSD:(g + 1) * N_SSD].astype(BF16)
            cb = _dot_nt(cm_g, bm_g)
            state = st_scr[s, g]
            off_parts.append(_dot_nt(cm_g, state.astype(BF16)))
            decay = jnp.exp(jnp.sum(sel_ref[g] * cum[c - 1:c, :], axis=1, keepdims=True))
            st_scr[s, g] = decay * state + _dot_tn(xw[:, g * GRP_W:(g + 1) * GRP_W], bm_g)
            for jj in range(R_SSD // 2):
                j = g * (R_SSD // 2) + jj
                w = []
                for hd in (2 * j, 2 * j + 1):
                    seg = cum[:, hd:hd + 1] - cum_t[hd:hd + 1, :]
                    w.append((jnp.exp(jnp.where(causal, seg, -jnp.inf)) * cb).astype(BF16))
                cols = slice(j * LANE, (j + 1) * LANE)
                y_parts.append(_dot(w[0], x_l[:, cols]) + _dot(w[1], x_r[:, cols]))
        y = jnp.concatenate(y_parts, axis=1) + jnp.exp(cum_e) * jnp.concatenate(off_parts, axis=1)
        y = (y + dsk_ref[...] * xs) * jax.nn.silu(z_ref[rows, :])
        normed = []
        for g in range(G_SSD):
            yg = y[:, g * GRP_W:(g + 1) * GRP_W]
            normed.append(yg * lax.rsqrt(jnp.mean(yg * yg, axis=-1, keepdims=True) + EPS))
        y_ref[rows, :] = (jnp.concatenate(normed, axis=1) * ng_ref[...]).astype(y_ref.dtype)

    @pl.when(ci == nc - 1)
    def _():
        nssm_ref[...] = st_scr[...]


def _ssd(xbc, z, dt, lp, s_conv, s_ssm, batch, length):
    c = math.gcd(length, CHUNK)
    nc = length // c
    nb = 1 if nc > 1 else SAMPLE_SEQS_PER_STEP
    assert batch % nb == 0 and c % SUBLANE == 0
    has_state = s_conv is not None
    t = batch * length
    rows = nb * c
    tri = (jnp.arange(c)[:, None] >= jnp.arange(c)[None, :]).astype(BF16)
    eye = jnp.eye(LANE, dtype=BF16)
    head_of_ch = jnp.arange(SSD_W) // P_SSD
    expand = (jnp.arange(LANE)[:, None] == head_of_ch[None, :]).astype(BF16)
    sel = (head_of_ch[:, None] == jnp.arange(LANE)[None, :]).astype(F32).reshape(G_SSD, GRP_W, LANE)

    seq = lambda shape: pl.BlockSpec((nb,) + shape, lambda b, ci: (b,) + (0,) * len(shape))
    rowb = lambda w: pl.BlockSpec((rows, w), lambda b, ci: (b * nc + ci, 0))
    in_specs = [rowb(CONV_DIM), rowb(SSD_W), rowb(LANE),
                _full((CONV_K, CONV_DIM)), _full((1, CONV_DIM)), _full((1, LANE)), _full((1, LANE)),
                _full((1, SSD_W)), _full((1, SSD_W)),
                _full((c, c)), _full((LANE, LANE)), _full((LANE, SSD_W)), _full((G_SSD, GRP_W, LANE))]
    args = [xbc, z, dt, lp["conv_w"], lp["conv_b"], lp["dt_bias"], lp["a"], lp["d_skip"], lp["ssd_norm_g"],
            tri, eye, expand, sel]
    if has_state:
        in_specs += [seq((CONV_K - 1, CONV_DIM)), seq((G_SSD, GRP_W, N_SSD))]
        args += [s_conv, s_ssm.reshape(batch, G_SSD, GRP_W, N_SSD)]
    block_bytes = (_nbytes((rows, CONV_DIM + 2 * SSD_W + LANE), F32) + _nbytes((nb, c + SUBLANE, CONV_DIM), F32)
                   + 3 * _nbytes((nb, SSD_W, N_SSD), F32) + _nbytes((LANE, SSD_W), F32)
                   + 12 * _nbytes((c, SSD_W), F32))
    y, nconv, nssm = pl.pallas_call(
        functools.partial(_ssd_body, c, nb, nc, has_state),
        grid=(batch // nb, nc), in_specs=in_specs,
        out_specs=[rowb(SSD_W), seq((CONV_K - 1, CONV_DIM)), seq((G_SSD, GRP_W, N_SSD))],
        out_shape=[jax.ShapeDtypeStruct((t, SSD_W), BF16),
                   jax.ShapeDtypeStruct((batch, CONV_K - 1, CONV_DIM), F32),
                   jax.ShapeDtypeStruct((batch, G_SSD, GRP_W, N_SSD), F32)],
        scratch_shapes=[pltpu.VMEM((nb, c + SUBLANE, CONV_DIM), F32),
                        pltpu.VMEM((nb, G_SSD, GRP_W, N_SSD), F32)],
        compiler_params=_params(("arbitrary", "arbitrary"), block_bytes), name="ssd",
    )(*args)
    return y, nconv, nssm.reshape(batch, H_SSD, P_SSD, N_SSD)


def _first_argmax(vals):
    best = vals[0]
    for v in vals[1:]:
        best = jnp.maximum(best, v)
    idx = jnp.full(best.shape, len(vals) - 1, jnp.int32)
    for i in range(len(vals) - 2, -1, -1):
        idx = jnp.where(vals[i] == best, i, idx)
    return best, idx


def _merge_body(bounds, *refs):
    n_groups = len(bounds)
    acts, refs = refs[:4 * n_groups], refs[4 * n_groups:]
    weights, refs = refs[:7], refs[7:]
    x2_refs, shared = refs[:n_groups], refs[n_groups:]
    i = pl.program_id(0)
    for gi, (first, count) in enumerate(bounds):
        @pl.when((i >= first) & (i < first + count))
        def _():
            _merge_tile(*acts[4 * gi:4 * gi + 4], *weights, x2_refs[gi], *shared)


def _merge_tile(ret_ref, ssd_ref, gates_ref, x_ref, wr_ref, ws_ref, wo_ref, g2_ref, rhi_ref, rlo_ref, low_ref,
                x2_ref, hw_ref, route_ref, cnt_ref):
    ret_y = _dot(ret_ref[...], wr_ref[...])
    ssd_y = _dot(ssd_ref[...], ws_ref[...])
    m = (jax.nn.sigmoid(gates_ref[:, :D_MODEL]) * ret_y + jax.nn.sigmoid(gates_ref[:, D_MODEL:]) * ssd_y)
    x2 = x_ref[...] + _dot(m.astype(BF16), wo_ref[...])
    x2_ref[...] = x2
    h2 = _rmsnorm(x2, g2_ref[...])
    bits = pltpu.bitcast(h2.astype(BF16).astype(F32), jnp.uint32)
    half = D_MODEL // 2
    word = lax.bitwise_or(lax.bitwise_and(bits[:, half:], jnp.uint32(HIGH_HALF)),
                          lax.shift_right_logical(bits[:, :half], jnp.uint32(16)))
    n_rows = h2.shape[0]
    for k in range(PACKED_ROWS):
        hw_ref[pl.ds(k, n_rows, stride=SUBLANE), :] = word[:, k * LANE:(k + 1) * LANE]

    h_hi = h2.astype(BF16)
    h_lo = (h2 - h_hi.astype(F32)).astype(BF16)
    logits = _dot(h_hi, rhi_ref[...]) + (_dot(h_hi, rlo_ref[...]) + _dot(h_lo, rhi_ref[...]))
    colv = lambda i: logits[:, i:i + 1]
    glog = [colv(i) for i in range(N_EGROUPS)]
    gmax, gidx = _first_argmax(glog)
    den = glog[0] * 0.0
    for v in glog:
        den = den + jnp.exp(v - gmax)
    gw = 1.0 / den
    sel = []
    for j in range(EXP_PER_GROUP):
        v = colv(N_EGROUPS + (N_EGROUPS - 1) * EXP_PER_GROUP + j)
        for g in range(N_EGROUPS - 2, -1, -1):
            v = jnp.where(gidx == g, colv(N_EGROUPS + g * EXP_PER_GROUP + j), v)
        sel.append(v)
    v1, i1 = _first_argmax(sel)
    v2, i2 = _first_argmax([jnp.where(i1 == j, -jnp.inf, sel[j]) for j in range(EXP_PER_GROUP)])
    e = jnp.exp(v2 - v1)
    w1 = gw / (1.0 + e)
    w2 = gw * e / (1.0 + e)
    first_low = i1 < i2
    lo = jnp.where(first_low, i1, i2)
    hi = jnp.where(first_low, i2, i1)
    pair = jnp.where(lo == 0, hi - 1, jnp.where(lo == 1, hi + 1, N_PAIRS - 1))
    cls = gidx * N_PAIRS + pair
    w_lo = jnp.where(first_low, w1, w2)
    w_hi = jnp.where(first_low, w2, w1)

    rows = logits.shape[0]
    lane = lax.broadcasted_iota(jnp.int32, (rows, META_W), 1)
    onehot = (lane == cls).astype(F32)
    before = _dot(low_ref[...], onehot.astype(BF16))
    rank = jnp.sum(onehot * before, axis=1, keepdims=True)
    meta = jnp.where(lane == 0, w_lo, jnp.where(lane == 1, w_hi, jnp.where(
        lane == 2, cls.astype(F32), jnp.where(lane == 3, rank, 0.0))))
    meta_bits = pltpu.bitcast(meta, jnp.uint32)
    for k in range(PACKED_ROWS, TOKEN_TILE_ROWS):
        hw_ref[pl.ds(k, n_rows, stride=SUBLANE), :] = meta_bits
    route_ref[...] = meta
    cnt_ref[0] = jnp.sum(onehot, axis=0, keepdims=True)


def _merge(group_acts, lp):
    counts = [a[3].shape[0] // ROW_TILE for a in group_acts]
    bounds = tuple((sum(counts[:gi]), counts[gi]) for gi in range(len(counts)))
    n_tiles = sum(counts)
    total_rows = n_tiles * ROW_TILE
    low = (jnp.arange(ROW_TILE)[:, None] > jnp.arange(ROW_TILE)[None, :]).astype(BF16)
    ws = [lp["w_ret_out"], lp["w_ssd_out"], lp["w_o"], lp["ln2_g"], lp["router_hi"], lp["router_lo"], low]
    block_bytes = (sum(_nbytes(w.shape, w.dtype) for w in ws)
                   + len(counts) * (_nbytes((ROW_TILE, RET_W + SSD_W), BF16) + _nbytes((ROW_TILE, 4 * D_MODEL), F32))
                   + _nbytes((ROW_TILE, 3 * D_MODEL), F32))

    def group_row(width, first, count):
        return pl.BlockSpec((ROW_TILE, width), lambda i: (jnp.clip(i - first, 0, count - 1), 0))

    in_specs, args, x2_specs, x2_shapes = [], [], [], []
    for (first, count), acts in zip(bounds, group_acts, strict=True):
        in_specs += [group_row(w, first, count) for w in (RET_W, SSD_W, 2 * D_MODEL, D_MODEL)]
        args += list(acts)
        x2_specs.append(group_row(D_MODEL, first, count))
        x2_shapes.append(jax.ShapeDtypeStruct((count * ROW_TILE, D_MODEL), F32))
    in_specs += [_full(w.shape) for w in ws]
    args += ws
    outs = pl.pallas_call(
        functools.partial(_merge_body, bounds), grid=(n_tiles,), in_specs=in_specs,
        out_specs=x2_specs + [_token_spec(0), pl.BlockSpec((ROW_TILE, META_W), lambda i: (i, 0)),
                              pl.BlockSpec((1, 1, META_W), lambda i: (i, 0, 0))],
        out_shape=x2_shapes + [jax.ShapeDtypeStruct((total_rows * TOKEN_TILE_ROWS, LANE), jnp.uint32),
                               jax.ShapeDtypeStruct((total_rows, META_W), F32),
                               jax.ShapeDtypeStruct((n_tiles, 1, META_W), F32)],
        compiler_params=_params(("arbitrary",), block_bytes), name="merge_router",
    )(*args)
    return outs[:len(counts)], outs[len(counts):]


PLAN_ROWS = LANE
TAB_ROWS = SUBLANE
TAB_LO, TAB_HI, TAB_USED = 0, 1, 2
BYTE = 256.0
MOE_RING = 3


def _row_of(col_vals):
    pick = (lax.broadcasted_iota(jnp.int32, (TAB_ROWS, LANE), 0)
            == lax.broadcasted_iota(jnp.int32, (TAB_ROWS, LANE), 1)).astype(BF16)
    hi = jnp.floor(col_vals * (1.0 / BYTE))
    lo = col_vals - hi * BYTE
    return BYTE * _dot_nt(pick, hi.astype(BF16)) + _dot_nt(pick, lo.astype(BF16))


def _plan_body(tile, n_row_tiles, route_ref, cnt_ref, low_ref, upp_ref, pos_ref, tab_ref, offs_scr):
    cnt = cnt_ref[...]
    tot = jnp.sum(cnt, axis=0, keepdims=True)
    cls_tiles = jnp.floor((tot + (tile - 1.0)) * (1.0 / tile))
    first_tile = _dot(jnp.broadcast_to(cls_tiles, (SUBLANE, LANE)).astype(BF16), upp_ref[...])[0:1, :]
    base = first_tile * tile
    ends = base + cls_tiles * tile
    offs_scr[...] = base + _dot(low_ref[...], cnt.astype(BF16))

    def per_row_tile(j, carry):
        meta = route_ref[pl.ds(j * ROW_TILE, ROW_TILE), :]
        lane = lax.broadcasted_iota(jnp.int32, (ROW_TILE, META_W), 1)
        onehot = lane.astype(F32) == meta[:, 2:3]
        pos = jnp.sum(jnp.where(onehot, offs_scr[pl.ds(j, 1), :], 0.0), axis=1, keepdims=True) + meta[:, 3:4]
        rows = _row_of(jnp.where(lane == 0, pos, 0.0))
        pos_ref[j] = rows[0:1, :].astype(jnp.int32)
        return carry

    lax.fori_loop(0, n_row_tiles, per_row_tile, 0)

    lane = lax.broadcasted_iota(jnp.int32, (PLAN_ROWS, LANE), 1)
    tile_idx = lax.broadcasted_iota(jnp.int32, (PLAN_ROWS, LANE), 0).astype(F32)[:, 0:1]
    n_used = jnp.sum(cls_tiles, axis=1, keepdims=True)
    start = jnp.minimum(tile_idx, n_used - 1.0) * tile
    tcls = jnp.sum(jnp.where((lane < N_CLASSES) & (ends <= start), 1.0, 0.0), axis=1, keepdims=True)
    tcls = jnp.minimum(tcls, N_CLASSES - 1.0)
    group = sum(jnp.where(tcls >= k * N_PAIRS, 1.0, 0.0) for k in range(1, N_EGROUPS))
    pair = tcls - group * N_PAIRS
    p_lo = jnp.where(pair >= 3, 1.0, 0.0) + jnp.where(pair >= 5, 1.0, 0.0)
    p_hi = jnp.where(pair == 0, 1.0, jnp.where((pair == 1) | (pair == 3), 2.0, 3.0))
    table = jnp.where(lane == TAB_LO, group * EXP_PER_GROUP + p_lo,
                      jnp.where(lane == TAB_HI, group * EXP_PER_GROUP + p_hi,
                                jnp.where(lane == TAB_USED, n_used, 0.0)))
    tab_ref[...] = _row_of(table).astype(jnp.int32)


def _plan(route, counts, tile):
    t = route.shape[0]
    nrt = t // ROW_TILE
    assert nrt <= PLAN_ROWS and t // tile + N_CLASSES <= PLAN_ROWS
    cnt = jnp.pad(counts[:, 0, :], ((0, PLAN_ROWS - nrt), (0, 0)))
    idx = jnp.arange(PLAN_ROWS)
    low = (idx[:, None] > idx[None, :]).astype(BF16)
    upp = (idx[:, None] < idx[None, :]).astype(BF16)
    pos, tab = pl.pallas_call(
        functools.partial(_plan_body, tile, nrt),
        out_shape=[jax.ShapeDtypeStruct((nrt, 1, ROW_TILE), jnp.int32),
                   jax.ShapeDtypeStruct((TAB_ROWS, PLAN_ROWS), jnp.int32)],
        scratch_shapes=[pltpu.VMEM((PLAN_ROWS, LANE), F32)],
        compiler_params=pltpu.CompilerParams(vmem_limit_bytes=_vmem_limit(_nbytes(route.shape, F32))),
        name="moe_plan",
    )(route, cnt, low, upp)
    return pos.reshape(t), tab


def _moe_body(tile, n_tok, pos_ref, tab_ref, hw_hbm, spare_hbm, wgl_ref, wul_ref, wdl_ref, wgh_ref, wuh_ref,
              wdh_ref, y_hbm, buf, obuf, tok_smem, gsem, ssem, isem):
    i = pl.program_id(0)
    n_used = tab_ref[TAB_USED, 0]
    ring = buf.shape[0]
    cur = lax.rem(i, ring)
    nxt1 = lax.rem(i + 1, ring)
    nxt2 = lax.rem(i + 2, ring)
    n_points = 16
    per_point = tile // n_points
    tr = TOKEN_TILE_ROWS

    def token_tile(ref, token):
        start = token * tr if isinstance(token, int) else pl.multiple_of(token * tr, tr)
        return ref.at[pl.ds(start, tr)]

    def dma_thread(r):
        return r % 2 if isinstance(r, int) else 0

    def gather_start(block, r, sl):
        src = jnp.minimum(tok_smem[block * tile + r], n_tok - 1)
        pltpu.make_async_copy(token_tile(hw_hbm, src), token_tile(buf.at[sl], r), gsem.at[sl]).start(dma_thread(r))

    def scatter_start(block, r, sl):
        dst = tok_smem[block * tile + r]
        pltpu.make_async_copy(token_tile(obuf.at[sl], r), token_tile(y_hbm, dst), ssem.at[sl]).start(dma_thread(r))

    def gather_wait(sl):
        pltpu.make_async_copy(hw_hbm.at[pl.ds(0, tile * tr)], buf.at[sl], gsem.at[sl]).wait()

    def scatter_wait(sl):
        pltpu.make_async_copy(obuf.at[sl], y_hbm.at[pl.ds(0, tile * tr)], ssem.at[sl]).wait()

    def for_rows(n, fn):
        lax.fori_loop(0, n, lambda r, carry: (fn(r), carry)[1], 0, unroll=8)

    def block_of(t):
        return jnp.minimum(t, n_used - 1) + 1

    @pl.when(i < n_used)
    def _():
        @pl.when(i == 0)
        def _():
            fill = pltpu.make_async_copy(spare_hbm, tok_smem, isem)
            fill.start()
            fill.wait()

            def place(t):
                tok_smem[tile + pos_ref[t]] = t
            for_rows(n_tok, place)
            obuf[ring - 1] = jnp.zeros(obuf.shape[1:], F32)
            for_rows(tile, lambda r: gather_start(block_of(0), r, 0))
            for_rows(tile, lambda r: gather_start(block_of(1), r, 1))

        gather_wait(cur)
        issued = [0]

        def issue_point():
            for r in range(issued[0] * per_point, (issued[0] + 1) * per_point):
                gather_start(block_of(i + 2), r, nxt2)
                scatter_start(i, r, nxt2)
            issued[0] += 1

        def tile_row(k):
            return buf[cur, pl.ds(k, tile, stride=tr), :]

        words = [tile_row(k) for k in range(PACKED_ROWS)]
        low = [pltpu.bitcast(lax.shift_left(w, jnp.uint32(16)), F32).astype(BF16) for w in words]
        high = [pltpu.bitcast(lax.bitwise_and(w, jnp.uint32(HIGH_HALF)), F32).astype(BF16) for w in words]
        xb = jnp.concatenate(low + high, axis=1)
        meta = pltpu.bitcast(tile_row(PACKED_ROWS), F32)
        w_lo, w_hi = meta[:, 0:1], meta[:, 1:2]
        half = D_FF // 2
        acts = []
        for wg_ref, wu_ref in ((wgl_ref, wul_ref), (wgh_ref, wuh_ref)):
            parts = []
            for n in range(2):
                gate = _dot(xb, wg_ref[:, n * half:(n + 1) * half])
                issue_point()
                up = _dot(xb, wu_ref[:, n * half:(n + 1) * half])
                issue_point()
                parts.append((jax.nn.silu(gate) * up).astype(BF16))
            acts.append(jnp.concatenate(parts, axis=1))
        quarter = D_MODEL // 4
        for n in range(4):
            cols = slice(n * quarter, (n + 1) * quarter)
            d_lo = _dot(acts[0], wdl_ref[:, cols])
            issue_point()
            d_hi = _dot(acts[1], wdh_ref[:, cols])
            issue_point()
            out = d_lo * w_lo + d_hi * w_hi
            for kk in range(quarter // LANE):
                obuf[cur, pl.ds(n * (quarter // LANE) + kk, tile, stride=tr), :] = out[:, kk * LANE:(kk + 1) * LANE]
        assert issued[0] == n_points

        @pl.when(i >= 1)
        def _():
            scatter_wait(nxt1)

        @pl.when(i == n_used - 1)
        def _():
            scatter_wait(nxt2)
            for_rows(tile, lambda r: scatter_start(i + 1, r, cur))
            scatter_wait(cur)
            gather_wait(nxt1)
            gather_wait(nxt2)


def _moe(hw, pos, tab, lp, tile):
    t = hw.shape[0] // TOKEN_TILE_ROWS
    n_tiles = t // tile + N_CLASSES
    wspec = lambda shape, row: pl.BlockSpec((None,) + shape, lambda i, pos, tab: (tab[row, i], 0, 0))
    in_specs = [pl.BlockSpec(memory_space=pl.ANY)] * 2
    spare = t + jnp.arange((n_tiles + 1) * tile, dtype=jnp.int32) % tile
    for row in (TAB_LO, TAB_HI):
        in_specs += [wspec((D_MODEL, D_FF), row), wspec((D_MODEL, D_FF), row), wspec((D_FF, D_MODEL), row)]
    block_bytes = 6 * _nbytes((D_MODEL, D_FF), BF16) + 6 * _nbytes((tile, D_MODEL), F32)
    tile_rows = tile * TOKEN_TILE_ROWS
    grid_spec = pltpu.PrefetchScalarGridSpec(
        num_scalar_prefetch=2, grid=(n_tiles,), in_specs=in_specs,
        out_specs=pl.BlockSpec(memory_space=pl.ANY),
        scratch_shapes=[pltpu.VMEM((MOE_RING, tile_rows, LANE), jnp.uint32),
                        pltpu.VMEM((MOE_RING, tile_rows, LANE), F32),
                        pltpu.SMEM(((n_tiles + 1) * tile,), jnp.int32),
                        pltpu.SemaphoreType.DMA((MOE_RING,)), pltpu.SemaphoreType.DMA((MOE_RING,)),
                        pltpu.SemaphoreType.DMA(())])
    return pl.pallas_call(
        functools.partial(_moe_body, tile, t), grid_spec=grid_spec,
        out_shape=jax.ShapeDtypeStruct(((t + tile) * TOKEN_TILE_ROWS, LANE), F32),
        compiler_params=_params(("arbitrary",), block_bytes), name="moe",
    )(pos, tab, hw, spare, lp["w_gate"], lp["w_up"], lp["w_down"], lp["w_gate"], lp["w_up"], lp["w_down"])


def _final_body(x_ref, y_ref, g_ref, o_ref):
    o_ref[...] = _rmsnorm(x_ref[...] + _token_rows(y_ref, ROW_TILE), g_ref[...])


def _final_norm(x, y, y_offset, g):
    t = x.shape[0]
    row = pl.BlockSpec((ROW_TILE, D_MODEL), lambda i: (i, 0))
    return pl.pallas_call(
        _final_body, grid=(t // ROW_TILE,), in_specs=[row, _token_spec(y_offset), _full((1, D_MODEL))],
        out_specs=row, out_shape=jax.ShapeDtypeStruct((t, D_MODEL), F32),
        compiler_params=_params(("arbitrary",), 3 * _nbytes((ROW_TILE, D_MODEL), F32)), name="final_norm",
    )(x, y, g)


def _layer_params(i, ln1_g, w_in, ret_gn_g, w_ret_out, conv_w, conv_b, dt_bias, a_log, d_skip, ssd_norm_g,
                  w_ssd_out, w_o, ln2_g, w_rg, w_re, w_gate, w_up, w_down):
    offs = [0]
    for s in IN_SIZES:
        offs.append(offs[-1] + s)
    w = w_in[i]
    pad_heads = lambda v: jnp.pad(v.astype(F32), (0, LANE - H_SSD)).reshape(1, LANE)
    router = jnp.pad(jnp.concatenate([w_rg[i], w_re[i]], axis=1),
                     ((0, 0), (0, META_W - N_EGROUPS - N_EXPERTS)))
    router_hi = router.astype(BF16)
    return {
        "ln1_g": ln1_g[i].reshape(1, D_MODEL),
        "w_in": [w[:, offs[0]:offs[4]].astype(BF16),
                 w[:, offs[4]:offs[5]].astype(BF16),
                 w[:, offs[5]:offs[6]].astype(BF16),
                 jnp.pad(w[:, offs[6]:offs[7]], ((0, 0), (0, LANE - H_SSD))).astype(BF16),
                 w[:, offs[7]:offs[9]].astype(BF16)],
        "ret_gn_g": ret_gn_g[i].reshape(1, RET_W),
        "w_ret_out": w_ret_out[i].astype(BF16),
        "conv_w": conv_w[i], "conv_b": conv_b[i].reshape(1, CONV_DIM),
        "dt_bias": pad_heads(dt_bias[i]), "a": pad_heads(-jnp.exp(a_log[i].astype(F32))),
        "d_skip": jnp.repeat(d_skip[i].astype(F32), P_SSD).reshape(1, SSD_W),
        "ssd_norm_g": ssd_norm_g[i].reshape(1, SSD_W),
        "w_ssd_out": w_ssd_out[i].astype(BF16), "w_o": w_o[i].astype(BF16),
        "ln2_g": ln2_g[i].reshape(1, D_MODEL),
        "router_hi": router_hi, "router_lo": (router - router_hi.astype(F32)).astype(BF16),
        "w_gate": w_gate[i].astype(BF16), "w_up": w_up[i].astype(BF16), "w_down": w_down[i].astype(BF16),
    }


class _Group:
    def __init__(self, x, start, states, row_offset):
        self.batch, self.length, _ = x.shape
        self.x = x.reshape(self.batch * self.length, D_MODEL)
        self.start, self.states, self.row_offset = start, states, row_offset
        self.rets, self.ssms, self.convs = [], [], []


def kernel(x_prompt, x_sample, state_ret, state_ssm, state_conv, ln1_g, w_in, ret_gn_g, w_ret_out, conv_w, conv_b, dt_bias, a_log, d_skip, ssd_norm_g, w_ssd_out, w_o, ln2_g, w_router_group, w_router_expert, w_e_gate, w_e_up, w_e_down, lnf_g):
    depth = w_in.shape[0]
    past_len = 16384.0
    layers = [_layer_params(i, ln1_g, w_in, ret_gn_g, w_ret_out, conv_w, conv_b, dt_bias, a_log, d_skip,
                            ssd_norm_g, w_ssd_out, w_o, ln2_g, w_router_group, w_router_expert,
                            w_e_gate, w_e_up, w_e_down) for i in range(depth)]
    n_prompt = x_prompt.shape[0] * x_prompt.shape[1]
    groups = [_Group(x_prompt, 0.0, None, 0),
              _Group(x_sample, past_len, (state_ret, state_ssm, state_conv), n_prompt)]
    y = None
    for i, lp in enumerate(layers):
        group_acts = []
        for grp in groups:
            s_ret, s_ssm, s_conv = (None, None, None) if grp.states is None else (s[i] for s in grp.states)
            outs = _inproj(grp.x, y, grp.row_offset, lp["ln1_g"], lp["w_in"])
            if y is not None:
                grp.x, outs = outs[0], outs[1:]
            qkvr, z, xbc, dt, gates = outs
            ret, new_ret = _retention(qkvr, grp.start, lp["ret_gn_g"], s_ret, grp.batch, grp.length)
            ssd, new_conv, new_ssm = _ssd(xbc, z, dt, lp, s_conv, s_ssm, grp.batch, grp.length)
            grp.rets.append(new_ret)
            grp.ssms.append(new_ssm)
            grp.convs.append(new_conv)
            group_acts.append((ret, ssd, gates, grp.x))
        new_x, (hw, route, counts) = _merge(group_acts, lp)
        for grp, x2 in zip(groups, new_x, strict=True):
            grp.x = x2
        pos, tab = _plan(route, counts, ROW_TILE)
        y = _moe(hw, pos, tab, lp, ROW_TILE)
    outs = []
    for grp in groups:
        out = _final_norm(grp.x, y, grp.row_offset, lnf_g.reshape(1, D_MODEL))
        outs.append(out.reshape(grp.batch, grp.length, D_MODEL))
    states = []
    for grp in groups:
        states += [jnp.stack(grp.rets), jnp.stack(grp.ssms), jnp.stack(grp.convs)]
    return tuple(outs + states)
```

```python
import functools
import math

import jax
import jax.numpy as jnp
from jax import lax
from jax.experimental import pallas as pl
from jax.experimental.pallas import tpu as pltpu

F32 = jnp.float32
BF16 = jnp.bfloat16

D_MODEL = 1024
H_RET, DK_RET, DV_RET = 4, 128, 128
RET_W = H_RET * DV_RET
ROPE_BASE = 10000.0
H_SSD, P_SSD, G_SSD, N_SSD = 16, 64, 2, 64
R_SSD = H_SSD // G_SSD
SSD_W = H_SSD * P_SSD
GRP_W = SSD_W // G_SSD
CONV_K = 4
CONV_DIM = SSD_W + 2 * G_SSD * N_SSD
CHUNK = 128
N_EGROUPS, EXP_PER_GROUP = 4, 4
N_EXPERTS = N_EGROUPS * EXP_PER_GROUP
N_PAIRS = EXP_PER_GROUP * (EXP_PER_GROUP - 1) // 2
N_CLASSES = N_EGROUPS * N_PAIRS
D_FF = 512
EPS = 1e-6
IN_SIZES = [RET_W, RET_W, RET_W, RET_W, SSD_W, CONV_DIM, H_SSD, D_MODEL, D_MODEL]

LANE = 128
SUBLANE = 8
MIB = 1024 * 1024

ROW_TILE = 256
META_W = LANE
TOKEN_TILE_ROWS = D_MODEL // LANE
assert TOKEN_TILE_ROWS == SUBLANE
PACKED_ROWS = D_MODEL // (2 * LANE)
HIGH_HALF = 0xFFFF0000
SAMPLE_SEQS_PER_STEP = 8


def _vmem_limit(block_bytes):
    return int(min(2 * block_bytes + 16 * MIB, 56 * MIB))


def _params(semantics, block_bytes):
    return pltpu.CompilerParams(dimension_semantics=semantics,
                                vmem_limit_bytes=_vmem_limit(block_bytes))


def _nbytes(shape, dtype):
    return math.prod(shape) * jnp.dtype(dtype).itemsize


def _full(shape):
    return pl.BlockSpec(shape, lambda *_: (0,) * len(shape))


def _dot(a, b):
    return jnp.dot(a, b, preferred_element_type=F32)


def _dot_nt(a, b):
    return lax.dot_general(a, b, (((1,), (1,)), ((), ())), preferred_element_type=F32)


def _dot_tn(a, b):
    return lax.dot_general(a, b, (((0,), (0,)), ((), ())), preferred_element_type=F32)


def _split3(a):
    hi = a.astype(BF16)
    r1 = a - hi.astype(F32)
    mid = r1.astype(BF16)
    lo = (r1 - mid.astype(F32)).astype(BF16)
    return hi, mid, lo


def _rmsnorm(x, g):
    r = lax.rsqrt(jnp.mean(x * x, axis=-1, keepdims=True) + EPS)
    return x * r * g


def _token_rows(ref, rows):
    return jnp.concatenate([ref[pl.ds(k, rows, stride=SUBLANE), :] for k in range(TOKEN_TILE_ROWS)], axis=1)


def _token_spec(row_offset):
    first = row_offset // ROW_TILE
    return pl.BlockSpec((ROW_TILE * TOKEN_TILE_ROWS, LANE), lambda i: (i + first, 0))


def _inproj_body(has_add, *refs):
    n_in = 6 if has_add else 5
    ins, outs = refs[:n_in], refs[n_in:]
    if has_add:
        x_ref, y_ref, g_ref = ins[:3]
        x = x_ref[...] + _token_rows(y_ref, ROW_TILE)
        outs[0][...] = x
        outs = outs[1:]
    else:
        x_ref, g_ref = ins[:2]
        x = x_ref[...]
    w_main, w_dt, w_gates = ins[-3:]
    h = _rmsnorm(x, g_ref[...]).astype(BF16)
    col = 0
    for o_ref in outs[:3]:
        width = o_ref.shape[1]
        o_ref[...] = _dot(h, w_main[:, col:col + width])
        col += width
    outs[3][...] = _dot(h, w_dt[...])
    outs[4][...] = _dot(h, w_gates[...])


def _layer_block(arr, layer):
    shape = arr.shape[1:]
    return pl.BlockSpec((None,) + shape, lambda *_: (layer,) + (0,) * len(shape))


def _inproj(x, y, y_offset, g, ws, layer):
    t = x.shape[0]
    has_add = y is not None
    widths = [4 * RET_W, SSD_W, CONV_DIM, LANE, 2 * D_MODEL]
    assert sum(widths[:3]) == sum(IN_SIZES[:6])
    row = lambda w: pl.BlockSpec((ROW_TILE, w), lambda i: (i, 0))
    in_specs = [row(D_MODEL)] + ([_token_spec(y_offset)] if has_add else []) + [_full((1, D_MODEL))]
    in_specs += [_layer_block(w, layer) for w in ws]
    out_shape = [jax.ShapeDtypeStruct((t, w), F32) for w in widths]
    out_specs = [row(w) for w in widths]
    if has_add:
        out_shape = [jax.ShapeDtypeStruct((t, D_MODEL), F32)] + out_shape
        out_specs = [row(D_MODEL)] + out_specs
    block_bytes = (sum(_nbytes(w.shape[1:], BF16) for w in ws)
                   + _nbytes((ROW_TILE, sum(widths) + 3 * D_MODEL), F32))
    args = ([x, y] if has_add else [x]) + [g] + list(ws)
    return pl.pallas_call(
        functools.partial(_inproj_body, has_add),
        grid=(t // ROW_TILE,), in_specs=in_specs, out_specs=out_specs, out_shape=out_shape,
        compiler_params=_params(("arbitrary",), block_bytes), name="inproj",
    )(*args)


def _retention_body(c, nb, nc, has_state, n_prev, *refs):
    refs = list(refs)
    qkvr_ref, cos_ref, sin_ref, dmat_ref, qdec_ref, kdec_ref, cdec_ref, gn_ref = refs[:8]
    refs = refs[8:]
    if has_state:
        s0_ref = refs.pop(0)
    prev_refs = [refs.pop(0) for _ in range(n_prev)]
    o_ref, ns_ref, st_scr = refs
    ci = pl.program_id(1)

    @pl.when(ci == 0)
    def _():
        if has_state:
            st_scr[...] = s0_ref[...]
        else:
            st_scr[...] = jnp.zeros(st_scr.shape, F32)

    for s in range(nb):
        rows = slice(s * c, (s + 1) * c)
        cosv, sinv = cos_ref[rows, :], sin_ref[rows, :]
        for h in range(H_RET):
            col = lambda j: slice(j * RET_W + h * DK_RET, j * RET_W + (h + 1) * DK_RET)
            q, k = qkvr_ref[rows, col(0)], qkvr_ref[rows, col(1)]
            v, rg = qkvr_ref[rows, col(2)], qkvr_ref[rows, col(3)]
            qr = q * cosv + pltpu.roll(q, DK_RET // 2, axis=1) * sinv
            kr = (k * cosv + pltpu.roll(k, DK_RET // 2, axis=1) * sinv) * (DK_RET ** -0.5)
            qb, kb, vb = qr.astype(BF16), kr.astype(BF16), v.astype(BF16)
            state = st_scr[s, h]
            sc = _dot_nt(qb, kb) * dmat_ref[h]
            o = _dot(sc.astype(BF16), vb) + _dot(qb, state.astype(BF16)) * qdec_ref[h]
            st_scr[s, h] = cdec_ref[h] * state + _dot_tn((kr * kdec_ref[h]).astype(BF16), vb)
            mu = jnp.mean(o, axis=-1, keepdims=True)
            d = o - mu
            var = jnp.mean(d * d, axis=-1, keepdims=True)
            on = d * lax.rsqrt(var + EPS) * gn_ref[:, h * DV_RET:(h + 1) * DV_RET]
            o_ref[rows, h * DV_RET:(h + 1) * DV_RET] = (jax.nn.silu(rg) * on).astype(o_ref.dtype)

    @pl.when(ci == nc - 1)
    def _():
        if n_prev:
            for l, prev_ref in enumerate(prev_refs):
                ns_ref[l] = prev_ref[...]
            ns_ref[n_prev] = st_scr[...]
        else:
            ns_ref[...] = st_scr[...]


def _state_out(prev, batch, nb, shape):
    zeros = (0,) * len(shape)
    if prev:
        depth = len(prev) + 1
        return (pl.BlockSpec((depth, nb) + shape, lambda b, ci: (0, b) + zeros),
                jax.ShapeDtypeStruct((depth, batch) + shape, F32))
    return pl.BlockSpec((nb,) + shape, lambda b, ci: (b,) + zeros), jax.ShapeDtypeStruct((batch,) + shape, F32)


def _retention(qkvr, start, gn_g, states, layer, prev, batch, length):
    c = math.gcd(length, CHUNK)
    nc = length // c
    nb = 1 if nc > 1 else SAMPLE_SEQS_PER_STEP
    assert batch % nb == 0
    has_state = states is not None
    t = batch * length
    half = DK_RET // 2
    pos = start + jnp.arange(length, dtype=F32)
    inv = ROPE_BASE ** (-jnp.arange(half, dtype=F32) / half)
    ang = pos[:, None] * inv[None, :]
    cosv = jnp.tile(jnp.concatenate([jnp.cos(ang), jnp.cos(ang)], axis=1), (nb, 1))
    sinv = jnp.tile(jnp.concatenate([-jnp.sin(ang), jnp.sin(ang)], axis=1), (nb, 1))
    log_g = jnp.log1p(-jnp.exp2(-5.0 - jnp.arange(H_RET, dtype=F32)))
    idx = jnp.arange(c, dtype=F32)
    rel = idx[:, None] - idx[None, :]
    causal = rel >= 0
    dmat = jnp.where(causal[None], jnp.exp(jnp.where(causal, rel, 0.0)[None] * log_g[:, None, None]), 0.0)
    qdec = jnp.broadcast_to(jnp.exp((idx[None, :] + 1.0) * log_g[:, None])[:, :, None], (H_RET, c, LANE))
    kdec = jnp.broadcast_to(jnp.exp((c - 1.0 - idx[None, :]) * log_g[:, None])[:, :, None], (H_RET, c, LANE))
    cdec = jnp.exp(c * log_g)

    rows = nb * c
    state_spec = pl.BlockSpec((nb, H_RET, DK_RET, DV_RET), lambda b, ci: (b, 0, 0, 0))
    in_specs = [
        pl.BlockSpec((rows, 4 * RET_W), lambda b, ci: (b * nc + ci, 0)),
        pl.BlockSpec((rows, LANE), lambda b, ci: (ci, 0)),
        pl.BlockSpec((rows, LANE), lambda b, ci: (ci, 0)),
        _full((H_RET, c, c)), _full((H_RET, c, LANE)), _full((H_RET, c, LANE)),
        pl.BlockSpec(memory_space=pltpu.SMEM),
        _full((1, RET_W)),
    ]
    args = [qkvr, cosv, sinv, dmat, qdec, kdec, cdec, gn_g]
    if has_state:
        in_specs.append(pl.BlockSpec((None, nb, H_RET, DK_RET, DV_RET), lambda b, ci: (layer, b, 0, 0, 0)))
        args.append(states)
    in_specs += [state_spec] * len(prev)
    args += list(prev)
    block_bytes = (_nbytes((rows, 4 * RET_W + 2 * LANE + RET_W), F32)
                   + (3 + 2 * len(prev)) * _nbytes((nb, H_RET, DK_RET, DV_RET), F32)
                   + 3 * _nbytes((H_RET, c, LANE), F32))
    ns_spec, ns_shape = _state_out(prev, batch, nb, (H_RET, DK_RET, DV_RET))
    return pl.pallas_call(
        functools.partial(_retention_body, c, nb, nc, has_state, len(prev)),
        grid=(batch // nb, nc), in_specs=in_specs,
        out_specs=[pl.BlockSpec((rows, RET_W), lambda b, ci: (b * nc + ci, 0)), ns_spec],
        out_shape=[jax.ShapeDtypeStruct((t, RET_W), BF16), ns_shape],
        scratch_shapes=[pltpu.VMEM((nb, H_RET, DK_RET, DV_RET), F32)],
        compiler_params=_params(("arbitrary", "arbitrary"), block_bytes), name="retention",
    )(*args)


def _ssd_body(c, nb, nc, has_state, n_prev, *refs):
    refs = list(refs)
    (xbc_ref, z_ref, dt_ref, cw_ref, cb_ref, dtb_ref, a_ref, dsk_ref, ng_ref,
     tri_ref, eye_ref, exp_ref, sel_ref) = refs[:13]
    refs = refs[13:]
    if has_state:
        sconv_ref, sssm_ref = refs.pop(0), refs.pop(0)
    prev_conv = [refs.pop(0) for _ in range(n_prev)]
    prev_ssm = [refs.pop(0) for _ in range(n_prev)]
    y_ref, nconv_ref, nssm_ref, ext_scr, st_scr = refs
    new_conv = nconv_ref.at[n_prev] if n_prev else nconv_ref
    new_ssm = nssm_ref.at[n_prev] if n_prev else nssm_ref
    ci = pl.program_id(1)
    pad = SUBLANE
    hist = CONV_K - 1

    @pl.when(ci == 0)
    def _():
        ext_scr[:, 0:pad, :] = jnp.zeros((nb, pad, CONV_DIM), F32)
        if has_state:
            ext_scr[:, pad - hist:pad, :] = sconv_ref[...]
            for s in range(nb):
                for g in range(G_SSD):
                    for r in range(R_SSD):
                        st_scr[s, g, r * P_SSD:(r + 1) * P_SSD, :] = sssm_ref[s, g * R_SSD + r]
        else:
            st_scr[...] = jnp.zeros(st_scr.shape, F32)

    causal = (lax.broadcasted_iota(jnp.int32, (c, c), 0) >= lax.broadcasted_iota(jnp.int32, (c, c), 1))
    left_half = (lax.broadcasted_iota(jnp.int32, (c, SSD_W), 1) % LANE) < P_SSD
    tri, eye, expand = tri_ref[...], eye_ref[...], exp_ref[...]

    for s in range(nb):
        rows = slice(s * c, (s + 1) * c)
        ext_scr[s, pad:pad + c, :] = xbc_ref[rows, :]
        acc = ext_scr[s, pad - hist:pad - hist + c, :] * cw_ref[0:1, :]
        for j in range(1, CONV_K):
            acc = acc + ext_scr[s, pad - hist + j:pad - hist + j + c, :] * cw_ref[j:j + 1, :]
        conv = jax.nn.silu(cb_ref[...] + acc)

        @pl.when(ci == nc - 1)
        def _():
            new_conv[s] = ext_scr[s, pad + c - hist:pad + c, :]

        ext_scr[s, 0:pad, :] = ext_scr[s, c:c + pad, :]

        xs = conv[:, :SSD_W]
        bm = conv[:, SSD_W:SSD_W + G_SSD * N_SSD]
        cm = conv[:, SSD_W + G_SSD * N_SSD:]
        dt = jax.nn.softplus(dt_ref[rows, :] + dtb_ref[...])
        dta = dt * a_ref[...]
        cum = sum(_dot(tri, p) for p in _split3(dta))
        cum_parts = _split3(cum)
        cum_t = sum(_dot_nt(eye, p) for p in cum_parts)
        cum_e = sum(_dot(p, expand) for p in cum_parts)
        dt_e = sum(_dot(p, expand) for p in _split3(dt))
        last_e = cum_e[c - 1:c, :]
        xdt = xs * dt_e
        xw = (xs * (jnp.exp(last_e - cum_e) * dt_e)).astype(BF16)
        x_l = jnp.where(left_half, xdt, 0.0).astype(BF16)
        x_r = jnp.where(left_half, 0.0, xdt).astype(BF16)

        y_parts, off_parts = [], []
        for g in range(G_SSD):
            bm_g = bm[:, g * N_SSD:(g + 1) * N_SSD].astype(BF16)
            cm_g = cm[:, g * N_SSD:(g + 1) * N_SSD].astype(BF16)
            cb = _dot_nt(cm_g, bm_g)
            state = st_scr[s, g]
            off_parts.append(_dot_nt(cm_g, state.astype(BF16)))
            decay = jnp.exp(jnp.sum(sel_ref[g] * cum[c - 1:c, :], axis=1, keepdims=True))
            st_scr[s, g] = decay * state + _dot_tn(xw[:, g * GRP_W:(g + 1) * GRP_W], bm_g)
            for jj in range(R_SSD // 2):
                j = g * (R_SSD // 2) + jj
                w = []
                for hd in (2 * j, 2 * j + 1):
                    seg = cum[:, hd:hd + 1] - cum_t[hd:hd + 1, :]
                    w.append((jnp.exp(jnp.where(causal, seg, -jnp.inf)) * cb).astype(BF16))
                cols = slice(j * LANE, (j + 1) * LANE)
                y_parts.append(_dot(w[0], x_l[:, cols]) + _dot(w[1], x_r[:, cols]))
        y = jnp.concatenate(y_parts, axis=1) + jnp.exp(cum_e) * jnp.concatenate(off_parts, axis=1)
        y = (y + dsk_ref[...] * xs) * jax.nn.silu(z_ref[rows, :])
        normed = []
        for g in range(G_SSD):
            yg = y[:, g * GRP_W:(g + 1) * GRP_W]
            normed.append(yg * lax.rsqrt(jnp.mean(yg * yg, axis=-1, keepdims=True) + EPS))
        y_ref[rows, :] = (jnp.concatenate(normed, axis=1) * ng_ref[...]).astype(y_ref.dtype)

    @pl.when(ci == nc - 1)
    def _():
        for l in range(n_prev):
            nconv_ref[l] = prev_conv[l][...]
            nssm_ref[l] = prev_ssm[l][...]
        for s in range(nb):
            for g in range(G_SSD):
                for r in range(R_SSD):
                    new_ssm[s, g * R_SSD + r] = st_scr[s, g, r * P_SSD:(r + 1) * P_SSD, :]


def _ssd(xbc, z, dt, lp, states, layer, prev, batch, length):
    c = math.gcd(length, CHUNK)
    nc = length // c
    nb = 1 if nc > 1 else SAMPLE_SEQS_PER_STEP
    assert batch % nb == 0 and c % SUBLANE == 0
    has_state = states is not None
    t = batch * length
    rows = nb * c
    tri = (jnp.arange(c)[:, None] >= jnp.arange(c)[None, :]).astype(BF16)
    eye = jnp.eye(LANE, dtype=BF16)
    head_of_ch = jnp.arange(SSD_W) // P_SSD
    expand = (jnp.arange(LANE)[:, None] == head_of_ch[None, :]).astype(BF16)
    sel = (head_of_ch[:, None] == jnp.arange(LANE)[None, :]).astype(F32).reshape(G_SSD, GRP_W, LANE)

    seq = lambda shape: pl.BlockSpec((nb,) + shape, lambda b, ci: (b,) + (0,) * len(shape))
    rowb = lambda w: pl.BlockSpec((rows, w), lambda b, ci: (b * nc + ci, 0))
    in_specs = [rowb(CONV_DIM), rowb(SSD_W), rowb(LANE),
                _full((CONV_K, CONV_DIM)), _full((1, CONV_DIM)), _full((1, LANE)), _full((1, LANE)),
                _full((1, SSD_W)), _full((1, SSD_W)),
                _full((c, c)), _full((LANE, LANE)), _full((LANE, SSD_W)), _full((G_SSD, GRP_W, LANE))]
    args = [xbc, z, dt, lp["conv_w"], lp["conv_b"], lp["dt_bias"], lp["a"], lp["d_skip"], lp["ssd_norm_g"],
            tri, eye, expand, sel]
    conv_shape, ssm_shape = (CONV_K - 1, CONV_DIM), (H_SSD, P_SSD, N_SSD)
    if has_state:
        stacked = lambda shape: pl.BlockSpec((None, nb) + shape, lambda b, ci: (layer, b) + (0,) * len(shape))
        in_specs += [stacked(conv_shape), stacked(ssm_shape)]
        args += list(states)
    in_specs += [seq(conv_shape)] * len(prev) + [seq(ssm_shape)] * len(prev)
    args += [p[0] for p in prev] + [p[1] for p in prev]
    block_bytes = (_nbytes((rows, CONV_DIM + 2 * SSD_W + LANE), F32) + _nbytes((nb, c + SUBLANE, CONV_DIM), F32)
                   + (3 + 4 * len(prev)) * _nbytes((nb, SSD_W, LANE), F32) + _nbytes((LANE, SSD_W), F32)
                   + 12 * _nbytes((c, SSD_W), F32))
    nconv_spec, nconv_shape = _state_out(prev, batch, nb, conv_shape)
    nssm_spec, nssm_shape = _state_out(prev, batch, nb, ssm_shape)
    return pl.pallas_call(
        functools.partial(_ssd_body, c, nb, nc, has_state, len(prev)),
        grid=(batch // nb, nc), in_specs=in_specs,
        out_specs=[rowb(SSD_W), nconv_spec, nssm_spec],
        out_shape=[jax.ShapeDtypeStruct((t, SSD_W), BF16), nconv_shape, nssm_shape],
        scratch_shapes=[pltpu.VMEM((nb, c + SUBLANE, CONV_DIM), F32),
                        pltpu.VMEM((nb, G_SSD, GRP_W, N_SSD), F32)],
        compiler_params=_params(("arbitrary", "arbitrary"), block_bytes), name="ssd",
    )(*args)


def _first_argmax(vals):
    best = vals[0]
    for v in vals[1:]:
        best = jnp.maximum(best, v)
    idx = jnp.full(best.shape, len(vals) - 1, jnp.int32)
    for i in range(len(vals) - 2, -1, -1):
        idx = jnp.where(vals[i] == best, i, idx)
    return best, idx


def _merge_body(bounds, *refs):
    n_groups = len(bounds)
    acts, refs = refs[:4 * n_groups], refs[4 * n_groups:]
    weights, refs = refs[:7], refs[7:]
    x2_refs, shared = refs[:n_groups], refs[n_groups:]
    i = pl.program_id(0)
    for gi, (first, count) in enumerate(bounds):
        @pl.when((i >= first) & (i < first + count))
        def _():
            _merge_tile(*acts[4 * gi:4 * gi + 4], *weights, x2_refs[gi], *shared)


def _merge_tile(ret_ref, ssd_ref, gates_ref, x_ref, wr_ref, ws_ref, wo_ref, g2_ref, rhi_ref, rlo_ref, low_ref,
                x2_ref, hw_ref, route_ref, cnt_ref):
    ret_y = _dot(ret_ref[...], wr_ref[...])
    ssd_y = _dot(ssd_ref[...], ws_ref[...])
    m = (jax.nn.sigmoid(gates_ref[:, :D_MODEL]) * ret_y + jax.nn.sigmoid(gates_ref[:, D_MODEL:]) * ssd_y)
    x2 = x_ref[...] + _dot(m.astype(BF16), wo_ref[...])
    x2_ref[...] = x2
    h2 = _rmsnorm(x2, g2_ref[...])
    bits = pltpu.bitcast(h2.astype(BF16).astype(F32), jnp.uint32)
    half = D_MODEL // 2
    word = lax.bitwise_or(lax.bitwise_and(bits[:, half:], jnp.uint32(HIGH_HALF)),
                          lax.shift_right_logical(bits[:, :half], jnp.uint32(16)))
    n_rows = h2.shape[0]
    for k in range(PACKED_ROWS):
        hw_ref[pl.ds(k, n_rows, stride=SUBLANE), :] = word[:, k * LANE:(k + 1) * LANE]

    h_hi = h2.astype(BF16)
    h_lo = (h2 - h_hi.astype(F32)).astype(BF16)
    logits = _dot(h_hi, rhi_ref[...]) + (_dot(h_hi, rlo_ref[...]) + _dot(h_lo, rhi_ref[...]))
    colv = lambda i: logits[:, i:i + 1]
    glog = [colv(i) for i in range(N_EGROUPS)]
    gmax, gidx = _first_argmax(glog)
    den = glog[0] * 0.0
    for v in glog:
        den = den + jnp.exp(v - gmax)
    gw = 1.0 / den
    sel = []
    for j in range(EXP_PER_GROUP):
        v = colv(N_EGROUPS + (N_EGROUPS - 1) * EXP_PER_GROUP + j)
        for g in range(N_EGROUPS - 2, -1, -1):
            v = jnp.where(gidx == g, colv(N_EGROUPS + g * EXP_PER_GROUP + j), v)
        sel.append(v)
    v1, i1 = _first_argmax(sel)
    v2, i2 = _first_argmax([jnp.where(i1 == j, -jnp.inf, sel[j]) for j in range(EXP_PER_GROUP)])
    e = jnp.exp(v2 - v1)
    w1 = gw / (1.0 + e)
    w2 = gw * e / (1.0 + e)
    first_low = i1 < i2
    lo = jnp.where(first_low, i1, i2)
    hi = jnp.where(first_low, i2, i1)
    pair = jnp.where(lo == 0, hi - 1, jnp.where(lo == 1, hi + 1, N_PAIRS - 1))
    cls = gidx * N_PAIRS + pair
    w_lo = jnp.where(first_low, w1, w2)
    w_hi = jnp.where(first_low, w2, w1)

    rows = logits.shape[0]
    lane = lax.broadcasted_iota(jnp.int32, (rows, META_W), 1)
    onehot = (lane == cls).astype(F32)
    before = _dot(low_ref[...], onehot.astype(BF16))
    rank = jnp.sum(onehot * before, axis=1, keepdims=True)
    meta = jnp.where(lane == 0, w_lo, jnp.where(lane == 1, w_hi, jnp.where(
        lane == 2, cls.astype(F32), jnp.where(lane == 3, rank, 0.0))))
    meta_bits = pltpu.bitcast(meta, jnp.uint32)
    for k in range(PACKED_ROWS, TOKEN_TILE_ROWS):
        hw_ref[pl.ds(k, n_rows, stride=SUBLANE), :] = meta_bits
    route_ref[...] = meta
    cnt_ref[0] = jnp.sum(onehot, axis=0, keepdims=True)


def _merge(group_acts, lp, big, layer):
    counts = [a[3].shape[0] // ROW_TILE for a in group_acts]
    bounds = tuple((sum(counts[:gi]), counts[gi]) for gi in range(len(counts)))
    n_tiles = sum(counts)
    total_rows = n_tiles * ROW_TILE
    low = (jnp.arange(ROW_TILE)[:, None] > jnp.arange(ROW_TILE)[None, :]).astype(BF16)
    stacked = [big["w_ret_out"], big["w_ssd_out"], big["w_o"]]
    ws = [lp["ln2_g"], lp["router_hi"], lp["router_lo"], low]
    block_bytes = (sum(_nbytes(w.shape[1:], w.dtype) for w in stacked) + sum(_nbytes(w.shape, w.dtype) for w in ws)
                   + len(counts) * (_nbytes((ROW_TILE, RET_W + SSD_W), BF16) + _nbytes((ROW_TILE, 4 * D_MODEL), F32))
                   + _nbytes((ROW_TILE, 3 * D_MODEL), F32))

    def group_row(width, first, count):
        return pl.BlockSpec((ROW_TILE, width), lambda i: (jnp.clip(i - first, 0, count - 1), 0))

    in_specs, args, x2_specs, x2_shapes = [], [], [], []
    for (first, count), acts in zip(bounds, group_acts, strict=True):
        in_specs += [group_row(w, first, count) for w in (RET_W, SSD_W, 2 * D_MODEL, D_MODEL)]
        args += list(acts)
        x2_specs.append(group_row(D_MODEL, first, count))
        x2_shapes.append(jax.ShapeDtypeStruct((count * ROW_TILE, D_MODEL), F32))
    in_specs += [_layer_block(w, layer) for w in stacked] + [_full(w.shape) for w in ws]
    args += stacked + ws
    outs = pl.pallas_call(
        functools.partial(_merge_body, bounds), grid=(n_tiles,), in_specs=in_specs,
        out_specs=x2_specs + [_token_spec(0), pl.BlockSpec((ROW_TILE, META_W), lambda i: (i, 0)),
                              pl.BlockSpec((1, 1, META_W), lambda i: (i, 0, 0))],
        out_shape=x2_shapes + [jax.ShapeDtypeStruct((total_rows * TOKEN_TILE_ROWS, LANE), jnp.uint32),
                               jax.ShapeDtypeStruct((total_rows, META_W), F32),
                               jax.ShapeDtypeStruct((n_tiles, 1, META_W), F32)],
        compiler_params=_params(("arbitrary",), block_bytes), name="merge_router",
    )(*args)
    return outs[:len(counts)], outs[len(counts):]


PLAN_ROWS = LANE
TAB_ROWS = SUBLANE
TAB_LO, TAB_HI, TAB_USED = 0, 1, 2
BYTE = 256.0
MOE_RING = 3


def _row_of(col_vals):
    pick = (lax.broadcasted_iota(jnp.int32, (TAB_ROWS, LANE), 0)
            == lax.broadcasted_iota(jnp.int32, (TAB_ROWS, LANE), 1)).astype(BF16)
    hi = jnp.floor(col_vals * (1.0 / BYTE))
    lo = col_vals - hi * BYTE
    return BYTE * _dot_nt(pick, hi.astype(BF16)) + _dot_nt(pick, lo.astype(BF16))


def _plan_body(tile, n_row_tiles, route_ref, cnt_ref, low_ref, upp_ref, pos_ref, tab_ref, offs_scr):
    cnt = cnt_ref[...]
    tot = jnp.sum(cnt, axis=0, keepdims=True)
    cls_tiles = jnp.floor((tot + (tile - 1.0)) * (1.0 / tile))
    first_tile = _dot(jnp.broadcast_to(cls_tiles, (SUBLANE, LANE)).astype(BF16), upp_ref[...])[0:1, :]
    base = first_tile * tile
    ends = base + cls_tiles * tile
    offs_scr[...] = base + _dot(low_ref[...], cnt.astype(BF16))

    def per_row_tile(j, carry):
        meta = route_ref[pl.ds(j * ROW_TILE, ROW_TILE), :]
        lane = lax.broadcasted_iota(jnp.int32, (ROW_TILE, META_W), 1)
        onehot = lane.astype(F32) == meta[:, 2:3]
        pos = jnp.sum(jnp.where(onehot, offs_scr[pl.ds(j, 1), :], 0.0), axis=1, keepdims=True) + meta[:, 3:4]
        rows = _row_of(jnp.where(lane == 0, pos, 0.0))
        pos_ref[j] = rows[0:1, :].astype(jnp.int32)
        return carry

    lax.fori_loop(0, n_row_tiles, per_row_tile, 0)

    lane = lax.broadcasted_iota(jnp.int32, (PLAN_ROWS, LANE), 1)
    tile_idx = lax.broadcasted_iota(jnp.int32, (PLAN_ROWS, LANE), 0).astype(F32)[:, 0:1]
    n_used = jnp.sum(cls_tiles, axis=1, keepdims=True)
    start = jnp.minimum(tile_idx, n_used - 1.0) * tile
    tcls = jnp.sum(jnp.where((lane < N_CLASSES) & (ends <= start), 1.0, 0.0), axis=1, keepdims=True)
    tcls = jnp.minimum(tcls, N_CLASSES - 1.0)
    group = sum(jnp.where(tcls >= k * N_PAIRS, 1.0, 0.0) for k in range(1, N_EGROUPS))
    pair = tcls - group * N_PAIRS
    p_lo = jnp.where(pair >= 3, 1.0, 0.0) + jnp.where(pair >= 5, 1.0, 0.0)
    p_hi = jnp.where(pair == 0, 1.0, jnp.where((pair == 1) | (pair == 3), 2.0, 3.0))
    table = jnp.where(lane == TAB_LO, group * EXP_PER_GROUP + p_lo,
                      jnp.where(lane == TAB_HI, group * EXP_PER_GROUP + p_hi,
                                jnp.where(lane == TAB_USED, n_used, 0.0)))
    tab_ref[...] = _row_of(table).astype(jnp.int32)


def _plan(route, counts, tile):
    t = route.shape[0]
    nrt = t // ROW_TILE
    assert nrt <= PLAN_ROWS and t // tile + N_CLASSES <= PLAN_ROWS
    cnt = jnp.pad(counts[:, 0, :], ((0, PLAN_ROWS - nrt), (0, 0)))
    idx = jnp.arange(PLAN_ROWS)
    low = (idx[:, None] > idx[None, :]).astype(BF16)
    upp = (idx[:, None] < idx[None, :]).astype(BF16)
    pos, tab = pl.pallas_call(
        functools.partial(_plan_body, tile, nrt),
        out_shape=[jax.ShapeDtypeStruct((nrt, 1, ROW_TILE), jnp.int32),
                   jax.ShapeDtypeStruct((TAB_ROWS, PLAN_ROWS), jnp.int32)],
        scratch_shapes=[pltpu.VMEM((PLAN_ROWS, LANE), F32)],
        compiler_params=pltpu.CompilerParams(vmem_limit_bytes=_vmem_limit(_nbytes(route.shape, F32))),
        name="moe_plan",
    )(route, cnt, low, upp)
    return pos.reshape(t), tab


def _moe_body(tile, n_tok, pos_ref, tab_ref, hw_hbm, spare_hbm, wgl_ref, wul_ref, wdl_ref, wgh_ref, wuh_ref,
              wdh_ref, y_hbm, buf, obuf, tok_smem, gsem, ssem, isem):
    i = pl.program_id(0)
    n_used = tab_ref[TAB_USED, 0]
    ring = buf.shape[0]
    cur = lax.rem(i, ring)
    nxt1 = lax.rem(i + 1, ring)
    nxt2 = lax.rem(i + 2, ring)
    n_points = 16
    per_point = tile // n_points
    tr = TOKEN_TILE_ROWS

    def token_tile(ref, token):
        start = token * tr if isinstance(token, int) else pl.multiple_of(token * tr, tr)
        return ref.at[pl.ds(start, tr)]

    def dma_thread(r):
        return r % 2 if isinstance(r, int) else 0

    def gather_start(block, r, sl):
        src = jnp.minimum(tok_smem[block * tile + r], n_tok - 1)
        pltpu.make_async_copy(token_tile(hw_hbm, src), token_tile(buf.at[sl], r), gsem.at[sl]).start(dma_thread(r))

    def scatter_start(block, r, sl):
        dst = tok_smem[block * tile + r]
        pltpu.make_async_copy(token_tile(obuf.at[sl], r), token_tile(y_hbm, dst), ssem.at[sl]).start(dma_thread(r))

    def gather_wait(sl):
        pltpu.make_async_copy(hw_hbm.at[pl.ds(0, tile * tr)], buf.at[sl], gsem.at[sl]).wait()

    def scatter_wait(sl):
        pltpu.make_async_copy(obuf.at[sl], y_hbm.at[pl.ds(0, tile * tr)], ssem.at[sl]).wait()

    def for_rows(n, fn):
        lax.fori_loop(0, n, lambda r, carry: (fn(r), carry)[1], 0, unroll=8)

    def block_of(t):
        return jnp.minimum(t, n_used - 1) + 1

    @pl.when(i < n_used)
    def _():
        @pl.when(i == 0)
        def _():
            fill = pltpu.make_async_copy(spare_hbm, tok_smem, isem)
            fill.start()
            fill.wait()

            def place(t):
                tok_smem[tile + pos_ref[t]] = t
            for_rows(n_tok, place)
            obuf[ring - 1] = jnp.zeros(obuf.shape[1:], F32)
            for_rows(tile, lambda r: gather_start(block_of(0), r, 0))
            for_rows(tile, lambda r: gather_start(block_of(1), r, 1))

        gather_wait(cur)
        issued = [0]

        def issue_point():
            for r in range(issued[0] * per_point, (issued[0] + 1) * per_point):
                gather_start(block_of(i + 2), r, nxt2)
                scatter_start(i, r, nxt2)
            issued[0] += 1

        def tile_row(k):
            return buf[cur, pl.ds(k, tile, stride=tr), :]

        words = [tile_row(k) for k in range(PACKED_ROWS)]
        low = [pltpu.bitcast(lax.shift_left(w, jnp.uint32(16)), F32).astype(BF16) for w in words]
        high = [pltpu.bitcast(lax.bitwise_and(w, jnp.uint32(HIGH_HALF)), F32).astype(BF16) for w in words]
        xb = jnp.concatenate(low + high, axis=1)
        meta = pltpu.bitcast(tile_row(PACKED_ROWS), F32)
        w_lo, w_hi = meta[:, 0:1], meta[:, 1:2]
        half = D_FF // 2
        acts = []
        for wg_ref, wu_ref in ((wgl_ref, wul_ref), (wgh_ref, wuh_ref)):
            parts = []
            for n in range(2):
                gate = _dot(xb, wg_ref[:, n * half:(n + 1) * half])
                issue_point()
                up = _dot(xb, wu_ref[:, n * half:(n + 1) * half])
                issue_point()
                parts.append((jax.nn.silu(gate) * up).astype(BF16))
            acts.append(jnp.concatenate(parts, axis=1))
        quarter = D_MODEL // 4
        for n in range(4):
            cols = slice(n * quarter, (n + 1) * quarter)
            d_lo = _dot(acts[0], wdl_ref[:, cols])
            issue_point()
            d_hi = _dot(acts[1], wdh_ref[:, cols])
            issue_point()
            out = d_lo * w_lo + d_hi * w_hi
            for kk in range(quarter // LANE):
                obuf[cur, pl.ds(n * (quarter // LANE) + kk, tile, stride=tr), :] = out[:, kk * LANE:(kk + 1) * LANE]
        assert issued[0] == n_points

        @pl.when(i >= 1)
        def _():
            scatter_wait(nxt1)

        @pl.when(i == n_used - 1)
        def _():
            scatter_wait(nxt2)
            for_rows(tile, lambda r: scatter_start(i + 1, r, cur))
            scatter_wait(cur)
            gather_wait(nxt1)
            gather_wait(nxt2)


def _moe(hw, pos, tab, big, layer, tile):
    t = hw.shape[0] // TOKEN_TILE_ROWS
    n_tiles = t // tile + N_CLASSES
    wspec = lambda shape, row: pl.BlockSpec((None, None) + shape, lambda i, pos, tab: (layer, tab[row, i], 0, 0))
    in_specs = [pl.BlockSpec(memory_space=pl.ANY)] * 2
    spare = t + jnp.arange((n_tiles + 1) * tile, dtype=jnp.int32) % tile
    for row in (TAB_LO, TAB_HI):
        in_specs += [wspec((D_MODEL, D_FF), row), wspec((D_MODEL, D_FF), row), wspec((D_FF, D_MODEL), row)]
    block_bytes = 6 * _nbytes((D_MODEL, D_FF), BF16) + 6 * _nbytes((tile, D_MODEL), F32)
    tile_rows = tile * TOKEN_TILE_ROWS
    grid_spec = pltpu.PrefetchScalarGridSpec(
        num_scalar_prefetch=2, grid=(n_tiles,), in_specs=in_specs,
        out_specs=pl.BlockSpec(memory_space=pl.ANY),
        scratch_shapes=[pltpu.VMEM((MOE_RING, tile_rows, LANE), jnp.uint32),
                        pltpu.VMEM((MOE_RING, tile_rows, LANE), F32),
                        pltpu.SMEM(((n_tiles + 1) * tile,), jnp.int32),
                        pltpu.SemaphoreType.DMA((MOE_RING,)), pltpu.SemaphoreType.DMA((MOE_RING,)),
                        pltpu.SemaphoreType.DMA(())])
    return pl.pallas_call(
        functools.partial(_moe_body, tile, t), grid_spec=grid_spec,
        out_shape=jax.ShapeDtypeStruct(((t + tile) * TOKEN_TILE_ROWS, LANE), F32),
        compiler_params=_params(("arbitrary",), block_bytes), name="moe",
    )(pos, tab, hw, spare, big["w_gate"], big["w_up"], big["w_down"], big["w_gate"], big["w_up"], big["w_down"])


def _final_body(x_ref, y_ref, g_ref, o_ref):
    o_ref[...] = _rmsnorm(x_ref[...] + _token_rows(y_ref, ROW_TILE), g_ref[...])


def _final_norm(x, y, y_offset, g):
    t = x.shape[0]
    row = pl.BlockSpec((ROW_TILE, D_MODEL), lambda i: (i, 0))
    return pl.pallas_call(
        _final_body, grid=(t // ROW_TILE,), in_specs=[row, _token_spec(y_offset), _full((1, D_MODEL))],
        out_specs=row, out_shape=jax.ShapeDtypeStruct((t, D_MODEL), F32),
        compiler_params=_params(("arbitrary",), 3 * _nbytes((ROW_TILE, D_MODEL), F32)), name="final_norm",
    )(x, y, g)


def _layer_params(i, ln1_g, ret_gn_g, conv_w, conv_b, dt_bias, a_log, d_skip, ssd_norm_g, ln2_g, w_rg, w_re):
    pad_heads = lambda v: jnp.pad(v.astype(F32), (0, LANE - H_SSD)).reshape(1, LANE)
    router = jnp.pad(jnp.concatenate([w_rg[i], w_re[i]], axis=1),
                     ((0, 0), (0, META_W - N_EGROUPS - N_EXPERTS)))
    router_hi = router.astype(BF16)
    return {
        "ln1_g": ln1_g[i].reshape(1, D_MODEL),
        "ret_gn_g": ret_gn_g[i].reshape(1, RET_W),
        "conv_w": conv_w[i], "conv_b": conv_b[i].reshape(1, CONV_DIM),
        "dt_bias": pad_heads(dt_bias[i]), "a": pad_heads(-jnp.exp(a_log[i].astype(F32))),
        "d_skip": jnp.repeat(d_skip[i].astype(F32), P_SSD).reshape(1, SSD_W),
        "ssd_norm_g": ssd_norm_g[i].reshape(1, SSD_W),
        "ln2_g": ln2_g[i].reshape(1, D_MODEL),
        "router_hi": router_hi, "router_lo": (router - router_hi.astype(F32)).astype(BF16),
    }


def _stacked_params(w_in, w_ret_out, w_ssd_out, w_o, w_gate, w_up, w_down):
    offs = [0]
    for s in IN_SIZES:
        offs.append(offs[-1] + s)
    return {
        "w_in": [w_in.astype(BF16),
                 jnp.pad(w_in[:, :, offs[6]:offs[7]], ((0, 0), (0, 0), (0, LANE - H_SSD))).astype(BF16),
                 w_in[:, :, offs[7]:offs[9]].astype(BF16)],
        "w_ret_out": w_ret_out.astype(BF16), "w_ssd_out": w_ssd_out.astype(BF16), "w_o": w_o.astype(BF16),
        "w_gate": w_gate.astype(BF16), "w_up": w_up.astype(BF16), "w_down": w_down.astype(BF16),
    }


class _Group:
    def __init__(self, x, start, states, row_offset):
        self.batch, self.length, _ = x.shape
        self.x = x.reshape(self.batch * self.length, D_MODEL)
        self.start, self.states, self.row_offset = start, states, row_offset
        self.rets, self.ssms, self.convs = [], [], []


def kernel(x_prompt, x_sample, state_ret, state_ssm, state_conv, ln1_g, w_in, ret_gn_g, w_ret_out, conv_w, conv_b, dt_bias, a_log, d_skip, ssd_norm_g, w_ssd_out, w_o, ln2_g, w_router_group, w_router_expert, w_e_gate, w_e_up, w_e_down, lnf_g):
    depth = w_in.shape[0]
    past_len = 16384.0
    layers = [_layer_params(i, ln1_g, ret_gn_g, conv_w, conv_b, dt_bias, a_log, d_skip, ssd_norm_g, ln2_g,
                            w_router_group, w_router_expert) for i in range(depth)]
    big = _stacked_params(w_in, w_ret_out, w_ssd_out, w_o, w_e_gate, w_e_up, w_e_down)
    n_prompt = x_prompt.shape[0] * x_prompt.shape[1]
    groups = [_Group(x_prompt, 0.0, None, 0),
              _Group(x_sample, past_len, (state_ret, state_ssm, state_conv), n_prompt)]
    y = None
    for i, lp in enumerate(layers):
        group_acts = []
        last = i == depth - 1
        for grp in groups:
            outs = _inproj(grp.x, y, grp.row_offset, lp["ln1_g"], big["w_in"], i)
            if y is not None:
                grp.x, outs = outs[0], outs[1:]
            qkvr, z, xbc, dt, gates = outs
            ret_state = None if grp.states is None else grp.states[0]
            ssd_state = None if grp.states is None else (grp.states[2], grp.states[1])
            ret, new_ret = _retention(qkvr, grp.start, lp["ret_gn_g"], ret_state, i,
                                      grp.rets if last else (), grp.batch, grp.length)
            ssd, new_conv, new_ssm = _ssd(xbc, z, dt, lp, ssd_state, i,
                                          list(zip(grp.convs, grp.ssms)) if last else (), grp.batch, grp.length)
            grp.rets.append(new_ret)
            grp.ssms.append(new_ssm)
            grp.convs.append(new_conv)
            group_acts.append((ret, ssd, gates, grp.x))
        new_x, (hw, route, counts) = _merge(group_acts, lp, big, i)
        for grp, x2 in zip(groups, new_x, strict=True):
            grp.x = x2
        pos, tab = _plan(route, counts, ROW_TILE)
        y = _moe(hw, pos, tab, big, i, ROW_TILE)
    outs = []
    for grp in groups:
        out = _final_norm(grp.x, y, grp.row_offset, lnf_g.reshape(1, D_MODEL))
        outs.append(out.reshape(grp.batch, grp.length, D_MODEL))
    states = []
    for grp in groups:
        states += [grp.rets[-1], grp.ssms[-1], grp.convs[-1]]
    return tuple(outs + states)
```

```python
import functools
import math

import jax
import jax.numpy as jnp
from jax import lax
from jax.experimental import pallas as pl
from jax.experimental.pallas import tpu as pltpu

F32 = jnp.float32
BF16 = jnp.bfloat16

D_MODEL = 1024
H_RET, DK_RET, DV_RET = 4, 128, 128
RET_W = H_RET * DV_RET
ROPE_BASE = 10000.0
H_SSD, P_SSD, G_SSD, N_SSD = 16, 64, 2, 64
R_SSD = H_SSD // G_SSD
SSD_W = H_SSD * P_SSD
GRP_W = SSD_W // G_SSD
CONV_K = 4
CONV_DIM = SSD_W + 2 * G_SSD * N_SSD
CHUNK = 128
N_EGROUPS, EXP_PER_GROUP = 4, 4
N_EXPERTS = N_EGROUPS * EXP_PER_GROUP
N_PAIRS = EXP_PER_GROUP * (EXP_PER_GROUP - 1) // 2
N_CLASSES = N_EGROUPS * N_PAIRS
D_FF = 512
EPS = 1e-6
IN_SIZES = [RET_W, RET_W, RET_W, RET_W, SSD_W, CONV_DIM, H_SSD, D_MODEL, D_MODEL]

LANE = 128
SUBLANE = 8
MIB = 1024 * 1024

ROW_TILE = 256
META_W = LANE
TOKEN_TILE_ROWS = D_MODEL // LANE
assert TOKEN_TILE_ROWS == SUBLANE
PACKED_ROWS = D_MODEL // (2 * LANE)
HIGH_HALF = 0xFFFF0000
SAMPLE_SEQS_PER_STEP = 8
PROMPT_SEQS_PER_STEP = 4


def _vmem_limit(block_bytes):
    return int(min(2 * block_bytes + 16 * MIB, 56 * MIB))


def _params(semantics, block_bytes):
    return pltpu.CompilerParams(dimension_semantics=semantics,
                                vmem_limit_bytes=_vmem_limit(block_bytes))


def _nbytes(shape, dtype):
    return math.prod(shape) * jnp.dtype(dtype).itemsize


def _full(shape):
    return pl.BlockSpec(shape, lambda *_: (0,) * len(shape))


def _dot(a, b):
    return jnp.dot(a, b, preferred_element_type=F32)


def _dot_nt(a, b):
    return lax.dot_general(a, b, (((1,), (1,)), ((), ())), preferred_element_type=F32)


def _dot_tn(a, b):
    return lax.dot_general(a, b, (((0,), (0,)), ((), ())), preferred_element_type=F32)


def _split3(a):
    hi = a.astype(BF16)
    r1 = a - hi.astype(F32)
    mid = r1.astype(BF16)
    lo = (r1 - mid.astype(F32)).astype(BF16)
    return hi, mid, lo


def _rmsnorm(x, g):
    r = lax.rsqrt(jnp.mean(x * x, axis=-1, keepdims=True) + EPS)
    return x * r * g


def _token_rows(ref, rows):
    return jnp.concatenate([ref[pl.ds(k, rows, stride=SUBLANE), :] for k in range(TOKEN_TILE_ROWS)], axis=1)


def _token_spec(row_offset, rows=ROW_TILE):
    assert row_offset % rows == 0
    first = row_offset // rows
    return pl.BlockSpec((rows * TOKEN_TILE_ROWS, LANE), lambda i: (i + first, 0))


def _inproj_body(has_add, *refs):
    n_in = 6 if has_add else 5
    ins, outs = refs[:n_in], refs[n_in:]
    if has_add:
        x_ref, y_ref, g_ref = ins[:3]
        x = x_ref[...] + _token_rows(y_ref, ROW_TILE)
        outs[0][...] = x
        outs = outs[1:]
    else:
        x_ref, g_ref = ins[:2]
        x = x_ref[...]
    w_main, w_dt, w_gates = ins[-3:]
    h = _rmsnorm(x, g_ref[...]).astype(BF16)
    col = 0
    for o_ref in outs[:3]:
        width = o_ref.shape[1]
        o_ref[...] = _dot(h, w_main[:, col:col + width])
        col += width
    outs[3][...] = _dot(h, w_dt[...])
    outs[4][...] = _dot(h, w_gates[...])


def _layer_block(arr, layer):
    shape = arr.shape[1:]
    return pl.BlockSpec((None,) + shape, lambda *_: (layer,) + (0,) * len(shape))


def _inproj(x, y, y_offset, g, ws, layer):
    t = x.shape[0]
    has_add = y is not None
    widths = [4 * RET_W, SSD_W, CONV_DIM, LANE, 2 * D_MODEL]
    assert sum(widths[:3]) == sum(IN_SIZES[:6])
    row = lambda w: pl.BlockSpec((ROW_TILE, w), lambda i: (i, 0))
    in_specs = [row(D_MODEL)] + ([_token_spec(y_offset)] if has_add else []) + [_full((1, D_MODEL))]
    in_specs += [_layer_block(w, layer) for w in ws]
    out_shape = [jax.ShapeDtypeStruct((t, w), F32) for w in widths]
    out_specs = [row(w) for w in widths]
    if has_add:
        out_shape = [jax.ShapeDtypeStruct((t, D_MODEL), F32)] + out_shape
        out_specs = [row(D_MODEL)] + out_specs
    block_bytes = (sum(_nbytes(w.shape[1:], BF16) for w in ws)
                   + _nbytes((ROW_TILE, sum(widths) + 3 * D_MODEL), F32))
    args = ([x, y] if has_add else [x]) + [g] + list(ws)
    return pl.pallas_call(
        functools.partial(_inproj_body, has_add),
        grid=(t // ROW_TILE,), in_specs=in_specs, out_specs=out_specs, out_shape=out_shape,
        compiler_params=_params(("arbitrary",), block_bytes), name="inproj",
    )(*args)


def _retention_body(c, nb, nc, has_state, n_prev, *refs):
    refs = list(refs)
    qkvr_ref, cos_ref, sin_ref, dmat_ref, qdec_ref, kdec_ref, cdec_ref, gn_ref = refs[:8]
    refs = refs[8:]
    if has_state:
        s0_ref = refs.pop(0)
    prev_refs = [refs.pop(0) for _ in range(n_prev)]
    o_ref, ns_ref, st_scr = refs
    ci = pl.program_id(1)

    @pl.when(ci == 0)
    def _():
        if has_state:
            st_scr[...] = s0_ref[...]
        else:
            st_scr[...] = jnp.zeros(st_scr.shape, F32)

    cosv, sinv = cos_ref[...], sin_ref[...]
    for s in range(nb):
        for h in range(H_RET):
            col = lambda j: slice(j * RET_W + h * DK_RET, j * RET_W + (h + 1) * DK_RET)
            q, k = qkvr_ref[s, :, col(0)], qkvr_ref[s, :, col(1)]
            v, rg = qkvr_ref[s, :, col(2)], qkvr_ref[s, :, col(3)]
            qr = q * cosv + pltpu.roll(q, DK_RET // 2, axis=1) * sinv
            kr = (k * cosv + pltpu.roll(k, DK_RET // 2, axis=1) * sinv) * (DK_RET ** -0.5)
            qb, kb, vb = qr.astype(BF16), kr.astype(BF16), v.astype(BF16)
            state = st_scr[s, h]
            sc = _dot_nt(qb, kb) * dmat_ref[h]
            o = _dot(sc.astype(BF16), vb) + _dot(qb, state.astype(BF16)) * qdec_ref[h]
            st_scr[s, h] = cdec_ref[h] * state + _dot_tn((kr * kdec_ref[h]).astype(BF16), vb)
            mu = jnp.mean(o, axis=-1, keepdims=True)
            d = o - mu
            var = jnp.mean(d * d, axis=-1, keepdims=True)
            on = d * lax.rsqrt(var + EPS) * gn_ref[:, h * DV_RET:(h + 1) * DV_RET]
            o_ref[s, :, h * DV_RET:(h + 1) * DV_RET] = (jax.nn.silu(rg) * on).astype(o_ref.dtype)

    @pl.when(ci == nc - 1)
    def _():
        if n_prev:
            for l, prev_ref in enumerate(prev_refs):
                ns_ref[l] = prev_ref[...]
            ns_ref[n_prev] = st_scr[...]
        else:
            ns_ref[...] = st_scr[...]


def _state_out(prev, batch, nb, shape):
    zeros = (0,) * len(shape)
    if prev:
        depth = len(prev) + 1
        return (pl.BlockSpec((depth, nb) + shape, lambda b, ci: (0, b) + zeros),
                jax.ShapeDtypeStruct((depth, batch) + shape, F32))
    return pl.BlockSpec((nb,) + shape, lambda b, ci: (b,) + zeros), jax.ShapeDtypeStruct((batch,) + shape, F32)


def _retention(qkvr, start, gn_g, states, layer, prev, batch, length):
    c = math.gcd(length, CHUNK)
    nc = length // c
    nb = PROMPT_SEQS_PER_STEP if nc > 1 else SAMPLE_SEQS_PER_STEP
    assert batch % nb == 0
    has_state = states is not None
    t = batch * length
    half = DK_RET // 2
    pos = start + jnp.arange(length, dtype=F32)
    inv = ROPE_BASE ** (-jnp.arange(half, dtype=F32) / half)
    ang = pos[:, None] * inv[None, :]
    cosv = jnp.concatenate([jnp.cos(ang), jnp.cos(ang)], axis=1)
    sinv = jnp.concatenate([-jnp.sin(ang), jnp.sin(ang)], axis=1)
    log_g = jnp.log1p(-jnp.exp2(-5.0 - jnp.arange(H_RET, dtype=F32)))
    idx = jnp.arange(c, dtype=F32)
    rel = idx[:, None] - idx[None, :]
    causal = rel >= 0
    dmat = jnp.where(causal[None], jnp.exp(jnp.where(causal, rel, 0.0)[None] * log_g[:, None, None]), 0.0)
    qdec = jnp.broadcast_to(jnp.exp((idx[None, :] + 1.0) * log_g[:, None])[:, :, None], (H_RET, c, LANE))
    kdec = jnp.broadcast_to(jnp.exp((c - 1.0 - idx[None, :]) * log_g[:, None])[:, :, None], (H_RET, c, LANE))
    cdec = jnp.exp(c * log_g)

    rows = nb * c
    state_spec = pl.BlockSpec((nb, H_RET, DK_RET, DV_RET), lambda b, ci: (b, 0, 0, 0))
    seq_rows = lambda w: pl.BlockSpec((nb, c, w), lambda b, ci: (b, ci, 0))
    in_specs = [
        seq_rows(4 * RET_W),
        pl.BlockSpec((c, LANE), lambda b, ci: (ci, 0)),
        pl.BlockSpec((c, LANE), lambda b, ci: (ci, 0)),
        _full((H_RET, c, c)), _full((H_RET, c, LANE)), _full((H_RET, c, LANE)),
        pl.BlockSpec(memory_space=pltpu.SMEM),
        _full((1, RET_W)),
    ]
    args = [qkvr.reshape(batch, length, 4 * RET_W), cosv, sinv, dmat, qdec, kdec, cdec, gn_g]
    if has_state:
        in_specs.append(pl.BlockSpec((None, nb, H_RET, DK_RET, DV_RET), lambda b, ci: (layer, b, 0, 0, 0)))
        args.append(states)
    in_specs += [state_spec] * len(prev)
    args += list(prev)
    block_bytes = (_nbytes((rows, 4 * RET_W + 2 * LANE + RET_W), F32)
                   + (3 + 2 * len(prev)) * _nbytes((nb, H_RET, DK_RET, DV_RET), F32)
                   + 3 * _nbytes((H_RET, c, LANE), F32))
    ns_spec, ns_shape = _state_out(prev, batch, nb, (H_RET, DK_RET, DV_RET))
    out, new_state = pl.pallas_call(
        functools.partial(_retention_body, c, nb, nc, has_state, len(prev)),
        grid=(batch // nb, nc), in_specs=in_specs,
        out_specs=[seq_rows(RET_W), ns_spec],
        out_shape=[jax.ShapeDtypeStruct((batch, length, RET_W), BF16), ns_shape],
        scratch_shapes=[pltpu.VMEM((nb, H_RET, DK_RET, DV_RET), F32)],
        compiler_params=_params(("arbitrary", "arbitrary"), block_bytes), name="retention",
    )(*args)
    return out.reshape(t, RET_W), new_state


def _ssd_body(c, nb, nc, has_state, n_prev, *refs):
    refs = list(refs)
    (xbc_ref, z_ref, dt_ref, cw_ref, cb_ref, dtb_ref, a_ref, dsk_ref, ng_ref,
     tri_ref, eye_ref, exp_ref, sel_ref) = refs[:13]
    refs = refs[13:]
    if has_state:
        sconv_ref, sssm_ref = refs.pop(0), refs.pop(0)
    prev_conv = [refs.pop(0) for _ in range(n_prev)]
    prev_ssm = [refs.pop(0) for _ in range(n_prev)]
    y_ref, nconv_ref, nssm_ref, ext_scr, st_scr = refs
    new_conv = nconv_ref.at[n_prev] if n_prev else nconv_ref
    new_ssm = nssm_ref.at[n_prev] if n_prev else nssm_ref
    ci = pl.program_id(1)
    pad = SUBLANE
    hist = CONV_K - 1

    @pl.when(ci == 0)
    def _():
        ext_scr[:, 0:pad, :] = jnp.zeros((nb, pad, CONV_DIM), F32)
        if has_state:
            ext_scr[:, pad - hist:pad, :] = sconv_ref[...]
            for s in range(nb):
                for g in range(G_SSD):
                    for r in range(R_SSD):
                        st_scr[s, g, r * P_SSD:(r + 1) * P_SSD, :] = sssm_ref[s, g * R_SSD + r]
        else:
            st_scr[...] = jnp.zeros(st_scr.shape, F32)

    causal = (lax.broadcasted_iota(jnp.int32, (c, c), 0) >= lax.broadcasted_iota(jnp.int32, (c, c), 1))
    left_half = (lax.broadcasted_iota(jnp.int32, (c, SSD_W), 1) % LANE) < P_SSD
    tri, eye, expand = tri_ref[...], eye_ref[...], exp_ref[...]

    for s in range(nb):
        rows = slice(s * c, (s + 1) * c)
        ext_scr[s, pad:pad + c, :] = xbc_ref[rows, :]
        acc = ext_scr[s, pad - hist:pad - hist + c, :] * cw_ref[0:1, :]
        for j in range(1, CONV_K):
            acc = acc + ext_scr[s, pad - hist + j:pad - hist + j + c, :] * cw_ref[j:j + 1, :]
        conv = jax.nn.silu(cb_ref[...] + acc)

        @pl.when(ci == nc - 1)
        def _():
            new_conv[s] = ext_scr[s, pad + c - hist:pad + c, :]

        ext_scr[s, 0:pad, :] = ext_scr[s, c:c + pad, :]

        xs = conv[:, :SSD_W]
        bm = conv[:, SSD_W:SSD_W + G_SSD * N_SSD]
        cm = conv[:, SSD_W + G_SSD * N_SSD:]
        dt = jax.nn.softplus(dt_ref[rows, :] + dtb_ref[...])
        dta = dt * a_ref[...]
        cum = sum(_dot(tri, p) for p in _split3(dta))
        cum_parts = _split3(cum)
        cum_t = sum(_dot_nt(eye, p) for p in cum_parts)
        cum_e = sum(_dot(p, expand) for p in cum_parts)
        dt_e = sum(_dot(p, expand) for p in _split3(dt))
        last_e = cum_e[c - 1:c, :]
        xdt = xs * dt_e
        xw = (xs * (jnp.exp(last_e - cum_e) * dt_e)).astype(BF16)
        x_l = jnp.where(left_half, xdt, 0.0).astype(BF16)
        x_r = jnp.where(left_half, 0.0, xdt).astype(BF16)

        y_parts, off_parts = [], []
        for g in range(G_SSD):
            bm_g = bm[:, g * N_SSD:(g + 1) * N_SSD].astype(BF16)
            cm_g = cm[:, g * N_SSD:(g + 1) * N_SSD].astype(BF16)
            cb = _dot_nt(cm_g, bm_g)
            state = st_scr[s, g]
            off_parts.append(_dot_nt(cm_g, state.astype(BF16)))
            decay = jnp.exp(jnp.sum(sel_ref[g] * cum[c - 1:c, :], axis=1, keepdims=True))
            st_scr[s, g] = decay * state + _dot_tn(xw[:, g * GRP_W:(g + 1) * GRP_W], bm_g)
            for jj in range(R_SSD // 2):
                j = g * (R_SSD // 2) + jj
                w = []
                for hd in (2 * j, 2 * j + 1):
                    seg = cum[:, hd:hd + 1] - cum_t[hd:hd + 1, :]
                    w.append((jnp.exp(jnp.where(causal, seg, -jnp.inf)) * cb).astype(BF16))
                cols = slice(j * LANE, (j + 1) * LANE)
                y_parts.append(_dot(w[0], x_l[:, cols]) + _dot(w[1], x_r[:, cols]))
        y = jnp.concatenate(y_parts, axis=1) + jnp.exp(cum_e) * jnp.concatenate(off_parts, axis=1)
        y = (y + dsk_ref[...] * xs) * jax.nn.silu(z_ref[rows, :])
        normed = []
        for g in range(G_SSD):
            yg = y[:, g * GRP_W:(g + 1) * GRP_W]
            normed.append(yg * lax.rsqrt(jnp.mean(yg * yg, axis=-1, keepdims=True) + EPS))
        y_ref[rows, :] = (jnp.concatenate(normed, axis=1) * ng_ref[...]).astype(y_ref.dtype)

    @pl.when(ci == nc - 1)
    def _():
        for l in range(n_prev):
            nconv_ref[l] = prev_conv[l][...]
            nssm_ref[l] = prev_ssm[l][...]
        for s in range(nb):
            for g in range(G_SSD):
                for r in range(R_SSD):
                    new_ssm[s, g * R_SSD + r] = st_scr[s, g, r * P_SSD:(r + 1) * P_SSD, :]


def _ssd(xbc, z, dt, lp, states, layer, prev, batch, length):
    c = math.gcd(length, CHUNK)
    nc = length // c
    nb = 1 if nc > 1 else SAMPLE_SEQS_PER_STEP
    assert batch % nb == 0 and c % SUBLANE == 0
    has_state = states is not None
    t = batch * length
    rows = nb * c
    tri = (jnp.arange(c)[:, None] >= jnp.arange(c)[None, :]).astype(BF16)
    eye = jnp.eye(LANE, dtype=BF16)
    head_of_ch = jnp.arange(SSD_W) // P_SSD
    expand = (jnp.arange(LANE)[:, None] == head_of_ch[None, :]).astype(BF16)
    sel = (head_of_ch[:, None] == jnp.arange(LANE)[None, :]).astype(F32).reshape(G_SSD, GRP_W, LANE)

    seq = lambda shape: pl.BlockSpec((nb,) + shape, lambda b, ci: (b,) + (0,) * len(shape))
    rowb = lambda w: pl.BlockSpec((rows, w), lambda b, ci: (b * nc + ci, 0))
    in_specs = [rowb(CONV_DIM), rowb(SSD_W), rowb(LANE),
                _full((CONV_K, CONV_DIM)), _full((1, CONV_DIM)), _full((1, LANE)), _full((1, LANE)),
                _full((1, SSD_W)), _full((1, SSD_W)),
                _full((c, c)), _full((LANE, LANE)), _full((LANE, SSD_W)), _full((G_SSD, GRP_W, LANE))]
    args = [xbc, z, dt, lp["conv_w"], lp["conv_b"], lp["dt_bias"], lp["a"], lp["d_skip"], lp["ssd_norm_g"],
            tri, eye, expand, sel]
    conv_shape, ssm_shape = (CONV_K - 1, CONV_DIM), (H_SSD, P_SSD, N_SSD)
    if has_state:
        stacked = lambda shape: pl.BlockSpec((None, nb) + shape, lambda b, ci: (layer, b) + (0,) * len(shape))
        in_specs += [stacked(conv_shape), stacked(ssm_shape)]
        args += list(states)
    in_specs += [seq(conv_shape)] * len(prev) + [seq(ssm_shape)] * len(prev)
    args += [p[0] for p in prev] + [p[1] for p in prev]
    block_bytes = (_nbytes((rows, CONV_DIM + 2 * SSD_W + LANE), F32) + _nbytes((nb, c + SUBLANE, CONV_DIM), F32)
                   + (3 + 4 * len(prev)) * _nbytes((nb, SSD_W, LANE), F32) + _nbytes((LANE, SSD_W), F32)
                   + 12 * _nbytes((c, SSD_W), F32))
    nconv_spec, nconv_shape = _state_out(prev, batch, nb, conv_shape)
    nssm_spec, nssm_shape = _state_out(prev, batch, nb, ssm_shape)
    return pl.pallas_call(
        functools.partial(_ssd_body, c, nb, nc, has_state, len(prev)),
        grid=(batch // nb, nc), in_specs=in_specs,
        out_specs=[rowb(SSD_W), nconv_spec, nssm_spec],
        out_shape=[jax.ShapeDtypeStruct((t, SSD_W), BF16), nconv_shape, nssm_shape],
        scratch_shapes=[pltpu.VMEM((nb, c + SUBLANE, CONV_DIM), F32),
                        pltpu.VMEM((nb, G_SSD, GRP_W, N_SSD), F32)],
        compiler_params=_params(("arbitrary", "arbitrary"), block_bytes), name="ssd",
    )(*args)


def _first_argmax(vals):
    best = vals[0]
    for v in vals[1:]:
        best = jnp.maximum(best, v)
    idx = jnp.full(best.shape, len(vals) - 1, jnp.int32)
    for i in range(len(vals) - 2, -1, -1):
        idx = jnp.where(vals[i] == best, i, idx)
    return best, idx


def _merge_body(bounds, *refs):
    n_groups = len(bounds)
    acts, refs = refs[:4 * n_groups], refs[4 * n_groups:]
    weights, refs = refs[:7], refs[7:]
    x2_refs, shared = refs[:n_groups], refs[n_groups:]
    i = pl.program_id(0)
    for gi, (first, count) in enumerate(bounds):
        @pl.when((i >= first) & (i < first + count))
        def _():
            _merge_tile(*acts[4 * gi:4 * gi + 4], *weights, x2_refs[gi], *shared)


def _merge_tile(ret_ref, ssd_ref, gates_ref, x_ref, wr_ref, ws_ref, wo_ref, g2_ref, rhi_ref, rlo_ref, upp_ref,
                x2_ref, hw_ref, route_ref, cnt_ref):
    ret_y = _dot(ret_ref[...], wr_ref[...])
    ssd_y = _dot(ssd_ref[...], ws_ref[...])
    m = (jax.nn.sigmoid(gates_ref[:, :D_MODEL]) * ret_y + jax.nn.sigmoid(gates_ref[:, D_MODEL:]) * ssd_y)
    x2 = x_ref[...] + _dot(m.astype(BF16), wo_ref[...])
    x2_ref[...] = x2
    h2 = _rmsnorm(x2, g2_ref[...])
    bits = pltpu.bitcast(h2.astype(BF16).astype(F32), jnp.uint32)
    half = D_MODEL // 2
    word = lax.bitwise_or(lax.bitwise_and(bits[:, half:], jnp.uint32(HIGH_HALF)),
                          lax.shift_right_logical(bits[:, :half], jnp.uint32(16)))
    n_rows = h2.shape[0]
    for k in range(PACKED_ROWS):
        hw_ref[pl.ds(k, n_rows, stride=SUBLANE), :] = word[:, k * LANE:(k + 1) * LANE]

    h_hi = h2.astype(BF16)
    h_lo = (h2 - h_hi.astype(F32)).astype(BF16)
    logits = _dot_nt(rhi_ref[...], h_hi) + (_dot_nt(rlo_ref[...], h_hi) + _dot_nt(rhi_ref[...], h_lo))
    colv = lambda i: logits[i:i + 1, :]
    glog = [colv(i) for i in range(N_EGROUPS)]
    gmax, gidx = _first_argmax(glog)
    den = glog[0] * 0.0
    for v in glog:
        den = den + jnp.exp(v - gmax)
    gw = 1.0 / den
    sel = []
    for j in range(EXP_PER_GROUP):
        v = colv(N_EGROUPS + (N_EGROUPS - 1) * EXP_PER_GROUP + j)
        for g in range(N_EGROUPS - 2, -1, -1):
            v = jnp.where(gidx == g, colv(N_EGROUPS + g * EXP_PER_GROUP + j), v)
        sel.append(v)
    v1, i1 = _first_argmax(sel)
    v2, i2 = _first_argmax([jnp.where(i1 == j, -jnp.inf, sel[j]) for j in range(EXP_PER_GROUP)])
    e = jnp.exp(v2 - v1)
    w1 = gw / (1.0 + e)
    w2 = gw * e / (1.0 + e)
    first_low = i1 < i2
    lo = jnp.where(first_low, i1, i2)
    hi = jnp.where(first_low, i2, i1)
    pair = jnp.where(lo == 0, hi - 1, jnp.where(lo == 1, hi + 1, N_PAIRS - 1))
    cls = gidx * N_PAIRS + pair
    w_lo = jnp.where(first_low, w1, w2)
    w_hi = jnp.where(first_low, w2, w1)

    sub = lax.broadcasted_iota(jnp.int32, (META_W, n_rows), 0)
    onehot = jnp.where(sub == cls, 1.0, 0.0).astype(BF16)
    before = _dot(onehot, upp_ref[...])
    rank = jnp.sum(onehot.astype(F32) * before, axis=0, keepdims=True)
    meta = jnp.where(sub == 0, w_lo, jnp.where(sub == 1, w_hi, jnp.where(
        sub == 2, cls.astype(F32), jnp.where(sub == 3, rank, 0.0))))
    route_ref[0] = meta[0:SUBLANE, :]
    meta_bits = pltpu.bitcast(meta.T, jnp.uint32)
    for k in range(PACKED_ROWS, TOKEN_TILE_ROWS):
        hw_ref[pl.ds(k, n_rows, stride=SUBLANE), :] = meta_bits
    cnt_ref[0] = _dot_nt(jnp.ones((SUBLANE, n_rows), BF16), onehot)[0:1, :]


def _merge(group_acts, lp, big, layer):
    counts = [a[3].shape[0] // ROW_TILE for a in group_acts]
    bounds = tuple((sum(counts[:gi]), counts[gi]) for gi in range(len(counts)))
    n_tiles = sum(counts)
    total_rows = n_tiles * ROW_TILE
    upp = (jnp.arange(ROW_TILE)[:, None] < jnp.arange(ROW_TILE)[None, :]).astype(BF16)
    stacked = [big["w_ret_out"], big["w_ssd_out"], big["w_o"]]
    ws = [lp["ln2_g"], lp["router_hi"], lp["router_lo"], upp]
    block_bytes = (sum(_nbytes(w.shape[1:], w.dtype) for w in stacked) + sum(_nbytes(w.shape, w.dtype) for w in ws)
                   + len(counts) * (_nbytes((ROW_TILE, RET_W + SSD_W), BF16) + _nbytes((ROW_TILE, 4 * D_MODEL), F32))
                   + _nbytes((ROW_TILE, 3 * D_MODEL), F32))

    def group_row(width, first, count):
        return pl.BlockSpec((ROW_TILE, width), lambda i: (jnp.clip(i - first, 0, count - 1), 0))

    in_specs, args, x2_specs, x2_shapes = [], [], [], []
    for (first, count), acts in zip(bounds, group_acts, strict=True):
        in_specs += [group_row(w, first, count) for w in (RET_W, SSD_W, 2 * D_MODEL, D_MODEL)]
        args += list(acts)
        x2_specs.append(group_row(D_MODEL, first, count))
        x2_shapes.append(jax.ShapeDtypeStruct((count * ROW_TILE, D_MODEL), F32))
    in_specs += [_layer_block(w, layer) for w in stacked] + [_full(w.shape) for w in ws]
    args += stacked + ws
    outs = pl.pallas_call(
        functools.partial(_merge_body, bounds), grid=(n_tiles,), in_specs=in_specs,
        out_specs=x2_specs + [_token_spec(0), pl.BlockSpec((1, SUBLANE, ROW_TILE), lambda i: (i, 0, 0)),
                              pl.BlockSpec((1, 1, META_W), lambda i: (i, 0, 0))],
        out_shape=x2_shapes + [jax.ShapeDtypeStruct((total_rows * TOKEN_TILE_ROWS, LANE), jnp.uint32),
                               jax.ShapeDtypeStruct((n_tiles, SUBLANE, ROW_TILE), F32),
                               jax.ShapeDtypeStruct((n_tiles, 1, META_W), F32)],
        compiler_params=_params(("arbitrary",), block_bytes), name="merge_router",
    )(*args)
    return outs[:len(counts)], outs[len(counts):]


PLAN_ROWS = LANE
TAB_ROWS = SUBLANE
TAB_LO, TAB_HI, TAB_USED = 0, 1, 2
BYTE = 256.0
MOE_RING = 3


def _row_of(col_vals):
    pick = (lax.broadcasted_iota(jnp.int32, (TAB_ROWS, LANE), 0)
            == lax.broadcasted_iota(jnp.int32, (TAB_ROWS, LANE), 1)).astype(BF16)
    hi = jnp.floor(col_vals * (1.0 / BYTE))
    lo = col_vals - hi * BYTE
    return BYTE * _dot_nt(pick, hi.astype(BF16)) + _dot_nt(pick, lo.astype(BF16))


def _plan_body(tile, n_row_tiles, route_ref, cnt_ref, low_ref, upp_ref, pos_ref, tab_ref, offs_scr):
    cnt = cnt_ref[...]
    tot = jnp.sum(cnt, axis=0, keepdims=True)
    cls_tiles = jnp.floor((tot + (tile - 1.0)) * (1.0 / tile))
    first_tile = _dot(jnp.broadcast_to(cls_tiles, (SUBLANE, LANE)).astype(BF16), upp_ref[...])[0:1, :]
    base = first_tile * tile
    ends = base + cls_tiles * tile
    offs_scr[...] = base + _dot(low_ref[...], cnt.astype(BF16))

    def per_row_tile(j, carry):
        meta = route_ref[j]
        sub = lax.broadcasted_iota(jnp.int32, (LANE, ROW_TILE), 0).astype(F32)
        onehot = jnp.where(sub == meta[2:3, :], 1.0, 0.0).astype(BF16)
        offs = offs_scr[pl.ds(j, 1), :]
        hi = jnp.floor(offs * (1.0 / BYTE))
        rows8 = lambda v: jnp.broadcast_to(v, (SUBLANE, LANE)).astype(BF16)
        pos = BYTE * _dot(rows8(hi), onehot) + _dot(rows8(offs - hi * BYTE), onehot)
        pos_ref[j] = (pos[0:1, :] + meta[3:4, :]).astype(jnp.int32)
        return carry

    lax.fori_loop(0, n_row_tiles, per_row_tile, 0)

    lane = lax.broadcasted_iota(jnp.int32, (PLAN_ROWS, LANE), 1)
    tile_idx = lax.broadcasted_iota(jnp.int32, (PLAN_ROWS, LANE), 0).astype(F32)[:, 0:1]
    n_used = jnp.sum(cls_tiles, axis=1, keepdims=True)
    start = jnp.minimum(tile_idx, n_used - 1.0) * tile
    tcls = jnp.sum(jnp.where((lane < N_CLASSES) & (ends <= start), 1.0, 0.0), axis=1, keepdims=True)
    tcls = jnp.minimum(tcls, N_CLASSES - 1.0)
    group = sum(jnp.where(tcls >= k * N_PAIRS, 1.0, 0.0) for k in range(1, N_EGROUPS))
    pair = tcls - group * N_PAIRS
    p_lo = jnp.where(pair >= 3, 1.0, 0.0) + jnp.where(pair >= 5, 1.0, 0.0)
    p_hi = jnp.where(pair == 0, 1.0, jnp.where((pair == 1) | (pair == 3), 2.0, 3.0))
    table = jnp.where(lane == TAB_LO, group * EXP_PER_GROUP + p_lo,
                      jnp.where(lane == TAB_HI, group * EXP_PER_GROUP + p_hi,
                                jnp.where(lane == TAB_USED, n_used, 0.0)))
    tab_ref[...] = _row_of(table).astype(jnp.int32)


def _plan(route, counts, tile):
    nrt = route.shape[0]
    t = nrt * ROW_TILE
    assert nrt <= PLAN_ROWS and t // tile + N_CLASSES <= PLAN_ROWS
    cnt = jnp.pad(counts[:, 0, :], ((0, PLAN_ROWS - nrt), (0, 0)))
    idx = jnp.arange(PLAN_ROWS)
    low = (idx[:, None] > idx[None, :]).astype(BF16)
    upp = (idx[:, None] < idx[None, :]).astype(BF16)
    pos, tab = pl.pallas_call(
        functools.partial(_plan_body, tile, nrt),
        out_shape=[jax.ShapeDtypeStruct((nrt, 1, ROW_TILE), jnp.int32),
                   jax.ShapeDtypeStruct((TAB_ROWS, PLAN_ROWS), jnp.int32)],
        scratch_shapes=[pltpu.VMEM((PLAN_ROWS, LANE), F32)],
        compiler_params=pltpu.CompilerParams(vmem_limit_bytes=_vmem_limit(_nbytes(route.shape, F32))),
        name="moe_plan",
    )(route, cnt, low, upp)
    return pos.reshape(t), tab


def _moe_body(tile, n_tok, pos_ref, tab_ref, hw_hbm, spare_hbm, wgl_ref, wul_ref, wdl_ref, wgh_ref, wuh_ref,
              wdh_ref, y_hbm, buf, obuf, tok_smem, gsem, ssem, isem):
    i = pl.program_id(0)
    n_used = tab_ref[TAB_USED, 0]
    ring = buf.shape[0]
    cur = lax.rem(i, ring)
    nxt1 = lax.rem(i + 1, ring)
    nxt2 = lax.rem(i + 2, ring)
    n_points = 16
    per_point = tile // n_points
    tr = TOKEN_TILE_ROWS

    def token_tile(ref, token):
        start = token * tr if isinstance(token, int) else pl.multiple_of(token * tr, tr)
        return ref.at[pl.ds(start, tr)]

    def dma_thread(r):
        return r % 2 if isinstance(r, int) else 0

    def gather_start(block, r, sl):
        src = jnp.minimum(tok_smem[block * tile + r], n_tok - 1)
        pltpu.make_async_copy(token_tile(hw_hbm, src), token_tile(buf.at[sl], r), gsem.at[sl]).start(dma_thread(r))

    def scatter_start(block, r, sl):
        dst = tok_smem[block * tile + r]
        pltpu.make_async_copy(token_tile(obuf.at[sl], r), token_tile(y_hbm, dst), ssem.at[sl]).start(dma_thread(r))

    def gather_wait(sl):
        pltpu.make_async_copy(hw_hbm.at[pl.ds(0, tile * tr)], buf.at[sl], gsem.at[sl]).wait()

    def scatter_wait(sl):
        pltpu.make_async_copy(obuf.at[sl], y_hbm.at[pl.ds(0, tile * tr)], ssem.at[sl]).wait()

    def for_rows(n, fn):
        lax.fori_loop(0, n, lambda r, carry: (fn(r), carry)[1], 0, unroll=8)

    def block_of(t):
        return jnp.minimum(t, n_used - 1) + 1

    @pl.when(i < n_used)
    def _():
        @pl.when(i == 0)
        def _():
            fill = pltpu.make_async_copy(spare_hbm, tok_smem, isem)
            fill.start()
            fill.wait()

            def place(t):
                tok_smem[tile + pos_ref[t]] = t
            for_rows(n_tok, place)
            obuf[ring - 1] = jnp.zeros(obuf.shape[1:], F32)
            for_rows(tile, lambda r: gather_start(block_of(0), r, 0))
            for_rows(tile, lambda r: gather_start(block_of(1), r, 1))

        gather_wait(cur)
        issued = [0]

        def issue_point():
            for r in range(issued[0] * per_point, (issued[0] + 1) * per_point):
                gather_start(block_of(i + 2), r, nxt2)
                scatter_start(i, r, nxt2)
            issued[0] += 1

        def tile_row(k):
            return buf[cur, pl.ds(k, tile, stride=tr), :]

        words = [tile_row(k) for k in range(PACKED_ROWS)]
        low = [pltpu.bitcast(lax.shift_left(w, jnp.uint32(16)), F32).astype(BF16) for w in words]
        high = [pltpu.bitcast(lax.bitwise_and(w, jnp.uint32(HIGH_HALF)), F32).astype(BF16) for w in words]
        xb = jnp.concatenate(low + high, axis=1)
        meta = pltpu.bitcast(tile_row(PACKED_ROWS), F32)
        w_lo, w_hi = meta[:, 0:1], meta[:, 1:2]
        half = D_FF // 2
        acts = []
        for wg_ref, wu_ref in ((wgl_ref, wul_ref), (wgh_ref, wuh_ref)):
            parts = []
            for n in range(2):
                gate = _dot(xb, wg_ref[:, n * half:(n + 1) * half])
                issue_point()
                up = _dot(xb, wu_ref[:, n * half:(n + 1) * half])
                issue_point()
                parts.append((jax.nn.silu(gate) * up).astype(BF16))
            acts.append(jnp.concatenate(parts, axis=1))
        quarter = D_MODEL // 4
        for n in range(4):
            cols = slice(n * quarter, (n + 1) * quarter)
            d_lo = _dot(acts[0], wdl_ref[:, cols])
            issue_point()
            d_hi = _dot(acts[1], wdh_ref[:, cols])
            issue_point()
            out = d_lo * w_lo + d_hi * w_hi
            for kk in range(quarter // LANE):
                obuf[cur, pl.ds(n * (quarter // LANE) + kk, tile, stride=tr), :] = out[:, kk * LANE:(kk + 1) * LANE]
        assert issued[0] == n_points

        @pl.when(i >= 1)
        def _():
            scatter_wait(nxt1)

        @pl.when(i == n_used - 1)
        def _():
            scatter_wait(nxt2)
            for_rows(tile, lambda r: scatter_start(i + 1, r, cur))
            scatter_wait(cur)
            gather_wait(nxt1)
            gather_wait(nxt2)


def _moe(hw, pos, tab, big, layer, tile):
    t = hw.shape[0] // TOKEN_TILE_ROWS
    n_tiles = t // tile + N_CLASSES
    wspec = lambda shape, row: pl.BlockSpec((None, None) + shape, lambda i, pos, tab: (layer, tab[row, i], 0, 0))
    in_specs = [pl.BlockSpec(memory_space=pl.ANY)] * 2
    spare = t + jnp.arange((n_tiles + 1) * tile, dtype=jnp.int32) % tile
    for row in (TAB_LO, TAB_HI):
        in_specs += [wspec((D_MODEL, D_FF), row), wspec((D_MODEL, D_FF), row), wspec((D_FF, D_MODEL), row)]
    block_bytes = 6 * _nbytes((D_MODEL, D_FF), BF16) + 6 * _nbytes((tile, D_MODEL), F32)
    tile_rows = tile * TOKEN_TILE_ROWS
    grid_spec = pltpu.PrefetchScalarGridSpec(
        num_scalar_prefetch=2, grid=(n_tiles,), in_specs=in_specs,
        out_specs=pl.BlockSpec(memory_space=pl.ANY),
        scratch_shapes=[pltpu.VMEM((MOE_RING, tile_rows, LANE), jnp.uint32),
                        pltpu.VMEM((MOE_RING, tile_rows, LANE), F32),
                        pltpu.SMEM(((n_tiles + 1) * tile,), jnp.int32),
                        pltpu.SemaphoreType.DMA((MOE_RING,)), pltpu.SemaphoreType.DMA((MOE_RING,)),
                        pltpu.SemaphoreType.DMA(())])
    return pl.pallas_call(
        functools.partial(_moe_body, tile, t), grid_spec=grid_spec,
        out_shape=jax.ShapeDtypeStruct(((t + tile) * TOKEN_TILE_ROWS, LANE), F32),
        compiler_params=_params(("arbitrary",), block_bytes), name="moe",
    )(pos, tab, hw, spare, big["w_gate"], big["w_up"], big["w_down"], big["w_gate"], big["w_up"], big["w_down"])


FINAL_TILE = 1024


def _final_body(x_ref, y_ref, g_ref, o_ref):
    o_ref[...] = _rmsnorm(x_ref[...] + _token_rows(y_ref, FINAL_TILE), g_ref[...])


def _final_norm(x, y, y_offset, g):
    t = x.shape[0]
    row = pl.BlockSpec((FINAL_TILE, D_MODEL), lambda i: (i, 0))
    return pl.pallas_call(
        _final_body, grid=(t // FINAL_TILE,),
        in_specs=[row, _token_spec(y_offset, FINAL_TILE), _full((1, D_MODEL))],
        out_specs=row, out_shape=jax.ShapeDtypeStruct((t, D_MODEL), F32),
        compiler_params=_params(("arbitrary",), 3 * _nbytes((FINAL_TILE, D_MODEL), F32)), name="final_norm",
    )(x, y, g)


def _layer_params(i, ln1_g, ret_gn_g, conv_w, conv_b, dt_bias, a_log, d_skip, ssd_norm_g, ln2_g, w_rg, w_re):
    pad_heads = lambda v: jnp.pad(v.astype(F32), (0, LANE - H_SSD)).reshape(1, LANE)
    router = jnp.pad(jnp.concatenate([w_rg[i], w_re[i]], axis=1),
                     ((0, 0), (0, META_W - N_EGROUPS - N_EXPERTS)))
    router_hi = router.astype(BF16)
    return {
        "ln1_g": ln1_g[i].reshape(1, D_MODEL),
        "ret_gn_g": ret_gn_g[i].reshape(1, RET_W),
        "conv_w": conv_w[i], "conv_b": conv_b[i].reshape(1, CONV_DIM),
        "dt_bias": pad_heads(dt_bias[i]), "a": pad_heads(-jnp.exp(a_log[i].astype(F32))),
        "d_skip": jnp.repeat(d_skip[i].astype(F32), P_SSD).reshape(1, SSD_W),
        "ssd_norm_g": ssd_norm_g[i].reshape(1, SSD_W),
        "ln2_g": ln2_g[i].reshape(1, D_MODEL),
        "router_hi": router_hi.T, "router_lo": (router - router_hi.astype(F32)).astype(BF16).T,
    }


def _stacked_params(w_in, w_ret_out, w_ssd_out, w_o, w_gate, w_up, w_down):
    offs = [0]
    for s in IN_SIZES:
        offs.append(offs[-1] + s)
    return {
        "w_in": [w_in.astype(BF16),
                 jnp.pad(w_in[:, :, offs[6]:offs[7]], ((0, 0), (0, 0), (0, LANE - H_SSD))).astype(BF16),
                 w_in[:, :, offs[7]:offs[9]].astype(BF16)],
        "w_ret_out": w_ret_out.astype(BF16), "w_ssd_out": w_ssd_out.astype(BF16), "w_o": w_o.astype(BF16),
        "w_gate": w_gate.astype(BF16), "w_up": w_up.astype(BF16), "w_down": w_down.astype(BF16),
    }


class _Group:
    def __init__(self, x, start, states, row_offset):
        self.batch, self.length, _ = x.shape
        self.x = x.reshape(self.batch * self.length, D_MODEL)
        self.start, self.states, self.row_offset = start, states, row_offset
        self.rets, self.ssms, self.convs = [], [], []


def kernel(x_prompt, x_sample, state_ret, state_ssm, state_conv, ln1_g, w_in, ret_gn_g, w_ret_out, conv_w, conv_b, dt_bias, a_log, d_skip, ssd_norm_g, w_ssd_out, w_o, ln2_g, w_router_group, w_router_expert, w_e_gate, w_e_up, w_e_down, lnf_g):
    depth = w_in.shape[0]
    past_len = 16384.0
    layers = [_layer_params(i, ln1_g, ret_gn_g, conv_w, conv_b, dt_bias, a_log, d_skip, ssd_norm_g, ln2_g,
                            w_router_group, w_router_expert) for i in range(depth)]
    big = _stacked_params(w_in, w_ret_out, w_ssd_out, w_o, w_e_gate, w_e_up, w_e_down)
    n_prompt = x_prompt.shape[0] * x_prompt.shape[1]
    groups = [_Group(x_prompt, 0.0, None, 0),
              _Group(x_sample, past_len, (state_ret, state_ssm, state_conv), n_prompt)]
    y = None
    for i, lp in enumerate(layers):
        group_acts = []
        last = i == depth - 1
        for grp in groups:
            outs = _inproj(grp.x, y, grp.row_offset, lp["ln1_g"], big["w_in"], i)
            if y is not None:
                grp.x, outs = outs[0], outs[1:]
            qkvr, z, xbc, dt, gates = outs
            ret_state = None if grp.states is None else grp.states[0]
            ssd_state = None if grp.states is None else (grp.states[2], grp.states[1])
            ret, new_ret = _retention(qkvr, grp.start, lp["ret_gn_g"], ret_state, i,
                                      grp.rets if last else (), grp.batch, grp.length)
            ssd, new_conv, new_ssm = _ssd(xbc, z, dt, lp, ssd_state, i,
                                          list(zip(grp.convs, grp.ssms)) if last else (), grp.batch, grp.length)
            grp.rets.append(new_ret)
            grp.ssms.append(new_ssm)
            grp.convs.append(new_conv)
            group_acts.append((ret, ssd, gates, grp.x))
        new_x, (hw, route, counts) = _merge(group_acts, lp, big, i)
        for grp, x2 in zip(groups, new_x, strict=True):
            grp.x = x2
        pos, tab = _plan(route, counts, ROW_TILE)
        y = _moe(hw, pos, tab, big, i, ROW_TILE)
    outs = []
    for grp in groups:
        out = _final_norm(grp.x, y, grp.row_offset, lnf_g.reshape(1, D_MODEL))
        outs.append(out.reshape(grp.batch, grp.length, D_MODEL))
    states = []
    for grp in groups:
        states += [grp.rets[-1], grp.ssms[-1], grp.convs[-1]]
    return tuple(outs + states)
```

```python
import functools
import math

import jax
import jax.numpy as jnp
from jax import lax
from jax.experimental import pallas as pl
from jax.experimental.pallas import tpu as pltpu

F32 = jnp.float32
BF16 = jnp.bfloat16

D_MODEL = 1024
H_RET, DK_RET, DV_RET = 4, 128, 128
RET_W = H_RET * DV_RET
ROPE_BASE = 10000.0
H_SSD, P_SSD, G_SSD, N_SSD = 16, 64, 2, 64
R_SSD = H_SSD // G_SSD
SSD_W = H_SSD * P_SSD
GRP_W = SSD_W // G_SSD
CONV_K = 4
CONV_DIM = SSD_W + 2 * G_SSD * N_SSD
CHUNK = 128
N_EGROUPS, EXP_PER_GROUP = 4, 4
N_EXPERTS = N_EGROUPS * EXP_PER_GROUP
N_PAIRS = EXP_PER_GROUP * (EXP_PER_GROUP - 1) // 2
N_CLASSES = N_EGROUPS * N_PAIRS
D_FF = 512
EPS = 1e-6
IN_SIZES = [RET_W, RET_W, RET_W, RET_W, SSD_W, CONV_DIM, H_SSD, D_MODEL, D_MODEL]

LANE = 128
SUBLANE = 8
MIB = 1024 * 1024

ROW_TILE = 256
META_W = LANE
TOKEN_TILE_ROWS = D_MODEL // LANE
assert TOKEN_TILE_ROWS == SUBLANE
PACKED_ROWS = D_MODEL // (2 * LANE)
HIGH_HALF = 0xFFFF0000
SAMPLE_SEQS_PER_STEP = 8
PROMPT_SEQS_PER_STEP = 4


def _vmem_limit(block_bytes):
    return int(min(2 * block_bytes + 16 * MIB, 56 * MIB))


def _params(semantics, block_bytes):
    return pltpu.CompilerParams(dimension_semantics=semantics,
                                vmem_limit_bytes=_vmem_limit(block_bytes))


def _nbytes(shape, dtype):
    return math.prod(shape) * jnp.dtype(dtype).itemsize


def _full(shape):
    return pl.BlockSpec(shape, lambda *_: (0,) * len(shape))


def _dot(a, b):
    return jnp.dot(a, b, preferred_element_type=F32)


def _dot_nt(a, b):
    return lax.dot_general(a, b, (((1,), (1,)), ((), ())), preferred_element_type=F32)


def _dot_tn(a, b):
    return lax.dot_general(a, b, (((0,), (0,)), ((), ())), preferred_element_type=F32)


def _split3(a):
    hi = a.astype(BF16)
    r1 = a - hi.astype(F32)
    mid = r1.astype(BF16)
    lo = (r1 - mid.astype(F32)).astype(BF16)
    return hi, mid, lo


def _rmsnorm(x, g):
    r = lax.rsqrt(jnp.mean(x * x, axis=-1, keepdims=True) + EPS)
    return x * r * g


def _token_rows(ref, rows):
    return jnp.concatenate([ref[pl.ds(k, rows, stride=SUBLANE), :] for k in range(TOKEN_TILE_ROWS)], axis=1)


def _token_spec(row_offset, rows=ROW_TILE):
    assert row_offset % rows == 0
    first = row_offset // rows
    return pl.BlockSpec((rows * TOKEN_TILE_ROWS, LANE), lambda i: (i + first, 0))


def _inproj_body(has_add, *refs):
    n_in = 6 if has_add else 5
    ins, outs = refs[:n_in], refs[n_in:]
    if has_add:
        x_ref, y_ref, g_ref = ins[:3]
        x = x_ref[...] + _token_rows(y_ref, ROW_TILE)
        outs[0][...] = x
        outs = outs[1:]
    else:
        x_ref, g_ref = ins[:2]
        x = x_ref[...]
    w_main, w_dt, w_gates = ins[-3:]
    h = _rmsnorm(x, g_ref[...]).astype(BF16)
    col = 0
    for o_ref in outs[:3]:
        width = o_ref.shape[1]
        o_ref[...] = _dot(h, w_main[:, col:col + width]).astype(o_ref.dtype)
        col += width
    outs[3][...] = _dot(h, w_dt[...])
    outs[4][...] = _dot(h, w_gates[...]).astype(outs[4].dtype)


def _layer_block(arr, layer):
    shape = arr.shape[1:]
    return pl.BlockSpec((None,) + shape, lambda *_: (layer,) + (0,) * len(shape))


def _inproj(x, y, y_offset, g, ws, layer):
    t = x.shape[0]
    has_add = y is not None
    widths = [4 * RET_W, SSD_W, CONV_DIM, LANE, 2 * D_MODEL]
    assert sum(widths[:3]) == sum(IN_SIZES[:6])
    row = lambda w: pl.BlockSpec((ROW_TILE, w), lambda i: (i, 0))
    in_specs = [row(D_MODEL)] + ([_token_spec(y_offset)] if has_add else []) + [_full((1, D_MODEL))]
    in_specs += [_layer_block(w, layer) for w in ws]
    out_shape = [jax.ShapeDtypeStruct((t, w), F32 if w == LANE else BF16) for w in widths]
    out_specs = [row(w) for w in widths]
    if has_add:
        out_shape = [jax.ShapeDtypeStruct((t, D_MODEL), F32)] + out_shape
        out_specs = [row(D_MODEL)] + out_specs
    block_bytes = (sum(_nbytes(w.shape[1:], BF16) for w in ws)
                   + _nbytes((ROW_TILE, sum(widths) + 3 * D_MODEL), F32))
    args = ([x, y] if has_add else [x]) + [g] + list(ws)
    return pl.pallas_call(
        functools.partial(_inproj_body, has_add),
        grid=(t // ROW_TILE,), in_specs=in_specs, out_specs=out_specs, out_shape=out_shape,
        compiler_params=_params(("arbitrary",), block_bytes), name="inproj",
    )(*args)


def _retention_body(c, nb, nc, has_state, n_prev, *refs):
    refs = list(refs)
    qkvr_ref, cos_ref, sin_ref, dmat_ref, qdec_ref, kdec_ref, cdec_ref, gn_ref = refs[:8]
    refs = refs[8:]
    if has_state:
        s0_ref = refs.pop(0)
    prev_refs = [refs.pop(0) for _ in range(n_prev)]
    o_ref, ns_ref, st_scr = refs
    ci = pl.program_id(1)

    @pl.when(ci == 0)
    def _():
        if has_state:
            st_scr[...] = s0_ref[...]
        else:
            st_scr[...] = jnp.zeros(st_scr.shape, F32)

    cosv, sinv = cos_ref[...], sin_ref[...]
    for s in range(nb):
        for h in range(H_RET):
            col = lambda j: slice(j * RET_W + h * DK_RET, j * RET_W + (h + 1) * DK_RET)
            q, k = qkvr_ref[s, :, col(0)].astype(F32), qkvr_ref[s, :, col(1)].astype(F32)
            v, rg = qkvr_ref[s, :, col(2)], qkvr_ref[s, :, col(3)].astype(F32)
            qr = q * cosv + pltpu.roll(q, DK_RET // 2, axis=1) * sinv
            kr = (k * cosv + pltpu.roll(k, DK_RET // 2, axis=1) * sinv) * (DK_RET ** -0.5)
            qb, kb, vb = qr.astype(BF16), kr.astype(BF16), v.astype(BF16)
            state = st_scr[s, h]
            sc = _dot_nt(qb, kb) * dmat_ref[h]
            o = _dot(sc.astype(BF16), vb) + _dot(qb, state.astype(BF16)) * qdec_ref[h]
            st_scr[s, h] = cdec_ref[h] * state + _dot_tn((kr * kdec_ref[h]).astype(BF16), vb)
            mu = jnp.mean(o, axis=-1, keepdims=True)
            d = o - mu
            var = jnp.mean(d * d, axis=-1, keepdims=True)
            on = d * lax.rsqrt(var + EPS) * gn_ref[:, h * DV_RET:(h + 1) * DV_RET]
            o_ref[s, :, h * DV_RET:(h + 1) * DV_RET] = (jax.nn.silu(rg) * on).astype(o_ref.dtype)

    @pl.when(ci == nc - 1)
    def _():
        if n_prev:
            for l, prev_ref in enumerate(prev_refs):
                ns_ref[l] = prev_ref[...]
            ns_ref[n_prev] = st_scr[...]
        else:
            ns_ref[...] = st_scr[...]


def _state_out(prev, batch, nb, shape):
    zeros = (0,) * len(shape)
    if prev:
        depth = len(prev) + 1
        return (pl.BlockSpec((depth, nb) + shape, lambda b, ci: (0, b) + zeros),
                jax.ShapeDtypeStruct((depth, batch) + shape, F32))
    return pl.BlockSpec((nb,) + shape, lambda b, ci: (b,) + zeros), jax.ShapeDtypeStruct((batch,) + shape, F32)


def _retention(qkvr, start, gn_g, states, layer, prev, batch, length):
    c = math.gcd(length, CHUNK)
    nc = length // c
    nb = PROMPT_SEQS_PER_STEP if nc > 1 else SAMPLE_SEQS_PER_STEP
    assert batch % nb == 0
    has_state = states is not None
    t = batch * length
    half = DK_RET // 2
    pos = start + jnp.arange(length, dtype=F32)
    inv = ROPE_BASE ** (-jnp.arange(half, dtype=F32) / half)
    ang = pos[:, None] * inv[None, :]
    cosv = jnp.concatenate([jnp.cos(ang), jnp.cos(ang)], axis=1)
    sinv = jnp.concatenate([-jnp.sin(ang), jnp.sin(ang)], axis=1)
    log_g = jnp.log1p(-jnp.exp2(-5.0 - jnp.arange(H_RET, dtype=F32)))
    idx = jnp.arange(c, dtype=F32)
    rel = idx[:, None] - idx[None, :]
    causal = rel >= 0
    dmat = jnp.where(causal[None], jnp.exp(jnp.where(causal, rel, 0.0)[None] * log_g[:, None, None]), 0.0)
    qdec = jnp.broadcast_to(jnp.exp((idx[None, :] + 1.0) * log_g[:, None])[:, :, None], (H_RET, c, LANE))
    kdec = jnp.broadcast_to(jnp.exp((c - 1.0 - idx[None, :]) * log_g[:, None])[:, :, None], (H_RET, c, LANE))
    cdec = jnp.exp(c * log_g)

    rows = nb * c
    state_spec = pl.BlockSpec((nb, H_RET, DK_RET, DV_RET), lambda b, ci: (b, 0, 0, 0))
    seq_rows = lambda w: pl.BlockSpec((nb, c, w), lambda b, ci: (b, ci, 0))
    in_specs = [
        seq_rows(4 * RET_W),
        pl.BlockSpec((c, LANE), lambda b, ci: (ci, 0)),
        pl.BlockSpec((c, LANE), lambda b, ci: (ci, 0)),
        _full((H_RET, c, c)), _full((H_RET, c, LANE)), _full((H_RET, c, LANE)),
        pl.BlockSpec(memory_space=pltpu.SMEM),
        _full((1, RET_W)),
    ]
    args = [qkvr.reshape(batch, length, 4 * RET_W), cosv, sinv, dmat, qdec, kdec, cdec, gn_g]
    if has_state:
        in_specs.append(pl.BlockSpec((None, nb, H_RET, DK_RET, DV_RET), lambda b, ci: (layer, b, 0, 0, 0)))
        args.append(states)
    in_specs += [state_spec] * len(prev)
    args += list(prev)
    block_bytes = (_nbytes((rows, 4 * RET_W + 2 * LANE + RET_W), F32)
                   + (3 + 2 * len(prev)) * _nbytes((nb, H_RET, DK_RET, DV_RET), F32)
                   + 3 * _nbytes((H_RET, c, LANE), F32))
    ns_spec, ns_shape = _state_out(prev, batch, nb, (H_RET, DK_RET, DV_RET))
    out, new_state = pl.pallas_call(
        functools.partial(_retention_body, c, nb, nc, has_state, len(prev)),
        grid=(batch // nb, nc), in_specs=in_specs,
        out_specs=[seq_rows(RET_W), ns_spec],
        out_shape=[jax.ShapeDtypeStruct((batch, length, RET_W), BF16), ns_shape],
        scratch_shapes=[pltpu.VMEM((nb, H_RET, DK_RET, DV_RET), F32)],
        compiler_params=_params(("arbitrary", "arbitrary"), block_bytes), name="retention",
    )(*args)
    return out.reshape(t, RET_W), new_state


def _ssd_body(c, nb, nc, has_state, n_prev, *refs):
    refs = list(refs)
    (xbc_ref, z_ref, dt_ref, cw_ref, cb_ref, dtb_ref, a_ref, dsk_ref, ng_ref,
     tri_ref, eye_ref, exp_ref, sel_ref) = refs[:13]
    refs = refs[13:]
    if has_state:
        sconv_ref, sssm_ref = refs.pop(0), refs.pop(0)
    prev_conv = [refs.pop(0) for _ in range(n_prev)]
    prev_ssm = [refs.pop(0) for _ in range(n_prev)]
    y_ref, nconv_ref, nssm_ref, ext_scr, st_scr = refs
    new_conv = nconv_ref.at[n_prev] if n_prev else nconv_ref
    new_ssm = nssm_ref.at[n_prev] if n_prev else nssm_ref
    ci = pl.program_id(1)
    pad = SUBLANE
    hist = CONV_K - 1

    @pl.when(ci == 0)
    def _():
        ext_scr[:, 0:pad, :] = jnp.zeros((nb, pad, CONV_DIM), F32)
        if has_state:
            ext_scr[:, pad - hist:pad, :] = sconv_ref[...]
            for s in range(nb):
                for g in range(G_SSD):
                    for r in range(R_SSD):
                        st_scr[s, g, r * P_SSD:(r + 1) * P_SSD, :] = sssm_ref[s, g * R_SSD + r]
        else:
            st_scr[...] = jnp.zeros(st_scr.shape, F32)

    causal = (lax.broadcasted_iota(jnp.int32, (c, c), 0) >= lax.broadcasted_iota(jnp.int32, (c, c), 1))
    left_half = (lax.broadcasted_iota(jnp.int32, (c, SSD_W), 1) % LANE) < P_SSD
    tri, eye, expand = tri_ref[...], eye_ref[...], exp_ref[...]

    for s in range(nb):
        rows = slice(s * c, (s + 1) * c)
        ext_scr[s, pad:pad + c, :] = xbc_ref[rows, :].astype(F32)
        acc = ext_scr[s, pad - hist:pad - hist + c, :] * cw_ref[0:1, :]
        for j in range(1, CONV_K):
            acc = acc + ext_scr[s, pad - hist + j:pad - hist + j + c, :] * cw_ref[j:j + 1, :]
        conv = jax.nn.silu(cb_ref[...] + acc)

        @pl.when(ci == nc - 1)
        def _():
            new_conv[s] = ext_scr[s, pad + c - hist:pad + c, :]

        ext_scr[s, 0:pad, :] = ext_scr[s, c:c + pad, :]

        xs = conv[:, :SSD_W]
        bm = conv[:, SSD_W:SSD_W + G_SSD * N_SSD]
        cm = conv[:, SSD_W + G_SSD * N_SSD:]
        dt = jax.nn.softplus(dt_ref[rows, :] + dtb_ref[...])
        dta = dt * a_ref[...]
        cum = sum(_dot(tri, p) for p in _split3(dta))
        cum_parts = _split3(cum)
        cum_t = sum(_dot_nt(eye, p) for p in cum_parts)
        cum_e = sum(_dot(p, expand) for p in cum_parts)
        dt_e = sum(_dot(p, expand) for p in _split3(dt))
        last_e = cum_e[c - 1:c, :]
        xdt = xs * dt_e
        xw = (xs * (jnp.exp(last_e - cum_e) * dt_e)).astype(BF16)
        x_l = jnp.where(left_half, xdt, 0.0).astype(BF16)
        x_r = jnp.where(left_half, 0.0, xdt).astype(BF16)

        y_parts, off_parts = [], []
        for g in range(G_SSD):
            bm_g = bm[:, g * N_SSD:(g + 1) * N_SSD].astype(BF16)
            cm_g = cm[:, g * N_SSD:(g + 1) * N_SSD].astype(BF16)
            cb = _dot_nt(cm_g, bm_g)
            state = st_scr[s, g]
            off_parts.append(_dot_nt(cm_g, state.astype(BF16)))
            decay = jnp.exp(jnp.sum(sel_ref[g] * cum[c - 1:c, :], axis=1, keepdims=True))
            st_scr[s, g] = decay * state + _dot_tn(xw[:, g * GRP_W:(g + 1) * GRP_W], bm_g)
            for jj in range(R_SSD // 2):
                j = g * (R_SSD // 2) + jj
                w = []
                for hd in (2 * j, 2 * j + 1):
                    seg = cum[:, hd:hd + 1] - cum_t[hd:hd + 1, :]
                    w.append((jnp.exp(jnp.where(causal, seg, -jnp.inf)) * cb).astype(BF16))
                cols = slice(j * LANE, (j + 1) * LANE)
                y_parts.append(_dot(w[0], x_l[:, cols]) + _dot(w[1], x_r[:, cols]))
        y = jnp.concatenate(y_parts, axis=1) + jnp.exp(cum_e) * jnp.concatenate(off_parts, axis=1)
        y = (y + dsk_ref[...] * xs) * jax.nn.silu(z_ref[rows, :].astype(F32))
        normed = []
        for g in range(G_SSD):
            yg = y[:, g * GRP_W:(g + 1) * GRP_W]
            normed.append(yg * lax.rsqrt(jnp.mean(yg * yg, axis=-1, keepdims=True) + EPS))
        y_ref[rows, :] = (jnp.concatenate(normed, axis=1) * ng_ref[...]).astype(y_ref.dtype)

    @pl.when(ci == nc - 1)
    def _():
        for l in range(n_prev):
            nconv_ref[l] = prev_conv[l][...]
            nssm_ref[l] = prev_ssm[l][...]
        for s in range(nb):
            for g in range(G_SSD):
                for r in range(R_SSD):
                    new_ssm[s, g * R_SSD + r] = st_scr[s, g, r * P_SSD:(r + 1) * P_SSD, :]


def _ssd(xbc, z, dt, lp, states, layer, prev, batch, length):
    c = math.gcd(length, CHUNK)
    nc = length // c
    nb = 1 if nc > 1 else SAMPLE_SEQS_PER_STEP
    assert batch % nb == 0 and c % SUBLANE == 0
    has_state = states is not None
    t = batch * length
    rows = nb * c
    tri = (jnp.arange(c)[:, None] >= jnp.arange(c)[None, :]).astype(BF16)
    eye = jnp.eye(LANE, dtype=BF16)
    head_of_ch = jnp.arange(SSD_W) // P_SSD
    expand = (jnp.arange(LANE)[:, None] == head_of_ch[None, :]).astype(BF16)
    sel = (head_of_ch[:, None] == jnp.arange(LANE)[None, :]).astype(F32).reshape(G_SSD, GRP_W, LANE)

    seq = lambda shape: pl.BlockSpec((nb,) + shape, lambda b, ci: (b,) + (0,) * len(shape))
    rowb = lambda w: pl.BlockSpec((rows, w), lambda b, ci: (b * nc + ci, 0))
    in_specs = [rowb(CONV_DIM), rowb(SSD_W), rowb(LANE),
                _full((CONV_K, CONV_DIM)), _full((1, CONV_DIM)), _full((1, LANE)), _full((1, LANE)),
                _full((1, SSD_W)), _full((1, SSD_W)),
                _full((c, c)), _full((LANE, LANE)), _full((LANE, SSD_W)), _full((G_SSD, GRP_W, LANE))]
    args = [xbc, z, dt, lp["conv_w"], lp["conv_b"], lp["dt_bias"], lp["a"], lp["d_skip"], lp["ssd_norm_g"],
            tri, eye, expand, sel]
    conv_shape, ssm_shape = (CONV_K - 1, CONV_DIM), (H_SSD, P_SSD, N_SSD)
    if has_state:
        stacked = lambda shape: pl.BlockSpec((None, nb) + shape, lambda b, ci: (layer, b) + (0,) * len(shape))
        in_specs += [stacked(conv_shape), stacked(ssm_shape)]
        args += list(states)
    in_specs += [seq(conv_shape)] * len(prev) + [seq(ssm_shape)] * len(prev)
    args += [p[0] for p in prev] + [p[1] for p in prev]
    block_bytes = (_nbytes((rows, CONV_DIM + 2 * SSD_W + LANE), F32) + _nbytes((nb, c + SUBLANE, CONV_DIM), F32)
                   + (3 + 4 * len(prev)) * _nbytes((nb, SSD_W, LANE), F32) + _nbytes((LANE, SSD_W), F32)
                   + 12 * _nbytes((c, SSD_W), F32))
    nconv_spec, nconv_shape = _state_out(prev, batch, nb, conv_shape)
    nssm_spec, nssm_shape = _state_out(prev, batch, nb, ssm_shape)
    return pl.pallas_call(
        functools.partial(_ssd_body, c, nb, nc, has_state, len(prev)),
        grid=(batch // nb, nc), in_specs=in_specs,
        out_specs=[rowb(SSD_W), nconv_spec, nssm_spec],
        out_shape=[jax.ShapeDtypeStruct((t, SSD_W), BF16), nconv_shape, nssm_shape],
        scratch_shapes=[pltpu.VMEM((nb, c + SUBLANE, CONV_DIM), F32),
                        pltpu.VMEM((nb, G_SSD, GRP_W, N_SSD), F32)],
        compiler_params=_params(("arbitrary", "arbitrary"), block_bytes), name="ssd",
    )(*args)


def _first_argmax(vals):
    best = vals[0]
    for v in vals[1:]:
        best = jnp.maximum(best, v)
    idx = jnp.full(best.shape, len(vals) - 1, jnp.int32)
    for i in range(len(vals) - 2, -1, -1):
        idx = jnp.where(vals[i] == best, i, idx)
    return best, idx


def _merge_body(bounds, *refs):
    n_groups = len(bounds)
    acts, refs = refs[:4 * n_groups], refs[4 * n_groups:]
    weights, refs = refs[:7], refs[7:]
    x2_refs, shared = refs[:n_groups], refs[n_groups:]
    i = pl.program_id(0)
    for gi, (first, count) in enumerate(bounds):
        @pl.when((i >= first) & (i < first + count))
        def _():
            _merge_tile(*acts[4 * gi:4 * gi + 4], *weights, x2_refs[gi], *shared)


def _merge_tile(ret_ref, ssd_ref, gates_ref, x_ref, wr_ref, ws_ref, wo_ref, g2_ref, rhi_ref, rlo_ref, upp_ref,
                x2_ref, hw_ref, route_ref, cnt_ref):
    ret_y = _dot(ret_ref[...], wr_ref[...])
    ssd_y = _dot(ssd_ref[...], ws_ref[...])
    gates = gates_ref[...].astype(F32)
    m = jax.nn.sigmoid(gates[:, :D_MODEL]) * ret_y + jax.nn.sigmoid(gates[:, D_MODEL:]) * ssd_y
    x2 = x_ref[...] + _dot(m.astype(BF16), wo_ref[...])
    x2_ref[...] = x2
    h2 = _rmsnorm(x2, g2_ref[...])
    bits = pltpu.bitcast(h2.astype(BF16).astype(F32), jnp.uint32)
    half = D_MODEL // 2
    word = lax.bitwise_or(lax.bitwise_and(bits[:, half:], jnp.uint32(HIGH_HALF)),
                          lax.shift_right_logical(bits[:, :half], jnp.uint32(16)))
    n_rows = h2.shape[0]
    for k in range(PACKED_ROWS):
        hw_ref[pl.ds(k, n_rows, stride=SUBLANE), :] = word[:, k * LANE:(k + 1) * LANE]

    h_hi = h2.astype(BF16)
    h_lo = (h2 - h_hi.astype(F32)).astype(BF16)
    logits = _dot_nt(rhi_ref[...], h_hi) + (_dot_nt(rlo_ref[...], h_hi) + _dot_nt(rhi_ref[...], h_lo))
    colv = lambda i: logits[i:i + 1, :]
    glog = [colv(i) for i in range(N_EGROUPS)]
    gmax, gidx = _first_argmax(glog)
    den = glog[0] * 0.0
    for v in glog:
        den = den + jnp.exp(v - gmax)
    gw = 1.0 / den
    sel = []
    for j in range(EXP_PER_GROUP):
        v = colv(N_EGROUPS + (N_EGROUPS - 1) * EXP_PER_GROUP + j)
        for g in range(N_EGROUPS - 2, -1, -1):
            v = jnp.where(gidx == g, colv(N_EGROUPS + g * EXP_PER_GROUP + j), v)
        sel.append(v)
    v1, i1 = _first_argmax(sel)
    v2, i2 = _first_argmax([jnp.where(i1 == j, -jnp.inf, sel[j]) for j in range(EXP_PER_GROUP)])
    e = jnp.exp(v2 - v1)
    w1 = gw / (1.0 + e)
    w2 = gw * e / (1.0 + e)
    first_low = i1 < i2
    lo = jnp.where(first_low, i1, i2)
    hi = jnp.where(first_low, i2, i1)
    pair = jnp.where(lo == 0, hi - 1, jnp.where(lo == 1, hi + 1, N_PAIRS - 1))
    cls = gidx * N_PAIRS + pair
    w_lo = jnp.where(first_low, w1, w2)
    w_hi = jnp.where(first_low, w2, w1)

    sub = lax.broadcasted_iota(jnp.int32, (META_W, n_rows), 0)
    onehot = jnp.where(sub == cls, 1.0, 0.0).astype(BF16)
    before = _dot(onehot, upp_ref[...])
    rank = jnp.sum(onehot.astype(F32) * before, axis=0, keepdims=True)
    meta = jnp.where(sub == 0, w_lo, jnp.where(sub == 1, w_hi, jnp.where(
        sub == 2, cls.astype(F32), jnp.where(sub == 3, rank, 0.0))))
    route_ref[0] = meta[0:SUBLANE, :]
    meta_bits = pltpu.bitcast(meta.T, jnp.uint32)
    for k in range(PACKED_ROWS, TOKEN_TILE_ROWS):
        hw_ref[pl.ds(k, n_rows, stride=SUBLANE), :] = meta_bits
    cnt_ref[0] = _dot_nt(jnp.ones((SUBLANE, n_rows), BF16), onehot)[0:1, :]


def _merge(group_acts, lp, big, layer):
    counts = [a[3].shape[0] // ROW_TILE for a in group_acts]
    bounds = tuple((sum(counts[:gi]), counts[gi]) for gi in range(len(counts)))
    n_tiles = sum(counts)
    total_rows = n_tiles * ROW_TILE
    upp = (jnp.arange(ROW_TILE)[:, None] < jnp.arange(ROW_TILE)[None, :]).astype(BF16)
    stacked = [big["w_ret_out"], big["w_ssd_out"], big["w_o"]]
    ws = [lp["ln2_g"], lp["router_hi"], lp["router_lo"], upp]
    block_bytes = (sum(_nbytes(w.shape[1:], w.dtype) for w in stacked) + sum(_nbytes(w.shape, w.dtype) for w in ws)
                   + len(counts) * (_nbytes((ROW_TILE, RET_W + SSD_W), BF16) + _nbytes((ROW_TILE, 4 * D_MODEL), F32))
                   + _nbytes((ROW_TILE, 3 * D_MODEL), F32))

    def group_row(width, first, count):
        return pl.BlockSpec((ROW_TILE, width), lambda i: (jnp.clip(i - first, 0, count - 1), 0))

    in_specs, args, x2_specs, x2_shapes = [], [], [], []
    for (first, count), acts in zip(bounds, group_acts, strict=True):
        in_specs += [group_row(w, first, count) for w in (RET_W, SSD_W, 2 * D_MODEL, D_MODEL)]
        args += list(acts)
        x2_specs.append(group_row(D_MODEL, first, count))
        x2_shapes.append(jax.ShapeDtypeStruct((count * ROW_TILE, D_MODEL), F32))
    in_specs += [_layer_block(w, layer) for w in stacked] + [_full(w.shape) for w in ws]
    args += stacked + ws
    outs = pl.pallas_call(
        functools.partial(_merge_body, bounds), grid=(n_tiles,), in_specs=in_specs,
        out_specs=x2_specs + [_token_spec(0), pl.BlockSpec((1, SUBLANE, ROW_TILE), lambda i: (i, 0, 0)),
                              pl.BlockSpec((1, 1, META_W), lambda i: (i, 0, 0))],
        out_shape=x2_shapes + [jax.ShapeDtypeStruct((total_rows * TOKEN_TILE_ROWS, LANE), jnp.uint32),
                               jax.ShapeDtypeStruct((n_tiles, SUBLANE, ROW_TILE), F32),
                               jax.ShapeDtypeStruct((n_tiles, 1, META_W), F32)],
        compiler_params=_params(("arbitrary",), block_bytes), name="merge_router",
    )(*args)
    return outs[:len(counts)], outs[len(counts):]


PLAN_ROWS = LANE
TAB_ROWS = SUBLANE
TAB_LO, TAB_HI, TAB_USED = 0, 1, 2
BYTE = 256.0
MOE_RING = 3


def _row_of(col_vals):
    pick = (lax.broadcasted_iota(jnp.int32, (TAB_ROWS, LANE), 0)
            == lax.broadcasted_iota(jnp.int32, (TAB_ROWS, LANE), 1)).astype(BF16)
    hi = jnp.floor(col_vals * (1.0 / BYTE))
    lo = col_vals - hi * BYTE
    return BYTE * _dot_nt(pick, hi.astype(BF16)) + _dot_nt(pick, lo.astype(BF16))


def _plan_body(tile, n_row_tiles, route_ref, cnt_ref, low_ref, upp_ref, pos_ref, tab_ref, offs_scr):
    cnt = cnt_ref[...]
    tot = jnp.sum(cnt, axis=0, keepdims=True)
    cls_tiles = jnp.floor((tot + (tile - 1.0)) * (1.0 / tile))
    first_tile = _dot(jnp.broadcast_to(cls_tiles, (SUBLANE, LANE)).astype(BF16), upp_ref[...])[0:1, :]
    base = first_tile * tile
    ends = base + cls_tiles * tile
    offs_scr[...] = base + _dot(low_ref[...], cnt.astype(BF16))

    def per_row_tile(j, carry):
        meta = route_ref[j]
        sub = lax.broadcasted_iota(jnp.int32, (LANE, ROW_TILE), 0).astype(F32)
        onehot = jnp.where(sub == meta[2:3, :], 1.0, 0.0).astype(BF16)
        offs = offs_scr[pl.ds(j, 1), :]
        hi = jnp.floor(offs * (1.0 / BYTE))
        rows8 = lambda v: jnp.broadcast_to(v, (SUBLANE, LANE)).astype(BF16)
        pos = BYTE * _dot(rows8(hi), onehot) + _dot(rows8(offs - hi * BYTE), onehot)
        pos_ref[j] = (pos[0:1, :] + meta[3:4, :]).astype(jnp.int32)
        return carry

    lax.fori_loop(0, n_row_tiles, per_row_tile, 0)

    lane = lax.broadcasted_iota(jnp.int32, (PLAN_ROWS, LANE), 1)
    tile_idx = lax.broadcasted_iota(jnp.int32, (PLAN_ROWS, LANE), 0).astype(F32)[:, 0:1]
    n_used = jnp.sum(cls_tiles, axis=1, keepdims=True)
    start = jnp.minimum(tile_idx, n_used - 1.0) * tile
    tcls = jnp.sum(jnp.where((lane < N_CLASSES) & (ends <= start), 1.0, 0.0), axis=1, keepdims=True)
    tcls = jnp.minimum(tcls, N_CLASSES - 1.0)
    group = sum(jnp.where(tcls >= k * N_PAIRS, 1.0, 0.0) for k in range(1, N_EGROUPS))
    pair = tcls - group * N_PAIRS
    p_lo = jnp.where(pair >= 3, 1.0, 0.0) + jnp.where(pair >= 5, 1.0, 0.0)
    p_hi = jnp.where(pair == 0, 1.0, jnp.where((pair == 1) | (pair == 3), 2.0, 3.0))
    table = jnp.where(lane == TAB_LO, group * EXP_PER_GROUP + p_lo,
                      jnp.where(lane == TAB_HI, group * EXP_PER_GROUP + p_hi,
                                jnp.where(lane == TAB_USED, n_used, 0.0)))
    tab_ref[...] = _row_of(table).astype(jnp.int32)


def _plan(route, counts, tile):
    nrt = route.shape[0]
    t = nrt * ROW_TILE
    assert nrt <= PLAN_ROWS and t // tile + N_CLASSES <= PLAN_ROWS
    cnt = jnp.pad(counts[:, 0, :], ((0, PLAN_ROWS - nrt), (0, 0)))
    idx = jnp.arange(PLAN_ROWS)
    low = (idx[:, None] > idx[None, :]).astype(BF16)
    upp = (idx[:, None] < idx[None, :]).astype(BF16)
    pos, tab = pl.pallas_call(
        functools.partial(_plan_body, tile, nrt),
        out_shape=[jax.ShapeDtypeStruct((nrt, 1, ROW_TILE), jnp.int32),
                   jax.ShapeDtypeStruct((TAB_ROWS, PLAN_ROWS), jnp.int32)],
        scratch_shapes=[pltpu.VMEM((PLAN_ROWS, LANE), F32)],
        compiler_params=pltpu.CompilerParams(vmem_limit_bytes=_vmem_limit(_nbytes(route.shape, F32))),
        name="moe_plan",
    )(route, cnt, low, upp)
    return pos.reshape(t), tab


def _moe_body(tile, n_tok, pos_ref, tab_ref, hw_hbm, spare_hbm, wgl_ref, wul_ref, wdl_ref, wgh_ref, wuh_ref,
              wdh_ref, y_hbm, buf, obuf, tok_smem, gsem, ssem, isem):
    i = pl.program_id(0)
    n_used = tab_ref[TAB_USED, 0]
    ring = buf.shape[0]
    cur = lax.rem(i, ring)
    nxt1 = lax.rem(i + 1, ring)
    nxt2 = lax.rem(i + 2, ring)
    n_points = 16
    per_point = tile // n_points
    tr = TOKEN_TILE_ROWS

    used = PACKED_ROWS + 1

    def token_tile(ref, token, rows=tr):
        start = token * tr if isinstance(token, int) else pl.multiple_of(token * tr, tr)
        return ref.at[pl.ds(start, rows)]

    def dma_thread(r):
        return r % 2 if isinstance(r, int) else 0

    def gather_start(block, r, sl):
        src = jnp.minimum(tok_smem[block * tile + r], n_tok - 1)
        pltpu.make_async_copy(token_tile(hw_hbm, src, used), token_tile(buf.at[sl], r, used),
                              gsem.at[sl]).start(dma_thread(r))

    def scatter_start(block, r, sl):
        dst = tok_smem[block * tile + r]
        pltpu.make_async_copy(token_tile(obuf.at[sl], r), token_tile(y_hbm, dst), ssem.at[sl]).start(dma_thread(r))

    def gather_wait(sl):
        pltpu.make_async_copy(hw_hbm.at[pl.ds(0, tile * used)], buf.at[sl, pl.ds(0, tile * used)],
                              gsem.at[sl]).wait()

    def scatter_wait(sl):
        pltpu.make_async_copy(obuf.at[sl], y_hbm.at[pl.ds(0, tile * tr)], ssem.at[sl]).wait()

    def for_rows(n, fn):
        lax.fori_loop(0, n, lambda r, carry: (fn(r), carry)[1], 0, unroll=8)

    def block_of(t):
        return jnp.minimum(t, n_used - 1) + 1

    @pl.when(i < n_used)
    def _():
        @pl.when(i == 0)
        def _():
            fill = pltpu.make_async_copy(spare_hbm, tok_smem, isem)
            fill.start()
            fill.wait()

            def place(t):
                tok_smem[tile + pos_ref[t]] = t
            for_rows(n_tok, place)
            obuf[ring - 1] = jnp.zeros(obuf.shape[1:], F32)
            for_rows(tile, lambda r: gather_start(block_of(0), r, 0))
            for_rows(tile, lambda r: gather_start(block_of(1), r, 1))

        gather_wait(cur)
        issued = [0]

        def issue_point():
            for r in range(issued[0] * per_point, (issued[0] + 1) * per_point):
                gather_start(block_of(i + 2), r, nxt2)
                scatter_start(i, r, nxt2)
            issued[0] += 1

        def tile_row(k):
            return buf[cur, pl.ds(k, tile, stride=tr), :]

        words = [tile_row(k) for k in range(PACKED_ROWS)]
        low = [pltpu.bitcast(lax.shift_left(w, jnp.uint32(16)), F32).astype(BF16) for w in words]
        high = [pltpu.bitcast(lax.bitwise_and(w, jnp.uint32(HIGH_HALF)), F32).astype(BF16) for w in words]
        xb = jnp.concatenate(low + high, axis=1)
        meta = pltpu.bitcast(tile_row(PACKED_ROWS), F32)
        w_lo, w_hi = meta[:, 0:1], meta[:, 1:2]
        half = D_FF // 2
        acts = []
        for wg_ref, wu_ref in ((wgl_ref, wul_ref), (wgh_ref, wuh_ref)):
            parts = []
            for n in range(2):
                gate = _dot(xb, wg_ref[:, n * half:(n + 1) * half])
                issue_point()
                up = _dot(xb, wu_ref[:, n * half:(n + 1) * half])
                issue_point()
                parts.append((jax.nn.silu(gate) * up).astype(BF16))
            acts.append(jnp.concatenate(parts, axis=1))
        quarter = D_MODEL // 4
        for n in range(4):
            cols = slice(n * quarter, (n + 1) * quarter)
            d_lo = _dot(acts[0], wdl_ref[:, cols])
            issue_point()
            d_hi = _dot(acts[1], wdh_ref[:, cols])
            issue_point()
            out = d_lo * w_lo + d_hi * w_hi
            for kk in range(quarter // LANE):
                obuf[cur, pl.ds(n * (quarter // LANE) + kk, tile, stride=tr), :] = out[:, kk * LANE:(kk + 1) * LANE]
        assert issued[0] == n_points

        @pl.when(i >= 1)
        def _():
            scatter_wait(nxt1)

        @pl.when(i == n_used - 1)
        def _():
            scatter_wait(nxt2)
            for_rows(tile, lambda r: scatter_start(i + 1, r, cur))
            scatter_wait(cur)
            gather_wait(nxt1)
            gather_wait(nxt2)


def _moe(hw, pos, tab, big, layer, tile):
    t = hw.shape[0] // TOKEN_TILE_ROWS
    n_tiles = t // tile + N_CLASSES
    wspec = lambda shape, row: pl.BlockSpec((None, None) + shape, lambda i, pos, tab: (layer, tab[row, i], 0, 0))
    in_specs = [pl.BlockSpec(memory_space=pl.ANY)] * 2
    spare = t + jnp.arange((n_tiles + 1) * tile, dtype=jnp.int32) % tile
    for row in (TAB_LO, TAB_HI):
        in_specs += [wspec((D_MODEL, D_FF), row), wspec((D_MODEL, D_FF), row), wspec((D_FF, D_MODEL), row)]
    block_bytes = 6 * _nbytes((D_MODEL, D_FF), BF16) + 6 * _nbytes((tile, D_MODEL), F32)
    tile_rows = tile * TOKEN_TILE_ROWS
    grid_spec = pltpu.PrefetchScalarGridSpec(
        num_scalar_prefetch=2, grid=(n_tiles,), in_specs=in_specs,
        out_specs=pl.BlockSpec(memory_space=pl.ANY),
        scratch_shapes=[pltpu.VMEM((MOE_RING, tile_rows, LANE), jnp.uint32),
                        pltpu.VMEM((MOE_RING, tile_rows, LANE), F32),
                        pltpu.SMEM(((n_tiles + 1) * tile,), jnp.int32),
                        pltpu.SemaphoreType.DMA((MOE_RING,)), pltpu.SemaphoreType.DMA((MOE_RING,)),
                        pltpu.SemaphoreType.DMA(())])
    return pl.pallas_call(
        functools.partial(_moe_body, tile, t), grid_spec=grid_spec,
        out_shape=jax.ShapeDtypeStruct(((t + tile) * TOKEN_TILE_ROWS, LANE), F32),
        compiler_params=_params(("arbitrary",), block_bytes), name="moe",
    )(pos, tab, hw, spare, big["w_gate"], big["w_up"], big["w_down"], big["w_gate"], big["w_up"], big["w_down"])


FINAL_TILE = 1024


def _final_body(x_ref, y_ref, g_ref, o_ref):
    o_ref[...] = _rmsnorm(x_ref[...] + _token_rows(y_ref, FINAL_TILE), g_ref[...])


def _final_norm(x, y, y_offset, g):
    t = x.shape[0]
    row = pl.BlockSpec((FINAL_TILE, D_MODEL), lambda i: (i, 0))
    return pl.pallas_call(
        _final_body, grid=(t // FINAL_TILE,),
        in_specs=[row, _token_spec(y_offset, FINAL_TILE), _full((1, D_MODEL))],
        out_specs=row, out_shape=jax.ShapeDtypeStruct((t, D_MODEL), F32),
        compiler_params=_params(("arbitrary",), 3 * _nbytes((FINAL_TILE, D_MODEL), F32)), name="final_norm",
    )(x, y, g)


def _layer_params(i, ln1_g, ret_gn_g, conv_w, conv_b, dt_bias, a_log, d_skip, ssd_norm_g, ln2_g, w_rg, w_re):
    pad_heads = lambda v: jnp.pad(v.astype(F32), (0, LANE - H_SSD)).reshape(1, LANE)
    router = jnp.pad(jnp.concatenate([w_rg[i], w_re[i]], axis=1),
                     ((0, 0), (0, META_W - N_EGROUPS - N_EXPERTS)))
    router_hi = router.astype(BF16)
    return {
        "ln1_g": ln1_g[i].reshape(1, D_MODEL),
        "ret_gn_g": ret_gn_g[i].reshape(1, RET_W),
        "conv_w": conv_w[i], "conv_b": conv_b[i].reshape(1, CONV_DIM),
        "dt_bias": pad_heads(dt_bias[i]), "a": pad_heads(-jnp.exp(a_log[i].astype(F32))),
        "d_skip": jnp.repeat(d_skip[i].astype(F32), P_SSD).reshape(1, SSD_W),
        "ssd_norm_g": ssd_norm_g[i].reshape(1, SSD_W),
        "ln2_g": ln2_g[i].reshape(1, D_MODEL),
        "router_hi": router_hi.T, "router_lo": (router - router_hi.astype(F32)).astype(BF16).T,
    }


def _stacked_params(w_in, w_ret_out, w_ssd_out, w_o, w_gate, w_up, w_down):
    offs = [0]
    for s in IN_SIZES:
        offs.append(offs[-1] + s)
    return {
        "w_in": [w_in.astype(BF16),
                 jnp.pad(w_in[:, :, offs[6]:offs[7]], ((0, 0), (0, 0), (0, LANE - H_SSD))).astype(BF16),
                 w_in[:, :, offs[7]:offs[9]].astype(BF16)],
        "w_ret_out": w_ret_out.astype(BF16), "w_ssd_out": w_ssd_out.astype(BF16), "w_o": w_o.astype(BF16),
        "w_gate": w_gate.astype(BF16), "w_up": w_up.astype(BF16), "w_down": w_down.astype(BF16),
    }


class _Group:
    def __init__(self, x, start, states, row_offset):
        self.batch, self.length, _ = x.shape
        self.x = x.reshape(self.batch * self.length, D_MODEL)
        self.start, self.states, self.row_offset = start, states, row_offset
        self.rets, self.ssms, self.convs = [], [], []


def kernel(x_prompt, x_sample, state_ret, state_ssm, state_conv, ln1_g, w_in, ret_gn_g, w_ret_out, conv_w, conv_b, dt_bias, a_log, d_skip, ssd_norm_g, w_ssd_out, w_o, ln2_g, w_router_group, w_router_expert, w_e_gate, w_e_up, w_e_down, lnf_g):
    depth = w_in.shape[0]
    past_len = 16384.0
    layers = [_layer_params(i, ln1_g, ret_gn_g, conv_w, conv_b, dt_bias, a_log, d_skip, ssd_norm_g, ln2_g,
                            w_router_group, w_router_expert) for i in range(depth)]
    big = _stacked_params(w_in, w_ret_out, w_ssd_out, w_o, w_e_gate, w_e_up, w_e_down)
    n_prompt = x_prompt.shape[0] * x_prompt.shape[1]
    groups = [_Group(x_prompt, 0.0, None, 0),
              _Group(x_sample, past_len, (state_ret, state_ssm, state_conv), n_prompt)]
    y = None
    for i, lp in enumerate(layers):
        group_acts = []
        last = i == depth - 1
        for grp in groups:
            outs = _inproj(grp.x, y, grp.row_offset, lp["ln1_g"], big["w_in"], i)
            if y is not None:
                grp.x, outs = outs[0], outs[1:]
            qkvr, z, xbc, dt, gates = outs
            ret_state = None if grp.states is None else grp.states[0]
            ssd_state = None if grp.states is None else (grp.states[2], grp.states[1])
            ret, new_ret = _retention(qkvr, grp.start, lp["ret_gn_g"], ret_state, i,
                                      grp.rets if last else (), grp.batch, grp.length)
            ssd, new_conv, new_ssm = _ssd(xbc, z, dt, lp, ssd_state, i,
                                          list(zip(grp.convs, grp.ssms)) if last else (), grp.batch, grp.length)
            grp.rets.append(new_ret)
            grp.ssms.append(new_ssm)
            grp.convs.append(new_conv)
            group_acts.append((ret, ssd, gates, grp.x))
        new_x, (hw, route, counts) = _merge(group_acts, lp, big, i)
        for grp, x2 in zip(groups, new_x, strict=True):
            grp.x = x2
        pos, tab = _plan(route, counts, ROW_TILE)
        y = _moe(hw, pos, tab, big, i, ROW_TILE)
    outs = []
    for grp in groups:
        out = _final_norm(grp.x, y, grp.row_offset, lnf_g.reshape(1, D_MODEL))
        outs.append(out.reshape(grp.batch, grp.length, D_MODEL))
    states = []
    for grp in groups:
        states += [grp.rets[-1], grp.ssms[-1], grp.convs[-1]]
    return tuple(outs + states)
```

```python
import functools
import math

import jax
import jax.numpy as jnp
from jax import lax
from jax.experimental import pallas as pl
from jax.experimental.pallas import tpu as pltpu

F32 = jnp.float32
BF16 = jnp.bfloat16

D_MODEL = 1024
H_RET, DK_RET, DV_RET = 4, 128, 128
RET_W = H_RET * DV_RET
ROPE_BASE = 10000.0
H_SSD, P_SSD, G_SSD, N_SSD = 16, 64, 2, 64
R_SSD = H_SSD // G_SSD
SSD_W = H_SSD * P_SSD
GRP_W = SSD_W // G_SSD
CONV_K = 4
CONV_DIM = SSD_W + 2 * G_SSD * N_SSD
CHUNK = 128
N_EGROUPS, EXP_PER_GROUP = 4, 4
N_EXPERTS = N_EGROUPS * EXP_PER_GROUP
N_PAIRS = EXP_PER_GROUP * (EXP_PER_GROUP - 1) // 2
N_CLASSES = N_EGROUPS * N_PAIRS
D_FF = 512
EPS = 1e-6
IN_SIZES = [RET_W, RET_W, RET_W, RET_W, SSD_W, CONV_DIM, H_SSD, D_MODEL, D_MODEL]

LANE = 128
SUBLANE = 8
MIB = 1024 * 1024

ROW_TILE = 256
META_W = LANE
TOKEN_TILE_ROWS = D_MODEL // LANE
assert TOKEN_TILE_ROWS == SUBLANE
PACKED_ROWS = D_MODEL // (2 * LANE)
HIGH_HALF = 0xFFFF0000
SAMPLE_SEQS_PER_STEP = 8
PROMPT_SEQS_PER_STEP = 4


def _vmem_limit(block_bytes):
    return int(min(2 * block_bytes + 16 * MIB, 56 * MIB))


def _params(semantics, block_bytes):
    return pltpu.CompilerParams(dimension_semantics=semantics,
                                vmem_limit_bytes=_vmem_limit(block_bytes))


def _nbytes(shape, dtype):
    return math.prod(shape) * jnp.dtype(dtype).itemsize


def _full(shape):
    return pl.BlockSpec(shape, lambda *_: (0,) * len(shape))


def _dot(a, b):
    return jnp.dot(a, b, preferred_element_type=F32)


def _dot_nt(a, b):
    return lax.dot_general(a, b, (((1,), (1,)), ((), ())), preferred_element_type=F32)


def _dot_tn(a, b):
    return lax.dot_general(a, b, (((0,), (0,)), ((), ())), preferred_element_type=F32)


def _split3(a):
    hi = a.astype(BF16)
    r1 = a - hi.astype(F32)
    mid = r1.astype(BF16)
    lo = (r1 - mid.astype(F32)).astype(BF16)
    return hi, mid, lo


def _rmsnorm(x, g):
    r = lax.rsqrt(jnp.mean(x * x, axis=-1, keepdims=True) + EPS)
    return x * r * g


def _token_rows(ref, rows):
    return jnp.concatenate([ref[pl.ds(k, rows, stride=SUBLANE), :] for k in range(TOKEN_TILE_ROWS)], axis=1)


def _token_spec(row_offset, rows=ROW_TILE):
    assert row_offset % rows == 0
    first = row_offset // rows
    return pl.BlockSpec((rows * TOKEN_TILE_ROWS, LANE), lambda i: (i + first, 0))


def _inproj_body(has_add, *refs):
    n_in = 6 if has_add else 5
    ins, outs = refs[:n_in], refs[n_in:]
    if has_add:
        x_ref, y_ref, g_ref = ins[:3]
        x = x_ref[...] + _token_rows(y_ref, ROW_TILE)
        outs[0][...] = x
        outs = outs[1:]
    else:
        x_ref, g_ref = ins[:2]
        x = x_ref[...]
    w_main, w_dt, w_gates = ins[-3:]
    h = _rmsnorm(x, g_ref[...]).astype(BF16)
    col = 0
    for o_ref in outs[:3]:
        width = o_ref.shape[1]
        o_ref[...] = _dot(h, w_main[:, col:col + width])
        col += width
    outs[3][...] = _dot(h, w_dt[...])
    outs[4][...] = _dot(h, w_gates[...])


def _layer_block(arr, layer):
    shape = arr.shape[1:]
    return pl.BlockSpec((None,) + shape, lambda *_: (layer,) + (0,) * len(shape))


def _inproj(x, y, y_offset, g, ws, layer):
    t = x.shape[0]
    has_add = y is not None
    widths = [4 * RET_W, SSD_W, CONV_DIM, LANE, 2 * D_MODEL]
    assert sum(widths[:3]) == sum(IN_SIZES[:6])
    row = lambda w: pl.BlockSpec((ROW_TILE, w), lambda i: (i, 0))
    in_specs = [row(D_MODEL)] + ([_token_spec(y_offset)] if has_add else []) + [_full((1, D_MODEL))]
    in_specs += [_layer_block(w, layer) for w in ws]
    out_shape = [jax.ShapeDtypeStruct((t, w), F32) for w in widths]
    out_specs = [row(w) for w in widths]
    if has_add:
        out_shape = [jax.ShapeDtypeStruct((t, D_MODEL), F32)] + out_shape
        out_specs = [row(D_MODEL)] + out_specs
    block_bytes = (sum(_nbytes(w.shape[1:], BF16) for w in ws)
                   + _nbytes((ROW_TILE, sum(widths) + 3 * D_MODEL), F32))
    args = ([x, y] if has_add else [x]) + [g] + list(ws)
    return pl.pallas_call(
        functools.partial(_inproj_body, has_add),
        grid=(t // ROW_TILE,), in_specs=in_specs, out_specs=out_specs, out_shape=out_shape,
        compiler_params=_params(("arbitrary",), block_bytes), name="inproj",
    )(*args)


def _retention_body(c, nb, nc, has_state, n_prev, *refs):
    refs = list(refs)
    qkvr_ref, cos_ref, sin_ref, dmat_ref, qdec_ref, kdec_ref, cdec_ref, gn_ref = refs[:8]
    refs = refs[8:]
    if has_state:
        s0_ref = refs.pop(0)
    prev_refs = [refs.pop(0) for _ in range(n_prev)]
    o_ref, ns_ref, st_scr = refs
    ci = pl.program_id(1)

    @pl.when(ci == 0)
    def _():
        if has_state:
            st_scr[...] = s0_ref[...]
        else:
            st_scr[...] = jnp.zeros(st_scr.shape, F32)

    cosv, sinv = cos_ref[...], sin_ref[...]
    for s in range(nb):
        for h in range(H_RET):
            col = lambda j: slice(j * RET_W + h * DK_RET, j * RET_W + (h + 1) * DK_RET)
            q, k = qkvr_ref[s, :, col(0)], qkvr_ref[s, :, col(1)]
            v, rg = qkvr_ref[s, :, col(2)], qkvr_ref[s, :, col(3)]
            qr = q * cosv + pltpu.roll(q, DK_RET // 2, axis=1) * sinv
            kr = (k * cosv + pltpu.roll(k, DK_RET // 2, axis=1) * sinv) * (DK_RET ** -0.5)
            qb, kb, vb = qr.astype(BF16), kr.astype(BF16), v.astype(BF16)
            state = st_scr[s, h]
            sc = _dot_nt(qb, kb) * dmat_ref[h]
            o = _dot(sc.astype(BF16), vb) + _dot(qb, state.astype(BF16)) * qdec_ref[h]
            st_scr[s, h] = cdec_ref[h] * state + _dot_tn((kr * kdec_ref[h]).astype(BF16), vb)
            mu = jnp.mean(o, axis=-1, keepdims=True)
            d = o - mu
            var = jnp.mean(d * d, axis=-1, keepdims=True)
            on = d * lax.rsqrt(var + EPS) * gn_ref[:, h * DV_RET:(h + 1) * DV_RET]
            o_ref[s, :, h * DV_RET:(h + 1) * DV_RET] = (jax.nn.silu(rg) * on).astype(o_ref.dtype)

    @pl.when(ci == nc - 1)
    def _():
        if n_prev:
            for l, prev_ref in enumerate(prev_refs):
                ns_ref[l] = prev_ref[...]
            ns_ref[n_prev] = st_scr[...]
        else:
            ns_ref[...] = st_scr[...]


def _state_out(prev, batch, nb, shape):
    zeros = (0,) * len(shape)
    if prev:
        depth = len(prev) + 1
        return (pl.BlockSpec((depth, nb) + shape, lambda b, ci: (0, b) + zeros),
                jax.ShapeDtypeStruct((depth, batch) + shape, F32))
    return pl.BlockSpec((nb,) + shape, lambda b, ci: (b,) + zeros), jax.ShapeDtypeStruct((batch,) + shape, F32)


def _retention(qkvr, start, gn_g, states, layer, prev, batch, length):
    c = math.gcd(length, CHUNK)
    nc = length // c
    nb = PROMPT_SEQS_PER_STEP if nc > 1 else SAMPLE_SEQS_PER_STEP
    assert batch % nb == 0
    has_state = states is not None
    t = batch * length
    half = DK_RET // 2
    pos = start + jnp.arange(length, dtype=F32)
    inv = ROPE_BASE ** (-jnp.arange(half, dtype=F32) / half)
    ang = pos[:, None] * inv[None, :]
    cosv = jnp.concatenate([jnp.cos(ang), jnp.cos(ang)], axis=1)
    sinv = jnp.concatenate([-jnp.sin(ang), jnp.sin(ang)], axis=1)
    log_g = jnp.log1p(-jnp.exp2(-5.0 - jnp.arange(H_RET, dtype=F32)))
    idx = jnp.arange(c, dtype=F32)
    rel = idx[:, None] - idx[None, :]
    causal = rel >= 0
    dmat = jnp.where(causal[None], jnp.exp(jnp.where(causal, rel, 0.0)[None] * log_g[:, None, None]), 0.0)
    qdec = jnp.broadcast_to(jnp.exp((idx[None, :] + 1.0) * log_g[:, None])[:, :, None], (H_RET, c, LANE))
    kdec = jnp.broadcast_to(jnp.exp((c - 1.0 - idx[None, :]) * log_g[:, None])[:, :, None], (H_RET, c, LANE))
    cdec = jnp.exp(c * log_g)

    rows = nb * c
    state_spec = pl.BlockSpec((nb, H_RET, DK_RET, DV_RET), lambda b, ci: (b, 0, 0, 0))
    seq_rows = lambda w: pl.BlockSpec((nb, c, w), lambda b, ci: (b, ci, 0))
    in_specs = [
        seq_rows(4 * RET_W),
        pl.BlockSpec((c, LANE), lambda b, ci: (ci, 0)),
        pl.BlockSpec((c, LANE), lambda b, ci: (ci, 0)),
        _full((H_RET, c, c)), _full((H_RET, c, LANE)), _full((H_RET, c, LANE)),
        pl.BlockSpec(memory_space=pltpu.SMEM),
        _full((1, RET_W)),
    ]
    args = [qkvr.reshape(batch, length, 4 * RET_W), cosv, sinv, dmat, qdec, kdec, cdec, gn_g]
    if has_state:
        in_specs.append(pl.BlockSpec((None, nb, H_RET, DK_RET, DV_RET), lambda b, ci: (layer, b, 0, 0, 0)))
        args.append(states)
    in_specs += [state_spec] * len(prev)
    args += list(prev)
    block_bytes = (_nbytes((rows, 4 * RET_W + 2 * LANE + RET_W), F32)
                   + (3 + 2 * len(prev)) * _nbytes((nb, H_RET, DK_RET, DV_RET), F32)
                   + 3 * _nbytes((H_RET, c, LANE), F32))
    ns_spec, ns_shape = _state_out(prev, batch, nb, (H_RET, DK_RET, DV_RET))
    out, new_state = pl.pallas_call(
        functools.partial(_retention_body, c, nb, nc, has_state, len(prev)),
        grid=(batch // nb, nc), in_specs=in_specs,
        out_specs=[seq_rows(RET_W), ns_spec],
        out_shape=[jax.ShapeDtypeStruct((batch, length, RET_W), BF16), ns_shape],
        scratch_shapes=[pltpu.VMEM((nb, H_RET, DK_RET, DV_RET), F32)],
        compiler_params=_params(("arbitrary", "arbitrary"), block_bytes), name="retention",
    )(*args)
    return out.reshape(t, RET_W), new_state


def _ssd_body(c, nb, nc, has_state, n_prev, *refs):
    refs = list(refs)
    (xbc_ref, z_ref, dt_ref, cw_ref, cb_ref, dtb_ref, a_ref, dsk_ref, ng_ref,
     tri_ref, eye_ref, exp_ref, sel_ref) = refs[:13]
    refs = refs[13:]
    if has_state:
        sconv_ref, sssm_ref = refs.pop(0), refs.pop(0)
    prev_conv = [refs.pop(0) for _ in range(n_prev)]
    prev_ssm = [refs.pop(0) for _ in range(n_prev)]
    y_ref, nconv_ref, nssm_ref, ext_scr, st_scr = refs
    new_conv = nconv_ref.at[n_prev] if n_prev else nconv_ref
    new_ssm = nssm_ref.at[n_prev] if n_prev else nssm_ref
    ci = pl.program_id(1)
    pad = SUBLANE
    hist = CONV_K - 1

    @pl.when(ci == 0)
    def _():
        ext_scr[:, 0:pad, :] = jnp.zeros((nb, pad, CONV_DIM), F32)
        if has_state:
            ext_scr[:, pad - hist:pad, :] = sconv_ref[...]
            for s in range(nb):
                for g in range(G_SSD):
                    for r in range(R_SSD):
                        st_scr[s, g, r * P_SSD:(r + 1) * P_SSD, :] = sssm_ref[s, g * R_SSD + r]
        else:
            st_scr[...] = jnp.zeros(st_scr.shape, F32)

    causal = (lax.broadcasted_iota(jnp.int32, (c, c), 0) >= lax.broadcasted_iota(jnp.int32, (c, c), 1))
    left_half = (lax.broadcasted_iota(jnp.int32, (c, SSD_W), 1) % LANE) < P_SSD
    tri, eye, expand = tri_ref[...], eye_ref[...], exp_ref[...]

    for s in range(nb):
        rows = slice(s * c, (s + 1) * c)
        ext_scr[s, pad:pad + c, :] = xbc_ref[rows, :]
        acc = ext_scr[s, pad - hist:pad - hist + c, :] * cw_ref[0:1, :]
        for j in range(1, CONV_K):
            acc = acc + ext_scr[s, pad - hist + j:pad - hist + j + c, :] * cw_ref[j:j + 1, :]
        conv = jax.nn.silu(cb_ref[...] + acc)

        @pl.when(ci == nc - 1)
        def _():
            new_conv[s] = ext_scr[s, pad + c - hist:pad + c, :]

        ext_scr[s, 0:pad, :] = ext_scr[s, c:c + pad, :]

        xs = conv[:, :SSD_W]
        bm = conv[:, SSD_W:SSD_W + G_SSD * N_SSD]
        cm = conv[:, SSD_W + G_SSD * N_SSD:]
        dt = jax.nn.softplus(dt_ref[rows, :] + dtb_ref[...])
        dta = dt * a_ref[...]
        cum = sum(_dot(tri, p) for p in _split3(dta))
        cum_parts = _split3(cum)
        cum_t = sum(_dot_nt(eye, p) for p in cum_parts)
        cum_e = sum(_dot(p, expand) for p in cum_parts)
        dt_e = sum(_dot(p, expand) for p in _split3(dt))
        last_e = cum_e[c - 1:c, :]
        xdt = xs * dt_e
        xw = (xs * (jnp.exp(last_e - cum_e) * dt_e)).astype(BF16)
        x_l = jnp.where(left_half, xdt, 0.0).astype(BF16)
        x_r = jnp.where(left_half, 0.0, xdt).astype(BF16)

        y_parts, off_parts = [], []
        for g in range(G_SSD):
            bm_g = bm[:, g * N_SSD:(g + 1) * N_SSD].astype(BF16)
            cm_g = cm[:, g * N_SSD:(g + 1) * N_SSD].astype(BF16)
            cb = _dot_nt(cm_g, bm_g)
            state = st_scr[s, g]
            off_parts.append(_dot_nt(cm_g, state.astype(BF16)))
            decay = jnp.exp(jnp.sum(sel_ref[g] * cum[c - 1:c, :], axis=1, keepdims=True))
            st_scr[s, g] = decay * state + _dot_tn(xw[:, g * GRP_W:(g + 1) * GRP_W], bm_g)
            for jj in range(R_SSD // 2):
                j = g * (R_SSD // 2) + jj
                w = []
                for hd in (2 * j, 2 * j + 1):
                    seg = cum[:, hd:hd + 1] - cum_t[hd:hd + 1, :]
                    w.append((jnp.exp(jnp.where(causal, seg, -jnp.inf)) * cb).astype(BF16))
                cols = slice(j * LANE, (j + 1) * LANE)
                y_parts.append(_dot(w[0], x_l[:, cols]) + _dot(w[1], x_r[:, cols]))
        y = jnp.concatenate(y_parts, axis=1) + jnp.exp(cum_e) * jnp.concatenate(off_parts, axis=1)
        y = (y + dsk_ref[...] * xs) * jax.nn.silu(z_ref[rows, :])
        normed = []
        for g in range(G_SSD):
            yg = y[:, g * GRP_W:(g + 1) * GRP_W]
            normed.append(yg * lax.rsqrt(jnp.mean(yg * yg, axis=-1, keepdims=True) + EPS))
        y_ref[rows, :] = (jnp.concatenate(normed, axis=1) * ng_ref[...]).astype(y_ref.dtype)

    @pl.when(ci == nc - 1)
    def _():
        for l in range(n_prev):
            nconv_ref[l] = prev_conv[l][...]
            nssm_ref[l] = prev_ssm[l][...]
        for s in range(nb):
            for g in range(G_SSD):
                for r in range(R_SSD):
                    new_ssm[s, g * R_SSD + r] = st_scr[s, g, r * P_SSD:(r + 1) * P_SSD, :]


def _ssd(xbc, z, dt, lp, states, layer, prev, batch, length):
    c = math.gcd(length, CHUNK)
    nc = length // c
    nb = 1 if nc > 1 else SAMPLE_SEQS_PER_STEP
    assert batch % nb == 0 and c % SUBLANE == 0
    has_state = states is not None
    t = batch * length
    rows = nb * c
    tri = (jnp.arange(c)[:, None] >= jnp.arange(c)[None, :]).astype(BF16)
    eye = jnp.eye(LANE, dtype=BF16)
    head_of_ch = jnp.arange(SSD_W) // P_SSD
    expand = (jnp.arange(LANE)[:, None] == head_of_ch[None, :]).astype(BF16)
    sel = (head_of_ch[:, None] == jnp.arange(LANE)[None, :]).astype(F32).reshape(G_SSD, GRP_W, LANE)

    seq = lambda shape: pl.BlockSpec((nb,) + shape, lambda b, ci: (b,) + (0,) * len(shape))
    rowb = lambda w: pl.BlockSpec((rows, w), lambda b, ci: (b * nc + ci, 0))
    in_specs = [rowb(CONV_DIM), rowb(SSD_W), rowb(LANE),
                _full((CONV_K, CONV_DIM)), _full((1, CONV_DIM)), _full((1, LANE)), _full((1, LANE)),
                _full((1, SSD_W)), _full((1, SSD_W)),
                _full((c, c)), _full((LANE, LANE)), _full((LANE, SSD_W)), _full((G_SSD, GRP_W, LANE))]
    args = [xbc, z, dt, lp["conv_w"], lp["conv_b"], lp["dt_bias"], lp["a"], lp["d_skip"], lp["ssd_norm_g"],
            tri, eye, expand, sel]
    conv_shape, ssm_shape = (CONV_K - 1, CONV_DIM), (H_SSD, P_SSD, N_SSD)
    if has_state:
        stacked = lambda shape: pl.BlockSpec((None, nb) + shape, lambda b, ci: (layer, b) + (0,) * len(shape))
        in_specs += [stacked(conv_shape), stacked(ssm_shape)]
        args += list(states)
    in_specs += [seq(conv_shape)] * len(prev) + [seq(ssm_shape)] * len(prev)
    args += [p[0] for p in prev] + [p[1] for p in prev]
    block_bytes = (_nbytes((rows, CONV_DIM + 2 * SSD_W + LANE), F32) + _nbytes((nb, c + SUBLANE, CONV_DIM), F32)
                   + (3 + 4 * len(prev)) * _nbytes((nb, SSD_W, LANE), F32) + _nbytes((LANE, SSD_W), F32)
                   + 12 * _nbytes((c, SSD_W), F32))
    nconv_spec, nconv_shape = _state_out(prev, batch, nb, conv_shape)
    nssm_spec, nssm_shape = _state_out(prev, batch, nb, ssm_shape)
    return pl.pallas_call(
        functools.partial(_ssd_body, c, nb, nc, has_state, len(prev)),
        grid=(batch // nb, nc), in_specs=in_specs,
        out_specs=[rowb(SSD_W), nconv_spec, nssm_spec],
        out_shape=[jax.ShapeDtypeStruct((t, SSD_W), BF16), nconv_shape, nssm_shape],
        scratch_shapes=[pltpu.VMEM((nb, c + SUBLANE, CONV_DIM), F32),
                        pltpu.VMEM((nb, G_SSD, GRP_W, N_SSD), F32)],
        compiler_params=_params(("arbitrary", "arbitrary"), block_bytes), name="ssd",
    )(*args)


def _ssd_dec_pre_body(n_prev, x_ref, dt_ref, cw_ref, cb_ref, sconv_ref, *rest):
    prev, (xt_ref, dtt_ref, nconv_ref) = rest[:n_prev], rest[n_prev:]
    length, hist = x_ref.shape[0], CONV_K - 1
    rows = [sconv_ref[j] for j in range(hist)] + [x_ref[m] for m in range(length)]
    for m in range(length):
        acc = rows[m] * cw_ref[0:1, :]
        for j in range(1, CONV_K):
            acc = acc + rows[m + j] * cw_ref[j:j + 1, :]
        conv = jax.nn.silu(cb_ref[...] + acc)
        for k in range(CONV_DIM // LANE):
            xt_ref[m, k * LANE:(k + 1) * LANE, :] = conv[:, k * LANE:(k + 1) * LANE].T
        dtt_ref[m] = dt_ref[m].T
    new_conv = nconv_ref.at[n_prev] if n_prev else nconv_ref
    for l in range(n_prev):
        nconv_ref[l] = prev[l][...]
    for j in range(hist):
        new_conv[j] = rows[length + j]


def _ssd_dec_head_body(n_prev, dtb_ref, a_ref, dsk_ref, x_ref, b_ref, c_ref, dt_ref, s_ref, *rest):
    prev, (y_ref, ns_ref, xw_scr, off_scr) = rest[:n_prev], rest[n_prev:]
    h = pl.program_id(0)
    length = x_ref.shape[0]
    new_state = ns_ref.at[n_prev] if n_prev else ns_ref
    for l in range(n_prev):
        ns_ref[l] = prev[l][...]
    dt = [jax.nn.softplus(dt_ref[m, pl.ds(h, 1), :] + dtb_ref[h]) for m in range(length)]
    cum = []
    for m in range(length):
        cum.append(dt[m] * a_ref[h] if m == 0 else cum[-1] + dt[m] * a_ref[h])
    last = cum[-1]
    decay = jnp.exp(last)
    for j in range(length):
        xw_scr[j] = x_ref[j] * (jnp.exp(last - cum[j]) * dt[j])

    def per_state_row(p, carry):
        sp = s_ref[p]
        new = decay * sp
        for j in range(length):
            new = new + xw_scr[j, pl.ds(p, 1), :] * b_ref[j]
        new_state[p] = new
        for m in range(length):
            off_scr[m, pl.ds(p, 1), :] = jnp.sum(c_ref[m] * sp, axis=0, keepdims=True)
        return carry

    lax.fori_loop(0, P_SSD, per_state_row, 0)
    for m in range(length):
        y = jnp.exp(cum[m]) * off_scr[m] + dsk_ref[h] * x_ref[m]
        for j in range(m + 1):
            cb = jnp.sum(c_ref[m] * b_ref[j], axis=0, keepdims=True)
            y = y + (jnp.exp(cum[m] - cum[j]) * cb * dt[j]) * x_ref[j]
        y_ref[m] = y


def _ssd_dec_post_body(yt_ref, z_ref, ng_ref, o_ref):
    y = jnp.concatenate([yt_ref[k * LANE:(k + 1) * LANE, :].T for k in range(SSD_W // LANE)], axis=1)
    y = y * jax.nn.silu(z_ref[...])
    normed = []
    for g in range(G_SSD):
        yg = y[:, g * GRP_W:(g + 1) * GRP_W]
        normed.append(yg * lax.rsqrt(jnp.mean(yg * yg, axis=-1, keepdims=True) + EPS))
    o_ref[...] = (jnp.concatenate(normed, axis=1) * ng_ref[...]).astype(o_ref.dtype)


def _ssd_decode(xbc, z, dt, lp, states, layer, prev, batch, length):
    assert batch == LANE
    t = batch * length
    depth_out = len(prev) + 1
    tok = lambda a, w: a.reshape(batch, length, w).transpose(1, 0, 2)
    conv_in = states[0].transpose(0, 2, 1, 3)
    ssm_in = states[1].transpose(0, 2, 3, 4, 1)
    hist = CONV_K - 1
    stack = lambda shape: ((depth_out,) + shape) if prev else shape

    pre_in = [_full((length, batch, CONV_DIM)), _full((length, batch, LANE)), _full((CONV_K, CONV_DIM)),
              _full((1, CONV_DIM)),
              pl.BlockSpec((None, hist, batch, CONV_DIM), lambda i: (layer, 0, 0, 0))]
    pre_in += [_full((hist, batch, CONV_DIM))] * len(prev)
    xt, dtt, new_conv = pl.pallas_call(
        functools.partial(_ssd_dec_pre_body, len(prev)), grid=(1,), in_specs=pre_in,
        out_specs=[_full((length, CONV_DIM, batch)), _full((length, LANE, batch)),
                   _full(stack((hist, batch, CONV_DIM)))],
        out_shape=[jax.ShapeDtypeStruct((length, CONV_DIM, batch), F32),
                   jax.ShapeDtypeStruct((length, LANE, batch), F32),
                   jax.ShapeDtypeStruct(stack((hist, batch, CONV_DIM)), F32)],
        compiler_params=_params(("arbitrary",), 4 * _nbytes((length, batch, CONV_DIM), F32)), name="ssd_dec_conv",
    )(tok(xbc, CONV_DIM), tok(dt, LANE), lp["conv_w"], lp["conv_b"], conv_in, *[p[0] for p in prev])

    head_rows = lambda first: pl.BlockSpec((length, P_SSD, batch), lambda h: (0, first(h), 0))
    smem = pl.BlockSpec(memory_space=pltpu.SMEM)
    state_block = (P_SSD, N_SSD, batch)
    head_in = [smem, smem, smem,
               head_rows(lambda h: h),
               head_rows(lambda h: SSD_W // P_SSD + h // R_SSD),
               head_rows(lambda h: (SSD_W + G_SSD * N_SSD) // P_SSD + h // R_SSD),
               _full((length, LANE, batch)),
               pl.BlockSpec((None, None) + state_block, lambda h: (layer, h, 0, 0, 0))]
    head_in += [pl.BlockSpec((None,) + state_block, lambda h: (h, 0, 0, 0))] * len(prev)
    if prev:
        ns_spec = pl.BlockSpec((depth_out, None) + state_block, lambda h: (0, h, 0, 0, 0))
    else:
        ns_spec = pl.BlockSpec((None,) + state_block, lambda h: (h, 0, 0, 0))
    heads = lambda v: v.reshape(LANE)[:H_SSD]
    yt, new_ssm = pl.pallas_call(
        functools.partial(_ssd_dec_head_body, len(prev)), grid=(H_SSD,), in_specs=head_in,
        out_specs=[head_rows(lambda h: h), ns_spec],
        out_shape=[jax.ShapeDtypeStruct((length, SSD_W, batch), F32),
                   jax.ShapeDtypeStruct(stack((H_SSD,) + state_block), F32)],
        scratch_shapes=[pltpu.VMEM((length, P_SSD, batch), F32), pltpu.VMEM((length, P_SSD, batch), F32)],
        compiler_params=_params(("arbitrary",), (4 + 2 * depth_out) * _nbytes(state_block, F32)), name="ssd_dec_heads",
    )(heads(lp["dt_bias"]), heads(lp["a"]), lp["d_skip"].reshape(H_SSD, P_SSD)[:, 0], xt, xt, xt, dtt, ssm_in,
      *[p[1] for p in prev])

    y = pl.pallas_call(
        _ssd_dec_post_body, grid=(length,),
        in_specs=[pl.BlockSpec((None, SSD_W, batch), lambda m: (m, 0, 0)),
                  pl.BlockSpec((None, batch, SSD_W), lambda m: (m, 0, 0)), _full((1, SSD_W))],
        out_specs=pl.BlockSpec((None, batch, SSD_W), lambda m: (m, 0, 0)),
        out_shape=jax.ShapeDtypeStruct((length, batch, SSD_W), BF16),
        compiler_params=_params(("arbitrary",), 3 * _nbytes((batch, SSD_W), F32)), name="ssd_dec_gate",
    )(yt, tok(z, SSD_W), lp["ssd_norm_g"])
    y = y.transpose(1, 0, 2).reshape(t, SSD_W)
    if prev:
        new_conv = new_conv.transpose(0, 2, 1, 3)
        new_ssm = new_ssm.transpose(0, 4, 1, 2, 3)
    return y, new_conv, new_ssm


def _first_argmax(vals):
    best = vals[0]
    for v in vals[1:]:
        best = jnp.maximum(best, v)
    idx = jnp.full(best.shape, len(vals) - 1, jnp.int32)
    for i in range(len(vals) - 2, -1, -1):
        idx = jnp.where(vals[i] == best, i, idx)
    return best, idx


def _merge_body(bounds, *refs):
    n_groups = len(bounds)
    acts, refs = refs[:4 * n_groups], refs[4 * n_groups:]
    weights, refs = refs[:7], refs[7:]
    x2_refs, shared = refs[:n_groups], refs[n_groups:]
    i = pl.program_id(0)
    for gi, (first, count) in enumerate(bounds):
        @pl.when((i >= first) & (i < first + count))
        def _():
            _merge_tile(*acts[4 * gi:4 * gi + 4], *weights, x2_refs[gi], *shared)


def _merge_tile(ret_ref, ssd_ref, gates_ref, x_ref, wr_ref, ws_ref, wo_ref, g2_ref, rhi_ref, rlo_ref, upp_ref,
                x2_ref, hw_ref, route_ref, cnt_ref):
    ret_y = _dot(ret_ref[...], wr_ref[...])
    ssd_y = _dot(ssd_ref[...], ws_ref[...])
    m = (jax.nn.sigmoid(gates_ref[:, :D_MODEL]) * ret_y + jax.nn.sigmoid(gates_ref[:, D_MODEL:]) * ssd_y)
    x2 = x_ref[...] + _dot(m.astype(BF16), wo_ref[...])
    x2_ref[...] = x2
    h2 = _rmsnorm(x2, g2_ref[...])
    bits = pltpu.bitcast(h2.astype(BF16).astype(F32), jnp.uint32)
    half = D_MODEL // 2
    word = lax.bitwise_or(lax.bitwise_and(bits[:, half:], jnp.uint32(HIGH_HALF)),
                          lax.shift_right_logical(bits[:, :half], jnp.uint32(16)))
    n_rows = h2.shape[0]
    for k in range(PACKED_ROWS):
        hw_ref[pl.ds(k, n_rows, stride=SUBLANE), :] = word[:, k * LANE:(k + 1) * LANE]

    h_hi = h2.astype(BF16)
    h_lo = (h2 - h_hi.astype(F32)).astype(BF16)
    logits = _dot_nt(rhi_ref[...], h_hi) + (_dot_nt(rlo_ref[...], h_hi) + _dot_nt(rhi_ref[...], h_lo))
    colv = lambda i: logits[i:i + 1, :]
    glog = [colv(i) for i in range(N_EGROUPS)]
    gmax, gidx = _first_argmax(glog)
    den = glog[0] * 0.0
    for v in glog:
        den = den + jnp.exp(v - gmax)
    gw = 1.0 / den
    sel = []
    for j in range(EXP_PER_GROUP):
        v = colv(N_EGROUPS + (N_EGROUPS - 1) * EXP_PER_GROUP + j)
        for g in range(N_EGROUPS - 2, -1, -1):
            v = jnp.where(gidx == g, colv(N_EGROUPS + g * EXP_PER_GROUP + j), v)
        sel.append(v)
    v1, i1 = _first_argmax(sel)
    v2, i2 = _first_argmax([jnp.where(i1 == j, -jnp.inf, sel[j]) for j in range(EXP_PER_GROUP)])
    e = jnp.exp(v2 - v1)
    w1 = gw / (1.0 + e)
    w2 = gw * e / (1.0 + e)
    first_low = i1 < i2
    lo = jnp.where(first_low, i1, i2)
    hi = jnp.where(first_low, i2, i1)
    pair = jnp.where(lo == 0, hi - 1, jnp.where(lo == 1, hi + 1, N_PAIRS - 1))
    cls = gidx * N_PAIRS + pair
    w_lo = jnp.where(first_low, w1, w2)
    w_hi = jnp.where(first_low, w2, w1)

    sub = lax.broadcasted_iota(jnp.int32, (META_W, n_rows), 0)
    onehot = jnp.where(sub == cls, 1.0, 0.0).astype(BF16)
    before = _dot(onehot, upp_ref[...])
    rank = jnp.sum(onehot.astype(F32) * before, axis=0, keepdims=True)
    meta = jnp.where(sub == 0, w_lo, jnp.where(sub == 1, w_hi, jnp.where(
        sub == 2, cls.astype(F32), jnp.where(sub == 3, rank, 0.0))))
    route_ref[0] = meta[0:SUBLANE, :]
    meta_bits = pltpu.bitcast(meta.T, jnp.uint32)
    for k in range(PACKED_ROWS, TOKEN_TILE_ROWS):
        hw_ref[pl.ds(k, n_rows, stride=SUBLANE), :] = meta_bits
    cnt_ref[0] = _dot_nt(jnp.ones((SUBLANE, n_rows), BF16), onehot)[0:1, :]


def _merge(group_acts, lp, big, layer):
    counts = [a[3].shape[0] // ROW_TILE for a in group_acts]
    bounds = tuple((sum(counts[:gi]), counts[gi]) for gi in range(len(counts)))
    n_tiles = sum(counts)
    total_rows = n_tiles * ROW_TILE
    upp = (jnp.arange(ROW_TILE)[:, None] < jnp.arange(ROW_TILE)[None, :]).astype(BF16)
    stacked = [big["w_ret_out"], big["w_ssd_out"], big["w_o"]]
    ws = [lp["ln2_g"], lp["router_hi"], lp["router_lo"], upp]
    block_bytes = (sum(_nbytes(w.shape[1:], w.dtype) for w in stacked) + sum(_nbytes(w.shape, w.dtype) for w in ws)
                   + len(counts) * (_nbytes((ROW_TILE, RET_W + SSD_W), BF16) + _nbytes((ROW_TILE, 4 * D_MODEL), F32))
                   + _nbytes((ROW_TILE, 3 * D_MODEL), F32))

    def group_row(width, first, count):
        return pl.BlockSpec((ROW_TILE, width), lambda i: (jnp.clip(i - first, 0, count - 1), 0))

    in_specs, args, x2_specs, x2_shapes = [], [], [], []
    for (first, count), acts in zip(bounds, group_acts, strict=True):
        in_specs += [group_row(w, first, count) for w in (RET_W, SSD_W, 2 * D_MODEL, D_MODEL)]
        args += list(acts)
        x2_specs.append(group_row(D_MODEL, first, count))
        x2_shapes.append(jax.ShapeDtypeStruct((count * ROW_TILE, D_MODEL), F32))
    in_specs += [_layer_block(w, layer) for w in stacked] + [_full(w.shape) for w in ws]
    args += stacked + ws
    outs = pl.pallas_call(
        functools.partial(_merge_body, bounds), grid=(n_tiles,), in_specs=in_specs,
        out_specs=x2_specs + [_token_spec(0), pl.BlockSpec((1, SUBLANE, ROW_TILE), lambda i: (i, 0, 0)),
                              pl.BlockSpec((1, 1, META_W), lambda i: (i, 0, 0))],
        out_shape=x2_shapes + [jax.ShapeDtypeStruct((total_rows * TOKEN_TILE_ROWS, LANE), jnp.uint32),
                               jax.ShapeDtypeStruct((n_tiles, SUBLANE, ROW_TILE), F32),
                               jax.ShapeDtypeStruct((n_tiles, 1, META_W), F32)],
        compiler_params=_params(("arbitrary",), block_bytes), name="merge_router",
    )(*args)
    return outs[:len(counts)], outs[len(counts):]


PLAN_ROWS = LANE
TAB_ROWS = SUBLANE
TAB_LO, TAB_HI, TAB_USED = 0, 1, 2
BYTE = 256.0
MOE_RING = 3


def _row_of(col_vals):
    pick = (lax.broadcasted_iota(jnp.int32, (TAB_ROWS, LANE), 0)
            == lax.broadcasted_iota(jnp.int32, (TAB_ROWS, LANE), 1)).astype(BF16)
    hi = jnp.floor(col_vals * (1.0 / BYTE))
    lo = col_vals - hi * BYTE
    return BYTE * _dot_nt(pick, hi.astype(BF16)) + _dot_nt(pick, lo.astype(BF16))


def _plan_body(tile, n_row_tiles, route_ref, cnt_ref, low_ref, upp_ref, pos_ref, tab_ref, offs_scr):
    cnt = cnt_ref[...]
    tot = jnp.sum(cnt, axis=0, keepdims=True)
    cls_tiles = jnp.floor((tot + (tile - 1.0)) * (1.0 / tile))
    first_tile = _dot(jnp.broadcast_to(cls_tiles, (SUBLANE, LANE)).astype(BF16), upp_ref[...])[0:1, :]
    base = first_tile * tile
    ends = base + cls_tiles * tile
    offs_scr[...] = base + _dot(low_ref[...], cnt.astype(BF16))

    def per_row_tile(j, carry):
        meta = route_ref[j]
        sub = lax.broadcasted_iota(jnp.int32, (LANE, ROW_TILE), 0).astype(F32)
        onehot = jnp.where(sub == meta[2:3, :], 1.0, 0.0).astype(BF16)
        offs = offs_scr[pl.ds(j, 1), :]
        hi = jnp.floor(offs * (1.0 / BYTE))
        rows8 = lambda v: jnp.broadcast_to(v, (SUBLANE, LANE)).astype(BF16)
        pos = BYTE * _dot(rows8(hi), onehot) + _dot(rows8(offs - hi * BYTE), onehot)
        pos_ref[j] = (pos[0:1, :] + meta[3:4, :]).astype(jnp.int32)
        return carry

    lax.fori_loop(0, n_row_tiles, per_row_tile, 0)

    lane = lax.broadcasted_iota(jnp.int32, (PLAN_ROWS, LANE), 1)
    tile_idx = lax.broadcasted_iota(jnp.int32, (PLAN_ROWS, LANE), 0).astype(F32)[:, 0:1]
    n_used = jnp.sum(cls_tiles, axis=1, keepdims=True)
    start = jnp.minimum(tile_idx, n_used - 1.0) * tile
    tcls = jnp.sum(jnp.where((lane < N_CLASSES) & (ends <= start), 1.0, 0.0), axis=1, keepdims=True)
    tcls = jnp.minimum(tcls, N_CLASSES - 1.0)
    group = sum(jnp.where(tcls >= k * N_PAIRS, 1.0, 0.0) for k in range(1, N_EGROUPS))
    pair = tcls - group * N_PAIRS
    p_lo = jnp.where(pair >= 3, 1.0, 0.0) + jnp.where(pair >= 5, 1.0, 0.0)
    p_hi = jnp.where(pair == 0, 1.0, jnp.where((pair == 1) | (pair == 3), 2.0, 3.0))
    table = jnp.where(lane == TAB_LO, group * EXP_PER_GROUP + p_lo,
                      jnp.where(lane == TAB_HI, group * EXP_PER_GROUP + p_hi,
                                jnp.where(lane == TAB_USED, n_used, 0.0)))
    tab_ref[...] = _row_of(table).astype(jnp.int32)


def _plan(route, counts, tile):
    nrt = route.shape[0]
    t = nrt * ROW_TILE
    assert nrt <= PLAN_ROWS and t // tile + N_CLASSES <= PLAN_ROWS
    cnt = jnp.pad(counts[:, 0, :], ((0, PLAN_ROWS - nrt), (0, 0)))
    idx = jnp.arange(PLAN_ROWS)
    low = (idx[:, None] > idx[None, :]).astype(BF16)
    upp = (idx[:, None] < idx[None, :]).astype(BF16)
    pos, tab = pl.pallas_call(
        functools.partial(_plan_body, tile, nrt),
        out_shape=[jax.ShapeDtypeStruct((nrt, 1, ROW_TILE), jnp.int32),
                   jax.ShapeDtypeStruct((TAB_ROWS, PLAN_ROWS), jnp.int32)],
        scratch_shapes=[pltpu.VMEM((PLAN_ROWS, LANE), F32)],
        compiler_params=pltpu.CompilerParams(vmem_limit_bytes=_vmem_limit(_nbytes(route.shape, F32))),
        name="moe_plan",
    )(route, cnt, low, upp)
    return pos.reshape(t), tab


def _moe_body(tile, n_tok, pos_ref, tab_ref, hw_hbm, spare_hbm, wgl_ref, wul_ref, wdl_ref, wgh_ref, wuh_ref,
              wdh_ref, y_hbm, buf, obuf, tok_smem, gsem, ssem, isem):
    i = pl.program_id(0)
    n_used = tab_ref[TAB_USED, 0]
    ring = buf.shape[0]
    cur = lax.rem(i, ring)
    nxt1 = lax.rem(i + 1, ring)
    nxt2 = lax.rem(i + 2, ring)
    n_points = 16
    per_point = tile // n_points
    tr = TOKEN_TILE_ROWS

    def token_tile(ref, token):
        start = token * tr if isinstance(token, int) else pl.multiple_of(token * tr, tr)
        return ref.at[pl.ds(start, tr)]

    def dma_thread(r):
        return r % 2 if isinstance(r, int) else 0

    def gather_start(block, r, sl):
        src = jnp.minimum(tok_smem[block * tile + r], n_tok - 1)
        pltpu.make_async_copy(token_tile(hw_hbm, src), token_tile(buf.at[sl], r), gsem.at[sl]).start(dma_thread(r))

    def scatter_start(block, r, sl):
        dst = tok_smem[block * tile + r]
        pltpu.make_async_copy(token_tile(obuf.at[sl], r), token_tile(y_hbm, dst), ssem.at[sl]).start(dma_thread(r))

    def gather_wait(sl):
        pltpu.make_async_copy(hw_hbm.at[pl.ds(0, tile * tr)], buf.at[sl], gsem.at[sl]).wait()

    def scatter_wait(sl):
        pltpu.make_async_copy(obuf.at[sl], y_hbm.at[pl.ds(0, tile * tr)], ssem.at[sl]).wait()

    def for_rows(n, fn):
        lax.fori_loop(0, n, lambda r, carry: (fn(r), carry)[1], 0, unroll=8)

    def block_of(t):
        return jnp.minimum(t, n_used - 1) + 1

    @pl.when(i < n_used)
    def _():
        @pl.when(i == 0)
        def _():
            fill = pltpu.make_async_copy(spare_hbm, tok_smem, isem)
            fill.start()
            fill.wait()

            def place(t):
                tok_smem[tile + pos_ref[t]] = t
            for_rows(n_tok, place)
            obuf[ring - 1] = jnp.zeros(obuf.shape[1:], F32)
            for_rows(tile, lambda r: gather_start(block_of(0), r, 0))
            for_rows(tile, lambda r: gather_start(block_of(1), r, 1))

        gather_wait(cur)
        issued = [0]

        def issue_point():
            for r in range(issued[0] * per_point, (issued[0] + 1) * per_point):
                gather_start(block_of(i + 2), r, nxt2)
                scatter_start(i, r, nxt2)
            issued[0] += 1

        def tile_row(k):
            return buf[cur, pl.ds(k, tile, stride=tr), :]

        words = [tile_row(k) for k in range(PACKED_ROWS)]
        low = [pltpu.bitcast(lax.shift_left(w, jnp.uint32(16)), F32).astype(BF16) for w in words]
        high = [pltpu.bitcast(lax.bitwise_and(w, jnp.uint32(HIGH_HALF)), F32).astype(BF16) for w in words]
        xb = jnp.concatenate(low + high, axis=1)
        meta = pltpu.bitcast(tile_row(PACKED_ROWS), F32)
        w_lo, w_hi = meta[:, 0:1], meta[:, 1:2]
        half = D_FF // 2
        acts = []
        for wg_ref, wu_ref in ((wgl_ref, wul_ref), (wgh_ref, wuh_ref)):
            parts = []
            for n in range(2):
                gate = _dot(xb, wg_ref[:, n * half:(n + 1) * half])
                issue_point()
                up = _dot(xb, wu_ref[:, n * half:(n + 1) * half])
                issue_point()
                parts.append((jax.nn.silu(gate) * up).astype(BF16))
            acts.append(jnp.concatenate(parts, axis=1))
        quarter = D_MODEL // 4
        for n in range(4):
            cols = slice(n * quarter, (n + 1) * quarter)
            d_lo = _dot(acts[0], wdl_ref[:, cols])
            issue_point()
            d_hi = _dot(acts[1], wdh_ref[:, cols])
            issue_point()
            out = d_lo * w_lo + d_hi * w_hi
            for kk in range(quarter // LANE):
                obuf[cur, pl.ds(n * (quarter // LANE) + kk, tile, stride=tr), :] = out[:, kk * LANE:(kk + 1) * LANE]
        assert issued[0] == n_points

        @pl.when(i >= 1)
        def _():
            scatter_wait(nxt1)

        @pl.when(i == n_used - 1)
        def _():
            scatter_wait(nxt2)
            for_rows(tile, lambda r: scatter_start(i + 1, r, cur))
            scatter_wait(cur)
            gather_wait(nxt1)
            gather_wait(nxt2)


def _moe(hw, pos, tab, big, layer, tile):
    t = hw.shape[0] // TOKEN_TILE_ROWS
    n_tiles = t // tile + N_CLASSES
    wspec = lambda shape, row: pl.BlockSpec((None, None) + shape, lambda i, pos, tab: (layer, tab[row, i], 0, 0))
    in_specs = [pl.BlockSpec(memory_space=pl.ANY)] * 2
    spare = t + jnp.arange((n_tiles + 1) * tile, dtype=jnp.int32) % tile
    for row in (TAB_LO, TAB_HI):
        in_specs += [wspec((D_MODEL, D_FF), row), wspec((D_MODEL, D_FF), row), wspec((D_FF, D_MODEL), row)]
    block_bytes = 6 * _nbytes((D_MODEL, D_FF), BF16) + 6 * _nbytes((tile, D_MODEL), F32)
    tile_rows = tile * TOKEN_TILE_ROWS
    grid_spec = pltpu.PrefetchScalarGridSpec(
        num_scalar_prefetch=2, grid=(n_tiles,), in_specs=in_specs,
        out_specs=pl.BlockSpec(memory_space=pl.ANY),
        scratch_shapes=[pltpu.VMEM((MOE_RING, tile_rows, LANE), jnp.uint32),
                        pltpu.VMEM((MOE_RING, tile_rows, LANE), F32),
                        pltpu.SMEM(((n_tiles + 1) * tile,), jnp.int32),
                        pltpu.SemaphoreType.DMA((MOE_RING,)), pltpu.SemaphoreType.DMA((MOE_RING,)),
                        pltpu.SemaphoreType.DMA(())])
    return pl.pallas_call(
        functools.partial(_moe_body, tile, t), grid_spec=grid_spec,
        out_shape=jax.ShapeDtypeStruct(((t + tile) * TOKEN_TILE_ROWS, LANE), F32),
        compiler_params=_params(("arbitrary",), block_bytes), name="moe",
    )(pos, tab, hw, spare, big["w_gate"], big["w_up"], big["w_down"], big["w_gate"], big["w_up"], big["w_down"])


FINAL_TILE = 1024


def _final_body(x_ref, y_ref, g_ref, o_ref):
    o_ref[...] = _rmsnorm(x_ref[...] + _token_rows(y_ref, FINAL_TILE), g_ref[...])


def _final_norm(x, y, y_offset, g):
    t = x.shape[0]
    row = pl.BlockSpec((FINAL_TILE, D_MODEL), lambda i: (i, 0))
    return pl.pallas_call(
        _final_body, grid=(t // FINAL_TILE,),
        in_specs=[row, _token_spec(y_offset, FINAL_TILE), _full((1, D_MODEL))],
        out_specs=row, out_shape=jax.ShapeDtypeStruct((t, D_MODEL), F32),
        compiler_params=_params(("arbitrary",), 3 * _nbytes((FINAL_TILE, D_MODEL), F32)), name="final_norm",
    )(x, y, g)


def _layer_params(i, ln1_g, ret_gn_g, conv_w, conv_b, dt_bias, a_log, d_skip, ssd_norm_g, ln2_g, w_rg, w_re):
    pad_heads = lambda v: jnp.pad(v.astype(F32), (0, LANE - H_SSD)).reshape(1, LANE)
    router = jnp.pad(jnp.concatenate([w_rg[i], w_re[i]], axis=1),
                     ((0, 0), (0, META_W - N_EGROUPS - N_EXPERTS)))
    router_hi = router.astype(BF16)
    return {
        "ln1_g": ln1_g[i].reshape(1, D_MODEL),
        "ret_gn_g": ret_gn_g[i].reshape(1, RET_W),
        "conv_w": conv_w[i], "conv_b": conv_b[i].reshape(1, CONV_DIM),
        "dt_bias": pad_heads(dt_bias[i]), "a": pad_heads(-jnp.exp(a_log[i].astype(F32))),
        "d_skip": jnp.repeat(d_skip[i].astype(F32), P_SSD).reshape(1, SSD_W),
        "ssd_norm_g": ssd_norm_g[i].reshape(1, SSD_W),
        "ln2_g": ln2_g[i].reshape(1, D_MODEL),
        "router_hi": router_hi.T, "router_lo": (router - router_hi.astype(F32)).astype(BF16).T,
    }


def _stacked_params(w_in, w_ret_out, w_ssd_out, w_o, w_gate, w_up, w_down):
    offs = [0]
    for s in IN_SIZES:
        offs.append(offs[-1] + s)
    return {
        "w_in": [w_in.astype(BF16),
                 jnp.pad(w_in[:, :, offs[6]:offs[7]], ((0, 0), (0, 0), (0, LANE - H_SSD))).astype(BF16),
                 w_in[:, :, offs[7]:offs[9]].astype(BF16)],
        "w_ret_out": w_ret_out.astype(BF16), "w_ssd_out": w_ssd_out.astype(BF16), "w_o": w_o.astype(BF16),
        "w_gate": w_gate.astype(BF16), "w_up": w_up.astype(BF16), "w_down": w_down.astype(BF16),
    }


class _Group:
    def __init__(self, x, start, states, row_offset):
        self.batch, self.length, _ = x.shape
        self.x = x.reshape(self.batch * self.length, D_MODEL)
        self.start, self.states, self.row_offset = start, states, row_offset
        self.rets, self.ssms, self.convs = [], [], []


def kernel(x_prompt, x_sample, state_ret, state_ssm, state_conv, ln1_g, w_in, ret_gn_g, w_ret_out, conv_w, conv_b, dt_bias, a_log, d_skip, ssd_norm_g, w_ssd_out, w_o, ln2_g, w_router_group, w_router_expert, w_e_gate, w_e_up, w_e_down, lnf_g):
    depth = w_in.shape[0]
    past_len = 16384.0
    layers = [_layer_params(i, ln1_g, ret_gn_g, conv_w, conv_b, dt_bias, a_log, d_skip, ssd_norm_g, ln2_g,
                            w_router_group, w_router_expert) for i in range(depth)]
    big = _stacked_params(w_in, w_ret_out, w_ssd_out, w_o, w_e_gate, w_e_up, w_e_down)
    n_prompt = x_prompt.shape[0] * x_prompt.shape[1]
    groups = [_Group(x_prompt, 0.0, None, 0),
              _Group(x_sample, past_len, (state_ret, state_ssm, state_conv), n_prompt)]
    y = None
    for i, lp in enumerate(layers):
        group_acts = []
        last = i == depth - 1
        for grp in groups:
            outs = _inproj(grp.x, y, grp.row_offset, lp["ln1_g"], big["w_in"], i)
            if y is not None:
                grp.x, outs = outs[0], outs[1:]
            qkvr, z, xbc, dt, gates = outs
            ret_state = None if grp.states is None else grp.states[0]
            ssd_state = None if grp.states is None else (grp.states[2], grp.states[1])
            ret, new_ret = _retention(qkvr, grp.start, lp["ret_gn_g"], ret_state, i,
                                      grp.rets if last else (), grp.batch, grp.length)
            one_chunk = grp.length == math.gcd(grp.length, CHUNK)
            ssd_fn = _ssd_decode if (one_chunk and ssd_state is not None and grp.batch == LANE) else _ssd
            ssd, new_conv, new_ssm = ssd_fn(xbc, z, dt, lp, ssd_state, i,
                                            list(zip(grp.convs, grp.ssms)) if last else (), grp.batch, grp.length)
            grp.rets.append(new_ret)
            grp.ssms.append(new_ssm)
            grp.convs.append(new_conv)
            group_acts.append((ret, ssd, gates, grp.x))
        new_x, (hw, route, counts) = _merge(group_acts, lp, big, i)
        for grp, x2 in zip(groups, new_x, strict=True):
            grp.x = x2
        pos, tab = _plan(route, counts, ROW_TILE)
        y = _moe(hw, pos, tab, big, i, ROW_TILE)
    outs = []
    for grp in groups:
        out = _final_norm(grp.x, y, grp.row_offset, lnf_g.reshape(1, D_MODEL))
        outs.append(out.reshape(grp.batch, grp.length, D_MODEL))
    states = []
    for grp in groups:
        states += [grp.rets[-1], grp.ssms[-1], grp.convs[-1]]
    return tuple(outs + states)
```

```python
import functools
import math

import jax
import jax.numpy as jnp
from jax import lax
from jax.experimental import pallas as pl
from jax.experimental.pallas import tpu as pltpu

F32 = jnp.float32
BF16 = jnp.bfloat16

D_MODEL = 1024
H_RET, DK_RET, DV_RET = 4, 128, 128
RET_W = H_RET * DV_RET
ROPE_BASE = 10000.0
H_SSD, P_SSD, G_SSD, N_SSD = 16, 64, 2, 64
R_SSD = H_SSD // G_SSD
SSD_W = H_SSD * P_SSD
GRP_W = SSD_W // G_SSD
CONV_K = 4
CONV_DIM = SSD_W + 2 * G_SSD * N_SSD
CHUNK = 128
N_EGROUPS, EXP_PER_GROUP = 4, 4
N_EXPERTS = N_EGROUPS * EXP_PER_GROUP
N_PAIRS = EXP_PER_GROUP * (EXP_PER_GROUP - 1) // 2
N_CLASSES = N_EGROUPS * N_PAIRS
D_FF = 512
EPS = 1e-6
IN_SIZES = [RET_W, RET_W, RET_W, RET_W, SSD_W, CONV_DIM, H_SSD, D_MODEL, D_MODEL]

LANE = 128
SUBLANE = 8
MIB = 1024 * 1024

ROW_TILE = 256
META_W = LANE
TOKEN_TILE_ROWS = D_MODEL // LANE
assert TOKEN_TILE_ROWS == SUBLANE
PACKED_ROWS = D_MODEL // (2 * LANE)
HIGH_HALF = 0xFFFF0000
SAMPLE_SEQS_PER_STEP = 8
PROMPT_SEQS_PER_STEP = 4


def _vmem_limit(block_bytes):
    return int(min(2 * block_bytes + 16 * MIB, 56 * MIB))


def _params(semantics, block_bytes):
    return pltpu.CompilerParams(dimension_semantics=semantics,
                                vmem_limit_bytes=_vmem_limit(block_bytes))


def _nbytes(shape, dtype):
    return math.prod(shape) * jnp.dtype(dtype).itemsize


def _full(shape):
    return pl.BlockSpec(shape, lambda *_: (0,) * len(shape))


def _dot(a, b):
    return jnp.dot(a, b, preferred_element_type=F32)


def _dot_nt(a, b):
    return lax.dot_general(a, b, (((1,), (1,)), ((), ())), preferred_element_type=F32)


def _dot_tn(a, b):
    return lax.dot_general(a, b, (((0,), (0,)), ((), ())), preferred_element_type=F32)


def _split3(a):
    hi = a.astype(BF16)
    r1 = a - hi.astype(F32)
    mid = r1.astype(BF16)
    lo = (r1 - mid.astype(F32)).astype(BF16)
    return hi, mid, lo


def _rmsnorm(x, g):
    r = lax.rsqrt(jnp.mean(x * x, axis=-1, keepdims=True) + EPS)
    return x * r * g


def _token_rows(ref, rows):
    return jnp.concatenate([ref[pl.ds(k, rows, stride=SUBLANE), :] for k in range(TOKEN_TILE_ROWS)], axis=1)


def _token_spec(row_offset, rows=ROW_TILE):
    assert row_offset % rows == 0
    first = row_offset // rows
    return pl.BlockSpec((rows * TOKEN_TILE_ROWS, LANE), lambda i: (i + first, 0))


def _inproj_body(has_add, *refs):
    n_in = 6 if has_add else 5
    ins, outs = refs[:n_in], refs[n_in:]
    if has_add:
        x_ref, y_ref, g_ref = ins[:3]
        x = x_ref[...] + _token_rows(y_ref, ROW_TILE)
        outs[0][...] = x
        outs = outs[1:]
    else:
        x_ref, g_ref = ins[:2]
        x = x_ref[...]
    w_main, w_dt, w_gates = ins[-3:]
    h = _rmsnorm(x, g_ref[...]).astype(BF16)
    col = 0
    for o_ref in outs[:3]:
        width = o_ref.shape[1]
        o_ref[...] = _dot(h, w_main[:, col:col + width])
        col += width
    outs[3][...] = _dot(h, w_dt[...])
    outs[4][...] = _dot(h, w_gates[...])


def _layer_block(arr, layer):
    shape = arr.shape[1:]
    return pl.BlockSpec((None,) + shape, lambda *_: (layer,) + (0,) * len(shape))


def _inproj(x, y, y_offset, g, ws, layer):
    t = x.shape[0]
    has_add = y is not None
    widths = [4 * RET_W, SSD_W, CONV_DIM, LANE, 2 * D_MODEL]
    assert sum(widths[:3]) == sum(IN_SIZES[:6])
    row = lambda w: pl.BlockSpec((ROW_TILE, w), lambda i: (i, 0))
    in_specs = [row(D_MODEL)] + ([_token_spec(y_offset)] if has_add else []) + [_full((1, D_MODEL))]
    in_specs += [_layer_block(w, layer) for w in ws]
    out_shape = [jax.ShapeDtypeStruct((t, w), F32) for w in widths]
    out_specs = [row(w) for w in widths]
    if has_add:
        out_shape = [jax.ShapeDtypeStruct((t, D_MODEL), F32)] + out_shape
        out_specs = [row(D_MODEL)] + out_specs
    block_bytes = (sum(_nbytes(w.shape[1:], BF16) for w in ws)
                   + _nbytes((ROW_TILE, sum(widths) + 3 * D_MODEL), F32))
    args = ([x, y] if has_add else [x]) + [g] + list(ws)
    return pl.pallas_call(
        functools.partial(_inproj_body, has_add),
        grid=(t // ROW_TILE,), in_specs=in_specs, out_specs=out_specs, out_shape=out_shape,
        compiler_params=_params(("arbitrary",), block_bytes), name="inproj",
    )(*args)


def _retention_body(c, nb, nc, has_state, n_prev, *refs):
    refs = list(refs)
    qkvr_ref, cos_ref, sin_ref, dmat_ref, qdec_ref, kdec_ref, cdec_ref, gn_ref = refs[:8]
    refs = refs[8:]
    if has_state:
        s0_ref = refs.pop(0)
    prev_refs = [refs.pop(0) for _ in range(n_prev)]
    o_ref, ns_ref, st_scr = refs
    ci = pl.program_id(1)

    @pl.when(ci == 0)
    def _():
        if has_state:
            st_scr[...] = s0_ref[...]
        else:
            st_scr[...] = jnp.zeros(st_scr.shape, F32)

    cosv, sinv = cos_ref[...], sin_ref[...]
    for s in range(nb):
        for h in range(H_RET):
            col = lambda j: slice(j * RET_W + h * DK_RET, j * RET_W + (h + 1) * DK_RET)
            q, k = qkvr_ref[s, :, col(0)], qkvr_ref[s, :, col(1)]
            v, rg = qkvr_ref[s, :, col(2)], qkvr_ref[s, :, col(3)]
            qr = q * cosv + pltpu.roll(q, DK_RET // 2, axis=1) * sinv
            kr = (k * cosv + pltpu.roll(k, DK_RET // 2, axis=1) * sinv) * (DK_RET ** -0.5)
            qb, kb, vb = qr.astype(BF16), kr.astype(BF16), v.astype(BF16)
            state = st_scr[s, h]
            sc = _dot_nt(qb, kb) * dmat_ref[h]
            o = _dot(sc.astype(BF16), vb) + _dot(qb, state.astype(BF16)) * qdec_ref[h]
            st_scr[s, h] = cdec_ref[h] * state + _dot_tn((kr * kdec_ref[h]).astype(BF16), vb)
            mu = jnp.mean(o, axis=-1, keepdims=True)
            d = o - mu
            var = jnp.mean(d * d, axis=-1, keepdims=True)
            on = d * lax.rsqrt(var + EPS) * gn_ref[:, h * DV_RET:(h + 1) * DV_RET]
            o_ref[s, :, h * DV_RET:(h + 1) * DV_RET] = (jax.nn.silu(rg) * on).astype(o_ref.dtype)

    @pl.when(ci == nc - 1)
    def _():
        if n_prev:
            for l, prev_ref in enumerate(prev_refs):
                ns_ref[l] = prev_ref[...]
            ns_ref[n_prev] = st_scr[...]
        else:
            ns_ref[...] = st_scr[...]


def _state_out(prev, batch, nb, shape):
    zeros = (0,) * len(shape)
    if prev:
        depth = len(prev) + 1
        return (pl.BlockSpec((depth, nb) + shape, lambda b, ci: (0, b) + zeros),
                jax.ShapeDtypeStruct((depth, batch) + shape, F32))
    return pl.BlockSpec((nb,) + shape, lambda b, ci: (b,) + zeros), jax.ShapeDtypeStruct((batch,) + shape, F32)


def _retention(qkvr, start, gn_g, states, layer, prev, batch, length):
    c = math.gcd(length, CHUNK)
    nc = length // c
    nb = PROMPT_SEQS_PER_STEP if nc > 1 else SAMPLE_SEQS_PER_STEP
    assert batch % nb == 0
    has_state = states is not None
    t = batch * length
    half = DK_RET // 2
    pos = start + jnp.arange(length, dtype=F32)
    inv = ROPE_BASE ** (-jnp.arange(half, dtype=F32) / half)
    ang = pos[:, None] * inv[None, :]
    cosv = jnp.concatenate([jnp.cos(ang), jnp.cos(ang)], axis=1)
    sinv = jnp.concatenate([-jnp.sin(ang), jnp.sin(ang)], axis=1)
    log_g = jnp.log1p(-jnp.exp2(-5.0 - jnp.arange(H_RET, dtype=F32)))
    idx = jnp.arange(c, dtype=F32)
    rel = idx[:, None] - idx[None, :]
    causal = rel >= 0
    dmat = jnp.where(causal[None], jnp.exp(jnp.where(causal, rel, 0.0)[None] * log_g[:, None, None]), 0.0)
    qdec = jnp.broadcast_to(jnp.exp((idx[None, :] + 1.0) * log_g[:, None])[:, :, None], (H_RET, c, LANE))
    kdec = jnp.broadcast_to(jnp.exp((c - 1.0 - idx[None, :]) * log_g[:, None])[:, :, None], (H_RET, c, LANE))
    cdec = jnp.exp(c * log_g)

    rows = nb * c
    state_spec = pl.BlockSpec((nb, H_RET, DK_RET, DV_RET), lambda b, ci: (b, 0, 0, 0))
    seq_rows = lambda w: pl.BlockSpec((nb, c, w), lambda b, ci: (b, ci, 0))
    in_specs = [
        seq_rows(4 * RET_W),
        pl.BlockSpec((c, LANE), lambda b, ci: (ci, 0)),
        pl.BlockSpec((c, LANE), lambda b, ci: (ci, 0)),
        _full((H_RET, c, c)), _full((H_RET, c, LANE)), _full((H_RET, c, LANE)),
        pl.BlockSpec(memory_space=pltpu.SMEM),
        _full((1, RET_W)),
    ]
    args = [qkvr.reshape(batch, length, 4 * RET_W), cosv, sinv, dmat, qdec, kdec, cdec, gn_g]
    if has_state:
        in_specs.append(pl.BlockSpec((None, nb, H_RET, DK_RET, DV_RET), lambda b, ci: (layer, b, 0, 0, 0)))
        args.append(states)
    in_specs += [state_spec] * len(prev)
    args += list(prev)
    block_bytes = (_nbytes((rows, 4 * RET_W + 2 * LANE + RET_W), F32)
                   + (3 + 2 * len(prev)) * _nbytes((nb, H_RET, DK_RET, DV_RET), F32)
                   + 3 * _nbytes((H_RET, c, LANE), F32))
    ns_spec, ns_shape = _state_out(prev, batch, nb, (H_RET, DK_RET, DV_RET))
    out, new_state = pl.pallas_call(
        functools.partial(_retention_body, c, nb, nc, has_state, len(prev)),
        grid=(batch // nb, nc), in_specs=in_specs,
        out_specs=[seq_rows(RET_W), ns_spec],
        out_shape=[jax.ShapeDtypeStruct((batch, length, RET_W), BF16), ns_shape],
        scratch_shapes=[pltpu.VMEM((nb, H_RET, DK_RET, DV_RET), F32)],
        compiler_params=_params(("arbitrary", "arbitrary"), block_bytes), name="retention",
    )(*args)
    return out.reshape(t, RET_W), new_state


def _ssd_body(c, nb, nc, has_state, n_prev, *refs):
    refs = list(refs)
    (xbc_ref, z_ref, dt_ref, cw_ref, cb_ref, dtb_ref, a_ref, dsk_ref, ng_ref,
     tri_ref, eye_ref, exp_ref, sel_ref) = refs[:13]
    refs = refs[13:]
    if has_state:
        sconv_ref, sssm_ref = refs.pop(0), refs.pop(0)
    prev_conv = [refs.pop(0) for _ in range(n_prev)]
    prev_ssm = [refs.pop(0) for _ in range(n_prev)]
    y_ref, nconv_ref, nssm_ref, ext_scr, st_scr = refs
    new_conv = nconv_ref.at[n_prev] if n_prev else nconv_ref
    new_ssm = nssm_ref.at[n_prev] if n_prev else nssm_ref
    ci = pl.program_id(1)
    pad = SUBLANE
    hist = CONV_K - 1

    @pl.when(ci == 0)
    def _():
        ext_scr[:, 0:pad, :] = jnp.zeros((nb, pad, CONV_DIM), F32)
        if has_state:
            ext_scr[:, pad - hist:pad, :] = sconv_ref[...]
            for s in range(nb):
                for g in range(G_SSD):
                    for r in range(R_SSD):
                        st_scr[s, g, r * P_SSD:(r + 1) * P_SSD, :] = sssm_ref[s, g * R_SSD + r]
        else:
            st_scr[...] = jnp.zeros(st_scr.shape, F32)

    causal = (lax.broadcasted_iota(jnp.int32, (c, c), 0) >= lax.broadcasted_iota(jnp.int32, (c, c), 1))
    left_half = (lax.broadcasted_iota(jnp.int32, (c, SSD_W), 1) % LANE) < P_SSD
    tri, eye, expand = tri_ref[...], eye_ref[...], exp_ref[...]

    for s in range(nb):
        rows = slice(s * c, (s + 1) * c)
        ext_scr[s, pad:pad + c, :] = xbc_ref[rows, :]
        acc = ext_scr[s, pad - hist:pad - hist + c, :] * cw_ref[0:1, :]
        for j in range(1, CONV_K):
            acc = acc + ext_scr[s, pad - hist + j:pad - hist + j + c, :] * cw_ref[j:j + 1, :]
        conv = jax.nn.silu(cb_ref[...] + acc)

        @pl.when(ci == nc - 1)
        def _():
            new_conv[s] = ext_scr[s, pad + c - hist:pad + c, :]

        ext_scr[s, 0:pad, :] = ext_scr[s, c:c + pad, :]

        xs = conv[:, :SSD_W]
        bm = conv[:, SSD_W:SSD_W + G_SSD * N_SSD]
        cm = conv[:, SSD_W + G_SSD * N_SSD:]
        dt = jax.nn.softplus(dt_ref[rows, :] + dtb_ref[...])
        dta = dt * a_ref[...]
        cum = sum(_dot(tri, p) for p in _split3(dta))
        cum_parts = _split3(cum)
        cum_t = sum(_dot_nt(eye, p) for p in cum_parts)
        cum_e = sum(_dot(p, expand) for p in cum_parts)
        dt_e = sum(_dot(p, expand) for p in _split3(dt))
        last_e = cum_e[c - 1:c, :]
        xdt = xs * dt_e
        xw = (xs * (jnp.exp(last_e - cum_e) * dt_e)).astype(BF16)
        x_l = jnp.where(left_half, xdt, 0.0).astype(BF16)
        x_r = jnp.where(left_half, 0.0, xdt).astype(BF16)

        y_parts, off_parts = [], []
        for g in range(G_SSD):
            bm_g = bm[:, g * N_SSD:(g + 1) * N_SSD].astype(BF16)
            cm_g = cm[:, g * N_SSD:(g + 1) * N_SSD].astype(BF16)
            cb = _dot_nt(cm_g, bm_g)
            state = st_scr[s, g]
            off_parts.append(_dot_nt(cm_g, state.astype(BF16)))
            decay = jnp.exp(jnp.sum(sel_ref[g] * cum[c - 1:c, :], axis=1, keepdims=True))
            st_scr[s, g] = decay * state + _dot_tn(xw[:, g * GRP_W:(g + 1) * GRP_W], bm_g)
            for jj in range(R_SSD // 2):
                j = g * (R_SSD // 2) + jj
                w = []
                for hd in (2 * j, 2 * j + 1):
                    seg = cum[:, hd:hd + 1] - cum_t[hd:hd + 1, :]
                    w.append((jnp.exp(jnp.where(causal, seg, -jnp.inf)) * cb).astype(BF16))
                cols = slice(j * LANE, (j + 1) * LANE)
                y_parts.append(_dot(w[0], x_l[:, cols]) + _dot(w[1], x_r[:, cols]))
        y = jnp.concatenate(y_parts, axis=1) + jnp.exp(cum_e) * jnp.concatenate(off_parts, axis=1)
        y = (y + dsk_ref[...] * xs) * jax.nn.silu(z_ref[rows, :])
        normed = []
        for g in range(G_SSD):
            yg = y[:, g * GRP_W:(g + 1) * GRP_W]
            normed.append(yg * lax.rsqrt(jnp.mean(yg * yg, axis=-1, keepdims=True) + EPS))
        y_ref[rows, :] = (jnp.concatenate(normed, axis=1) * ng_ref[...]).astype(y_ref.dtype)

    @pl.when(ci == nc - 1)
    def _():
        for l in range(n_prev):
            nconv_ref[l] = prev_conv[l][...]
            nssm_ref[l] = prev_ssm[l][...]
        for s in range(nb):
            for g in range(G_SSD):
                for r in range(R_SSD):
                    new_ssm[s, g * R_SSD + r] = st_scr[s, g, r * P_SSD:(r + 1) * P_SSD, :]


def _ssd(xbc, z, dt, lp, states, layer, prev, batch, length):
    c = math.gcd(length, CHUNK)
    nc = length // c
    nb = 1 if nc > 1 else SAMPLE_SEQS_PER_STEP
    assert batch % nb == 0 and c % SUBLANE == 0
    has_state = states is not None
    t = batch * length
    rows = nb * c
    tri = (jnp.arange(c)[:, None] >= jnp.arange(c)[None, :]).astype(BF16)
    eye = jnp.eye(LANE, dtype=BF16)
    head_of_ch = jnp.arange(SSD_W) // P_SSD
    expand = (jnp.arange(LANE)[:, None] == head_of_ch[None, :]).astype(BF16)
    sel = (head_of_ch[:, None] == jnp.arange(LANE)[None, :]).astype(F32).reshape(G_SSD, GRP_W, LANE)

    seq = lambda shape: pl.BlockSpec((nb,) + shape, lambda b, ci: (b,) + (0,) * len(shape))
    rowb = lambda w: pl.BlockSpec((rows, w), lambda b, ci: (b * nc + ci, 0))
    in_specs = [rowb(CONV_DIM), rowb(SSD_W), rowb(LANE),
                _full((CONV_K, CONV_DIM)), _full((1, CONV_DIM)), _full((1, LANE)), _full((1, LANE)),
                _full((1, SSD_W)), _full((1, SSD_W)),
                _full((c, c)), _full((LANE, LANE)), _full((LANE, SSD_W)), _full((G_SSD, GRP_W, LANE))]
    args = [xbc, z, dt, lp["conv_w"], lp["conv_b"], lp["dt_bias"], lp["a"], lp["d_skip"], lp["ssd_norm_g"],
            tri, eye, expand, sel]
    conv_shape, ssm_shape = (CONV_K - 1, CONV_DIM), (H_SSD, P_SSD, N_SSD)
    if has_state:
        stacked = lambda shape: pl.BlockSpec((None, nb) + shape, lambda b, ci: (layer, b) + (0,) * len(shape))
        in_specs += [stacked(conv_shape), stacked(ssm_shape)]
        args += list(states)
    in_specs += [seq(conv_shape)] * len(prev) + [seq(ssm_shape)] * len(prev)
    args += [p[0] for p in prev] + [p[1] for p in prev]
    block_bytes = (_nbytes((rows, CONV_DIM + 2 * SSD_W + LANE), F32) + _nbytes((nb, c + SUBLANE, CONV_DIM), F32)
                   + (3 + 4 * len(prev)) * _nbytes((nb, SSD_W, LANE), F32) + _nbytes((LANE, SSD_W), F32)
                   + 12 * _nbytes((c, SSD_W), F32))
    nconv_spec, nconv_shape = _state_out(prev, batch, nb, conv_shape)
    nssm_spec, nssm_shape = _state_out(prev, batch, nb, ssm_shape)
    return pl.pallas_call(
        functools.partial(_ssd_body, c, nb, nc, has_state, len(prev)),
        grid=(batch // nb, nc), in_specs=in_specs,
        out_specs=[rowb(SSD_W), nconv_spec, nssm_spec],
        out_shape=[jax.ShapeDtypeStruct((t, SSD_W), BF16), nconv_shape, nssm_shape],
        scratch_shapes=[pltpu.VMEM((nb, c + SUBLANE, CONV_DIM), F32),
                        pltpu.VMEM((nb, G_SSD, GRP_W, N_SSD), F32)],
        compiler_params=_params(("arbitrary", "arbitrary"), block_bytes), name="ssd",
    )(*args)


def _ssd_dec_pre_body(n_prev, x_ref, dt_ref, cw_ref, cb_ref, sconv_ref, *rest):
    prev, (xt_ref, dtt_ref, nconv_ref) = rest[:n_prev], rest[n_prev:]
    length, hist = x_ref.shape[0], CONV_K - 1
    rows = [sconv_ref[j] for j in range(hist)] + [x_ref[m] for m in range(length)]
    for m in range(length):
        acc = rows[m] * cw_ref[0:1, :]
        for j in range(1, CONV_K):
            acc = acc + rows[m + j] * cw_ref[j:j + 1, :]
        conv = jax.nn.silu(cb_ref[...] + acc)
        for k in range(CONV_DIM // LANE):
            xt_ref[m, k * LANE:(k + 1) * LANE, :] = conv[:, k * LANE:(k + 1) * LANE].T
        dtt_ref[m] = dt_ref[m].T
    new_conv = nconv_ref.at[n_prev] if n_prev else nconv_ref
    for l in range(n_prev):
        nconv_ref[l] = prev[l][...]
    for j in range(hist):
        new_conv[j] = rows[length + j]


def _ssd_dec_head_body(n_prev, dtb_ref, a_ref, dsk_ref, x_ref, b_ref, c_ref, dt_ref, s_ref, *rest):
    prev, (y_ref, ns_ref, xw_scr, off_scr) = rest[:n_prev], rest[n_prev:]
    h = pl.program_id(0)
    length = x_ref.shape[0]
    new_state = ns_ref.at[n_prev] if n_prev else ns_ref
    for l in range(n_prev):
        ns_ref[l] = prev[l][...]
    dt = [jax.nn.softplus(dt_ref[m, pl.ds(h, 1), :] + dtb_ref[h]) for m in range(length)]
    cum = []
    for m in range(length):
        cum.append(dt[m] * a_ref[h] if m == 0 else cum[-1] + dt[m] * a_ref[h])
    last = cum[-1]
    decay = jnp.exp(last)
    for j in range(length):
        xw_scr[j] = x_ref[j] * (jnp.exp(last - cum[j]) * dt[j])

    def per_state_row(p, carry):
        sp = s_ref[p]
        new = decay * sp
        for j in range(length):
            new = new + xw_scr[j, pl.ds(p, 1), :] * b_ref[j]
        new_state[p] = new
        for m in range(length):
            off_scr[m, pl.ds(p, 1), :] = jnp.sum(c_ref[m] * sp, axis=0, keepdims=True)
        return carry

    lax.fori_loop(0, P_SSD, per_state_row, 0)
    for m in range(length):
        y = jnp.exp(cum[m]) * off_scr[m] + dsk_ref[h] * x_ref[m]
        for j in range(m + 1):
            cb = jnp.sum(c_ref[m] * b_ref[j], axis=0, keepdims=True)
            y = y + (jnp.exp(cum[m] - cum[j]) * cb * dt[j]) * x_ref[j]
        y_ref[m] = y


def _ssd_dec_post_body(yt_ref, z_ref, ng_ref, o_ref):
    y = jnp.concatenate([yt_ref[k * LANE:(k + 1) * LANE, :].T for k in range(SSD_W // LANE)], axis=1)
    y = y * jax.nn.silu(z_ref[...])
    normed = []
    for g in range(G_SSD):
        yg = y[:, g * GRP_W:(g + 1) * GRP_W]
        normed.append(yg * lax.rsqrt(jnp.mean(yg * yg, axis=-1, keepdims=True) + EPS))
    o_ref[...] = (jnp.concatenate(normed, axis=1) * ng_ref[...]).astype(o_ref.dtype)


def _ssd_decode(xbc, z, dt, lp, states, layer, prev, batch, length):
    assert batch == LANE
    t = batch * length
    depth_out = len(prev) + 1
    tok = lambda a, w: a.reshape(batch, length, w).transpose(1, 0, 2)
    conv_in = states[0].transpose(0, 2, 1, 3)
    ssm_in = states[1].transpose(0, 2, 3, 4, 1)
    hist = CONV_K - 1
    stack = lambda shape: ((depth_out,) + shape) if prev else shape

    pre_in = [_full((length, batch, CONV_DIM)), _full((length, batch, LANE)), _full((CONV_K, CONV_DIM)),
              _full((1, CONV_DIM)),
              pl.BlockSpec((None, hist, batch, CONV_DIM), lambda i: (layer, 0, 0, 0))]
    pre_in += [_full((hist, batch, CONV_DIM))] * len(prev)
    xt, dtt, new_conv = pl.pallas_call(
        functools.partial(_ssd_dec_pre_body, len(prev)), grid=(1,), in_specs=pre_in,
        out_specs=[_full((length, CONV_DIM, batch)), _full((length, LANE, batch)),
                   _full(stack((hist, batch, CONV_DIM)))],
        out_shape=[jax.ShapeDtypeStruct((length, CONV_DIM, batch), F32),
                   jax.ShapeDtypeStruct((length, LANE, batch), F32),
                   jax.ShapeDtypeStruct(stack((hist, batch, CONV_DIM)), F32)],
        compiler_params=_params(("arbitrary",), 4 * _nbytes((length, batch, CONV_DIM), F32)), name="ssd_dec_conv",
    )(tok(xbc, CONV_DIM), tok(dt, LANE), lp["conv_w"], lp["conv_b"], conv_in, *[p[0] for p in prev])

    head_rows = lambda first: pl.BlockSpec((length, P_SSD, batch), lambda h: (0, first(h), 0))
    smem = pl.BlockSpec(memory_space=pltpu.SMEM)
    state_block = (P_SSD, N_SSD, batch)
    head_in = [smem, smem, smem,
               head_rows(lambda h: h),
               head_rows(lambda h: SSD_W // P_SSD + h // R_SSD),
               head_rows(lambda h: (SSD_W + G_SSD * N_SSD) // P_SSD + h // R_SSD),
               _full((length, LANE, batch)),
               pl.BlockSpec((None, None) + state_block, lambda h: (layer, h, 0, 0, 0))]
    head_in += [pl.BlockSpec((None,) + state_block, lambda h: (h, 0, 0, 0))] * len(prev)
    if prev:
        ns_spec = pl.BlockSpec((depth_out, None) + state_block, lambda h: (0, h, 0, 0, 0))
    else:
        ns_spec = pl.BlockSpec((None,) + state_block, lambda h: (h, 0, 0, 0))
    heads = lambda v: v.reshape(LANE)[:H_SSD]
    yt, new_ssm = pl.pallas_call(
        functools.partial(_ssd_dec_head_body, len(prev)), grid=(H_SSD,), in_specs=head_in,
        out_specs=[head_rows(lambda h: h), ns_spec],
        out_shape=[jax.ShapeDtypeStruct((length, SSD_W, batch), F32),
                   jax.ShapeDtypeStruct(stack((H_SSD,) + state_block), F32)],
        scratch_shapes=[pltpu.VMEM((length, P_SSD, batch), F32), pltpu.VMEM((length, P_SSD, batch), F32)],
        compiler_params=_params(("arbitrary",), (4 + 2 * depth_out) * _nbytes(state_block, F32)), name="ssd_dec_heads",
    )(heads(lp["dt_bias"]), heads(lp["a"]), lp["d_skip"].reshape(H_SSD, P_SSD)[:, 0], xt, xt, xt, dtt, ssm_in,
      *[p[1] for p in prev])

    y = pl.pallas_call(
        _ssd_dec_post_body, grid=(length,),
        in_specs=[pl.BlockSpec((None, SSD_W, batch), lambda m: (m, 0, 0)),
                  pl.BlockSpec((None, batch, SSD_W), lambda m: (m, 0, 0)), _full((1, SSD_W))],
        out_specs=pl.BlockSpec((None, batch, SSD_W), lambda m: (m, 0, 0)),
        out_shape=jax.ShapeDtypeStruct((length, batch, SSD_W), BF16),
        compiler_params=_params(("arbitrary",), 3 * _nbytes((batch, SSD_W), F32)), name="ssd_dec_gate",
    )(yt, tok(z, SSD_W), lp["ssd_norm_g"])
    y = y.transpose(1, 0, 2).reshape(t, SSD_W)
    if prev:
        new_conv = new_conv.transpose(0, 2, 1, 3)
        new_ssm = new_ssm.transpose(0, 4, 1, 2, 3)
    return y, new_conv, new_ssm


def _first_argmax(vals):
    best = vals[0]
    for v in vals[1:]:
        best = jnp.maximum(best, v)
    idx = jnp.full(best.shape, len(vals) - 1, jnp.int32)
    for i in range(len(vals) - 2, -1, -1):
        idx = jnp.where(vals[i] == best, i, idx)
    return best, idx


def _merge_body(bounds, *refs):
    n_groups = len(bounds)
    acts, refs = refs[:4 * n_groups], refs[4 * n_groups:]
    weights, refs = refs[:7], refs[7:]
    x2_refs, shared = refs[:n_groups], refs[n_groups:]
    i = pl.program_id(0)
    for gi, (first, count) in enumerate(bounds):
        @pl.when((i >= first) & (i < first + count))
        def _():
            _merge_tile(*acts[4 * gi:4 * gi + 4], *weights, x2_refs[gi], *shared)


def _merge_tile(ret_ref, ssd_ref, gates_ref, x_ref, wr_ref, ws_ref, wo_ref, g2_ref, rhi_ref, rlo_ref, upp_ref,
                x2_ref, hw_ref, route_ref, cnt_ref):
    ret_y = _dot(ret_ref[...], wr_ref[...])
    ssd_y = _dot(ssd_ref[...], ws_ref[...])
    m = (jax.nn.sigmoid(gates_ref[:, :D_MODEL]) * ret_y + jax.nn.sigmoid(gates_ref[:, D_MODEL:]) * ssd_y)
    x2 = x_ref[...] + _dot(m.astype(BF16), wo_ref[...])
    x2_ref[...] = x2
    h2 = _rmsnorm(x2, g2_ref[...])
    bits = pltpu.bitcast(h2.astype(BF16).astype(F32), jnp.uint32)
    half = D_MODEL // 2
    word = lax.bitwise_or(lax.bitwise_and(bits[:, half:], jnp.uint32(HIGH_HALF)),
                          lax.shift_right_logical(bits[:, :half], jnp.uint32(16)))
    n_rows = h2.shape[0]
    for k in range(PACKED_ROWS):
        hw_ref[pl.ds(k, n_rows, stride=SUBLANE), :] = word[:, k * LANE:(k + 1) * LANE]

    h_hi = h2.astype(BF16)
    h_lo = (h2 - h_hi.astype(F32)).astype(BF16)
    logits = _dot_nt(rhi_ref[...], h_hi) + (_dot_nt(rlo_ref[...], h_hi) + _dot_nt(rhi_ref[...], h_lo))
    colv = lambda i: logits[i:i + 1, :]
    glog = [colv(i) for i in range(N_EGROUPS)]
    gmax, gidx = _first_argmax(glog)
    den = glog[0] * 0.0
    for v in glog:
        den = den + jnp.exp(v - gmax)
    gw = 1.0 / den
    sel = []
    for j in range(EXP_PER_GROUP):
        v = colv(N_EGROUPS + (N_EGROUPS - 1) * EXP_PER_GROUP + j)
        for g in range(N_EGROUPS - 2, -1, -1):
            v = jnp.where(gidx == g, colv(N_EGROUPS + g * EXP_PER_GROUP + j), v)
        sel.append(v)
    v1, i1 = _first_argmax(sel)
    v2, i2 = _first_argmax([jnp.where(i1 == j, -jnp.inf, sel[j]) for j in range(EXP_PER_GROUP)])
    e = jnp.exp(v2 - v1)
    w1 = gw / (1.0 + e)
    w2 = gw * e / (1.0 + e)
    first_low = i1 < i2
    lo = jnp.where(first_low, i1, i2)
    hi = jnp.where(first_low, i2, i1)
    pair = jnp.where(lo == 0, hi - 1, jnp.where(lo == 1, hi + 1, N_PAIRS - 1))
    cls = gidx * N_PAIRS + pair
    w_lo = jnp.where(first_low, w1, w2)
    w_hi = jnp.where(first_low, w2, w1)

    sub = lax.broadcasted_iota(jnp.int32, (META_W, n_rows), 0)
    onehot = jnp.where(sub == cls, 1.0, 0.0).astype(BF16)
    before = _dot(onehot, upp_ref[...])
    rank = jnp.sum(onehot.astype(F32) * before, axis=0, keepdims=True)
    meta = jnp.where(sub == 0, w_lo, jnp.where(sub == 1, w_hi, jnp.where(
        sub == 2, cls.astype(F32), jnp.where(sub == 3, rank, 0.0))))
    route_ref[0] = meta[0:SUBLANE, :]
    meta_bits = pltpu.bitcast(meta.T, jnp.uint32)
    for k in range(PACKED_ROWS, TOKEN_TILE_ROWS):
        hw_ref[pl.ds(k, n_rows, stride=SUBLANE), :] = meta_bits
    cnt_ref[0] = _dot_nt(jnp.ones((SUBLANE, n_rows), BF16), onehot)[0:1, :]


def _merge(group_acts, lp, big, layer):
    counts = [a[3].shape[0] // ROW_TILE for a in group_acts]
    bounds = tuple((sum(counts[:gi]), counts[gi]) for gi in range(len(counts)))
    n_tiles = sum(counts)
    total_rows = n_tiles * ROW_TILE
    upp = (jnp.arange(ROW_TILE)[:, None] < jnp.arange(ROW_TILE)[None, :]).astype(BF16)
    stacked = [big["w_ret_out"], big["w_ssd_out"], big["w_o"]]
    ws = [lp["ln2_g"], lp["router_hi"], lp["router_lo"], upp]
    block_bytes = (sum(_nbytes(w.shape[1:], w.dtype) for w in stacked) + sum(_nbytes(w.shape, w.dtype) for w in ws)
                   + len(counts) * (_nbytes((ROW_TILE, RET_W + SSD_W), BF16) + _nbytes((ROW_TILE, 4 * D_MODEL), F32))
                   + _nbytes((ROW_TILE, 3 * D_MODEL), F32))

    def group_row(width, first, count):
        return pl.BlockSpec((ROW_TILE, width), lambda i: (jnp.clip(i - first, 0, count - 1), 0))

    in_specs, args, x2_specs, x2_shapes = [], [], [], []
    for (first, count), acts in zip(bounds, group_acts, strict=True):
        in_specs += [group_row(w, first, count) for w in (RET_W, SSD_W, 2 * D_MODEL, D_MODEL)]
        args += list(acts)
        x2_specs.append(group_row(D_MODEL, first, count))
        x2_shapes.append(jax.ShapeDtypeStruct((count * ROW_TILE, D_MODEL), F32))
    in_specs += [_layer_block(w, layer) for w in stacked] + [_full(w.shape) for w in ws]
    args += stacked + ws
    outs = pl.pallas_call(
        functools.partial(_merge_body, bounds), grid=(n_tiles,), in_specs=in_specs,
        out_specs=x2_specs + [_token_spec(0), pl.BlockSpec((1, SUBLANE, ROW_TILE), lambda i: (i, 0, 0)),
                              pl.BlockSpec((1, 1, META_W), lambda i: (i, 0, 0))],
        out_shape=x2_shapes + [jax.ShapeDtypeStruct((total_rows * TOKEN_TILE_ROWS, LANE), jnp.uint32),
                               jax.ShapeDtypeStruct((n_tiles, SUBLANE, ROW_TILE), F32),
                               jax.ShapeDtypeStruct((n_tiles, 1, META_W), F32)],
        compiler_params=_params(("arbitrary",), block_bytes), name="merge_router",
    )(*args)
    return outs[:len(counts)], outs[len(counts):]


PLAN_ROWS = LANE
TAB_ROWS = SUBLANE
TAB_LO, TAB_HI, TAB_USED = 0, 1, 2
BYTE = 256.0
MOE_RING = 3


def _row_of(col_vals):
    pick = (lax.broadcasted_iota(jnp.int32, (TAB_ROWS, LANE), 0)
            == lax.broadcasted_iota(jnp.int32, (TAB_ROWS, LANE), 1)).astype(BF16)
    hi = jnp.floor(col_vals * (1.0 / BYTE))
    lo = col_vals - hi * BYTE
    return BYTE * _dot_nt(pick, hi.astype(BF16)) + _dot_nt(pick, lo.astype(BF16))


def _plan_body(tile, n_row_tiles, route_ref, cnt_ref, low_ref, upp_ref, pos_ref, tab_ref, offs_scr):
    cnt = cnt_ref[...]
    tot = jnp.sum(cnt, axis=0, keepdims=True)
    cls_tiles = jnp.floor((tot + (tile - 1.0)) * (1.0 / tile))
    first_tile = _dot(jnp.broadcast_to(cls_tiles, (SUBLANE, LANE)).astype(BF16), upp_ref[...])[0:1, :]
    base = first_tile * tile
    ends = base + cls_tiles * tile
    offs_scr[...] = base + _dot(low_ref[...], cnt.astype(BF16))

    def per_row_tile(j, carry):
        meta = route_ref[j]
        sub = lax.broadcasted_iota(jnp.int32, (LANE, ROW_TILE), 0).astype(F32)
        onehot = jnp.where(sub == meta[2:3, :], 1.0, 0.0).astype(BF16)
        offs = offs_scr[pl.ds(j, 1), :]
        hi = jnp.floor(offs * (1.0 / BYTE))
        rows8 = lambda v: jnp.broadcast_to(v, (SUBLANE, LANE)).astype(BF16)
        pos = BYTE * _dot(rows8(hi), onehot) + _dot(rows8(offs - hi * BYTE), onehot)
        pos_ref[j] = (pos[0:1, :] + meta[3:4, :]).astype(jnp.int32)
        return carry

    lax.fori_loop(0, n_row_tiles, per_row_tile, 0)

    lane = lax.broadcasted_iota(jnp.int32, (PLAN_ROWS, LANE), 1)
    tile_idx = lax.broadcasted_iota(jnp.int32, (PLAN_ROWS, LANE), 0).astype(F32)[:, 0:1]
    n_used = jnp.sum(cls_tiles, axis=1, keepdims=True)
    start = jnp.minimum(tile_idx, n_used - 1.0) * tile
    tcls = jnp.sum(jnp.where((lane < N_CLASSES) & (ends <= start), 1.0, 0.0), axis=1, keepdims=True)
    tcls = jnp.minimum(tcls, N_CLASSES - 1.0)
    group = sum(jnp.where(tcls >= k * N_PAIRS, 1.0, 0.0) for k in range(1, N_EGROUPS))
    pair = tcls - group * N_PAIRS
    p_lo = jnp.where(pair >= 3, 1.0, 0.0) + jnp.where(pair >= 5, 1.0, 0.0)
    p_hi = jnp.where(pair == 0, 1.0, jnp.where((pair == 1) | (pair == 3), 2.0, 3.0))
    table = jnp.where(lane == TAB_LO, group * EXP_PER_GROUP + p_lo,
                      jnp.where(lane == TAB_HI, group * EXP_PER_GROUP + p_hi,
                                jnp.where(lane == TAB_USED, n_used, 0.0)))
    tab_ref[...] = _row_of(table).astype(jnp.int32)


def _plan(route, counts, tile):
    nrt = route.shape[0]
    t = nrt * ROW_TILE
    assert nrt <= PLAN_ROWS and t // tile + N_CLASSES <= PLAN_ROWS
    cnt = jnp.pad(counts[:, 0, :], ((0, PLAN_ROWS - nrt), (0, 0)))
    idx = jnp.arange(PLAN_ROWS)
    low = (idx[:, None] > idx[None, :]).astype(BF16)
    upp = (idx[:, None] < idx[None, :]).astype(BF16)
    pos, tab = pl.pallas_call(
        functools.partial(_plan_body, tile, nrt),
        out_shape=[jax.ShapeDtypeStruct((nrt, 1, ROW_TILE), jnp.int32),
                   jax.ShapeDtypeStruct((TAB_ROWS, PLAN_ROWS), jnp.int32)],
        scratch_shapes=[pltpu.VMEM((PLAN_ROWS, LANE), F32)],
        compiler_params=pltpu.CompilerParams(vmem_limit_bytes=_vmem_limit(_nbytes(route.shape, F32))),
        name="moe_plan",
    )(route, cnt, low, upp)
    return pos.reshape(t), tab


def _moe_body(tile, n_tok, pos_ref, tab_ref, hw_hbm, spare_hbm, wgl_ref, wul_ref, wdl_ref, wgh_ref, wuh_ref,
              wdh_ref, y_hbm, buf, obuf, wgu_scr, wd_scr, tok_smem, gsem, ssem, isem):
    i = pl.program_id(0)
    n_used = tab_ref[TAB_USED, 0]
    ring = buf.shape[0]
    cur = lax.rem(i, ring)
    nxt1 = lax.rem(i + 1, ring)
    nxt2 = lax.rem(i + 2, ring)
    n_points = 16
    per_point = tile // n_points
    tr = TOKEN_TILE_ROWS

    def token_tile(ref, token):
        start = token * tr if isinstance(token, int) else pl.multiple_of(token * tr, tr)
        return ref.at[pl.ds(start, tr)]

    def dma_thread(r):
        return r % 2 if isinstance(r, int) else 0

    def gather_start(block, r, sl):
        src = jnp.minimum(tok_smem[block * tile + r], n_tok - 1)
        pltpu.make_async_copy(token_tile(hw_hbm, src), token_tile(buf.at[sl], r), gsem.at[sl]).start(dma_thread(r))

    def scatter_start(block, r, sl):
        dst = tok_smem[block * tile + r]
        pltpu.make_async_copy(token_tile(obuf.at[sl], r), token_tile(y_hbm, dst), ssem.at[sl]).start(dma_thread(r))

    def gather_wait(sl):
        pltpu.make_async_copy(hw_hbm.at[pl.ds(0, tile * tr)], buf.at[sl], gsem.at[sl]).wait()

    def scatter_wait(sl):
        pltpu.make_async_copy(obuf.at[sl], y_hbm.at[pl.ds(0, tile * tr)], ssem.at[sl]).wait()

    def for_rows(n, fn):
        lax.fori_loop(0, n, lambda r, carry: (fn(r), carry)[1], 0, unroll=8)

    def block_of(t):
        return jnp.minimum(t, n_used - 1) + 1

    @pl.when(i < n_used)
    def _():
        @pl.when(i == 0)
        def _():
            fill = pltpu.make_async_copy(spare_hbm, tok_smem, isem)
            fill.start()
            fill.wait()

            def place(t):
                tok_smem[tile + pos_ref[t]] = t
            for_rows(n_tok, place)
            obuf[ring - 1] = jnp.zeros(obuf.shape[1:], F32)
            for_rows(tile, lambda r: gather_start(block_of(0), r, 0))
            for_rows(tile, lambda r: gather_start(block_of(1), r, 1))

        before = jnp.maximum(i - 1, 0)
        for row, srcs in ((TAB_LO, (wgl_ref, wul_ref, wdl_ref)), (TAB_HI, (wgh_ref, wuh_ref, wdh_ref))):
            slot = 0 if row == TAB_LO else 1

            @pl.when((i == 0) | (tab_ref[row, i] != tab_ref[row, before]))
            def _():
                wgu_scr[2 * slot] = srcs[0][...].astype(BF16)
                wgu_scr[2 * slot + 1] = srcs[1][...].astype(BF16)
                wd_scr[slot] = srcs[2][...].astype(BF16)

        gather_wait(cur)
        issued = [0]

        def issue_point():
            for r in range(issued[0] * per_point, (issued[0] + 1) * per_point):
                gather_start(block_of(i + 2), r, nxt2)
                scatter_start(i, r, nxt2)
            issued[0] += 1

        def tile_row(k):
            return buf[cur, pl.ds(k, tile, stride=tr), :]

        words = [tile_row(k) for k in range(PACKED_ROWS)]
        low = [pltpu.bitcast(lax.shift_left(w, jnp.uint32(16)), F32).astype(BF16) for w in words]
        high = [pltpu.bitcast(lax.bitwise_and(w, jnp.uint32(HIGH_HALF)), F32).astype(BF16) for w in words]
        xb = jnp.concatenate(low + high, axis=1)
        meta = pltpu.bitcast(tile_row(PACKED_ROWS), F32)
        w_lo, w_hi = meta[:, 0:1], meta[:, 1:2]
        half = D_FF // 2
        acts = []
        for slot in range(2):
            parts = []
            for n in range(2):
                gate = _dot(xb, wgu_scr[2 * slot, :, n * half:(n + 1) * half])
                issue_point()
                up = _dot(xb, wgu_scr[2 * slot + 1, :, n * half:(n + 1) * half])
                issue_point()
                parts.append((jax.nn.silu(gate) * up).astype(BF16))
            acts.append(jnp.concatenate(parts, axis=1))
        quarter = D_MODEL // 4
        for n in range(4):
            cols = slice(n * quarter, (n + 1) * quarter)
            d_lo = _dot(acts[0], wd_scr[0, :, cols])
            issue_point()
            d_hi = _dot(acts[1], wd_scr[1, :, cols])
            issue_point()
            out = d_lo * w_lo + d_hi * w_hi
            for kk in range(quarter // LANE):
                obuf[cur, pl.ds(n * (quarter // LANE) + kk, tile, stride=tr), :] = out[:, kk * LANE:(kk + 1) * LANE]
        assert issued[0] == n_points

        @pl.when(i >= 1)
        def _():
            scatter_wait(nxt1)

        @pl.when(i == n_used - 1)
        def _():
            scatter_wait(nxt2)
            for_rows(tile, lambda r: scatter_start(i + 1, r, cur))
            scatter_wait(cur)
            gather_wait(nxt1)
            gather_wait(nxt2)


def _moe(hw, pos, tab, big, layer, tile):
    t = hw.shape[0] // TOKEN_TILE_ROWS
    n_tiles = t // tile + N_CLASSES
    wspec = lambda shape, row: pl.BlockSpec((None, None) + shape, lambda i, pos, tab: (layer, tab[row, i], 0, 0))
    in_specs = [pl.BlockSpec(memory_space=pl.ANY)] * 2
    spare = t + jnp.arange((n_tiles + 1) * tile, dtype=jnp.int32) % tile
    for row in (TAB_LO, TAB_HI):
        in_specs += [wspec((D_MODEL, D_FF), row), wspec((D_MODEL, D_FF), row), wspec((D_FF, D_MODEL), row)]
    block_bytes = 6 * _nbytes((D_MODEL, D_FF), F32) + 3 * _nbytes((D_MODEL, D_FF), BF16) + 3 * _nbytes((tile, D_MODEL), F32)
    tile_rows = tile * TOKEN_TILE_ROWS
    grid_spec = pltpu.PrefetchScalarGridSpec(
        num_scalar_prefetch=2, grid=(n_tiles,), in_specs=in_specs,
        out_specs=pl.BlockSpec(memory_space=pl.ANY),
        scratch_shapes=[pltpu.VMEM((MOE_RING, tile_rows, LANE), jnp.uint32),
                        pltpu.VMEM((MOE_RING, tile_rows, LANE), F32),
                        pltpu.VMEM((4, D_MODEL, D_FF), BF16), pltpu.VMEM((2, D_FF, D_MODEL), BF16),
                        pltpu.SMEM(((n_tiles + 1) * tile,), jnp.int32),
                        pltpu.SemaphoreType.DMA((MOE_RING,)), pltpu.SemaphoreType.DMA((MOE_RING,)),
                        pltpu.SemaphoreType.DMA(())])
    return pl.pallas_call(
        functools.partial(_moe_body, tile, t), grid_spec=grid_spec,
        out_shape=jax.ShapeDtypeStruct(((t + tile) * TOKEN_TILE_ROWS, LANE), F32),
        compiler_params=_params(("arbitrary",), block_bytes), name="moe",
    )(pos, tab, hw, spare, big["w_gate"], big["w_up"], big["w_down"], big["w_gate"], big["w_up"], big["w_down"])


FINAL_TILE = 1024


def _final_body(x_ref, y_ref, g_ref, o_ref):
    o_ref[...] = _rmsnorm(x_ref[...] + _token_rows(y_ref, FINAL_TILE), g_ref[...])


def _final_norm(x, y, y_offset, g):
    t = x.shape[0]
    row = pl.BlockSpec((FINAL_TILE, D_MODEL), lambda i: (i, 0))
    return pl.pallas_call(
        _final_body, grid=(t // FINAL_TILE,),
        in_specs=[row, _token_spec(y_offset, FINAL_TILE), _full((1, D_MODEL))],
        out_specs=row, out_shape=jax.ShapeDtypeStruct((t, D_MODEL), F32),
        compiler_params=_params(("arbitrary",), 3 * _nbytes((FINAL_TILE, D_MODEL), F32)), name="final_norm",
    )(x, y, g)


def _layer_params(i, ln1_g, ret_gn_g, conv_w, conv_b, dt_bias, a_log, d_skip, ssd_norm_g, ln2_g, w_rg, w_re):
    pad_heads = lambda v: jnp.pad(v.astype(F32), (0, LANE - H_SSD)).reshape(1, LANE)
    router = jnp.pad(jnp.concatenate([w_rg[i], w_re[i]], axis=1),
                     ((0, 0), (0, META_W - N_EGROUPS - N_EXPERTS)))
    router_hi = router.astype(BF16)
    return {
        "ln1_g": ln1_g[i].reshape(1, D_MODEL),
        "ret_gn_g": ret_gn_g[i].reshape(1, RET_W),
        "conv_w": conv_w[i], "conv_b": conv_b[i].reshape(1, CONV_DIM),
        "dt_bias": pad_heads(dt_bias[i]), "a": pad_heads(-jnp.exp(a_log[i].astype(F32))),
        "d_skip": jnp.repeat(d_skip[i].astype(F32), P_SSD).reshape(1, SSD_W),
        "ssd_norm_g": ssd_norm_g[i].reshape(1, SSD_W),
        "ln2_g": ln2_g[i].reshape(1, D_MODEL),
        "router_hi": router_hi.T, "router_lo": (router - router_hi.astype(F32)).astype(BF16).T,
    }


def _stacked_params(w_in, w_ret_out, w_ssd_out, w_o, w_gate, w_up, w_down):
    offs = [0]
    for s in IN_SIZES:
        offs.append(offs[-1] + s)
    return {
        "w_in": [w_in.astype(BF16),
                 jnp.pad(w_in[:, :, offs[6]:offs[7]], ((0, 0), (0, 0), (0, LANE - H_SSD))).astype(BF16),
                 w_in[:, :, offs[7]:offs[9]].astype(BF16)],
        "w_ret_out": w_ret_out.astype(BF16), "w_ssd_out": w_ssd_out.astype(BF16), "w_o": w_o.astype(BF16),
        "w_gate": w_gate, "w_up": w_up, "w_down": w_down,
    }


class _Group:
    def __init__(self, x, start, states, row_offset):
        self.batch, self.length, _ = x.shape
        self.x = x.reshape(self.batch * self.length, D_MODEL)
        self.start, self.states, self.row_offset = start, states, row_offset
        self.rets, self.ssms, self.convs = [], [], []


def kernel(x_prompt, x_sample, state_ret, state_ssm, state_conv, ln1_g, w_in, ret_gn_g, w_ret_out, conv_w, conv_b, dt_bias, a_log, d_skip, ssd_norm_g, w_ssd_out, w_o, ln2_g, w_router_group, w_router_expert, w_e_gate, w_e_up, w_e_down, lnf_g):
    depth = w_in.shape[0]
    past_len = 16384.0
    layers = [_layer_params(i, ln1_g, ret_gn_g, conv_w, conv_b, dt_bias, a_log, d_skip, ssd_norm_g, ln2_g,
                            w_router_group, w_router_expert) for i in range(depth)]
    big = _stacked_params(w_in, w_ret_out, w_ssd_out, w_o, w_e_gate, w_e_up, w_e_down)
    n_prompt = x_prompt.shape[0] * x_prompt.shape[1]
    groups = [_Group(x_prompt, 0.0, None, 0),
              _Group(x_sample, past_len, (state_ret, state_ssm, state_conv), n_prompt)]
    y = None
    for i, lp in enumerate(layers):
        group_acts = []
        last = i == depth - 1
        for grp in groups:
            outs = _inproj(grp.x, y, grp.row_offset, lp["ln1_g"], big["w_in"], i)
            if y is not None:
                grp.x, outs = outs[0], outs[1:]
            qkvr, z, xbc, dt, gates = outs
            ret_state = None if grp.states is None else grp.states[0]
            ssd_state = None if grp.states is None else (grp.states[2], grp.states[1])
            ret, new_ret = _retention(qkvr, grp.start, lp["ret_gn_g"], ret_state, i,
                                      grp.rets if last else (), grp.batch, grp.length)
            one_chunk = grp.length == math.gcd(grp.length, CHUNK)
            ssd_fn = _ssd_decode if (one_chunk and ssd_state is not None and grp.batch == LANE) else _ssd
            ssd, new_conv, new_ssm = ssd_fn(xbc, z, dt, lp, ssd_state, i,
                                            list(zip(grp.convs, grp.ssms)) if last else (), grp.batch, grp.length)
            grp.rets.append(new_ret)
            grp.ssms.append(new_ssm)
            grp.convs.append(new_conv)
            group_acts.append((ret, ssd, gates, grp.x))
        new_x, (hw, route, counts) = _merge(group_acts, lp, big, i)
        for grp, x2 in zip(groups, new_x, strict=True):
            grp.x = x2
        pos, tab = _plan(route, counts, ROW_TILE)
        y = _moe(hw, pos, tab, big, i, ROW_TILE)
    outs = []
    for grp in groups:
        out = _final_norm(grp.x, y, grp.row_offset, lnf_g.reshape(1, D_MODEL))
        outs.append(out.reshape(grp.batch, grp.length, D_MODEL))
    states = []
    for grp in groups:
        states += [grp.rets[-1], grp.ssms[-1], grp.convs[-1]]
    return tuple(outs + states)
```

```python
import functools
import math

import jax
import jax.numpy as jnp
from jax import lax
from jax.experimental import pallas as pl
from jax.experimental.pallas import tpu as pltpu

F32 = jnp.float32
BF16 = jnp.bfloat16

D_MODEL = 1024
H_RET, DK_RET, DV_RET = 4, 128, 128
RET_W = H_RET * DV_RET
ROPE_BASE = 10000.0
H_SSD, P_SSD, G_SSD, N_SSD = 16, 64, 2, 64
R_SSD = H_SSD // G_SSD
SSD_W = H_SSD * P_SSD
GRP_W = SSD_W // G_SSD
CONV_K = 4
CONV_DIM = SSD_W + 2 * G_SSD * N_SSD
CHUNK = 128
N_EGROUPS, EXP_PER_GROUP = 4, 4
N_EXPERTS = N_EGROUPS * EXP_PER_GROUP
N_PAIRS = EXP_PER_GROUP * (EXP_PER_GROUP - 1) // 2
N_CLASSES = N_EGROUPS * N_PAIRS
D_FF = 512
EPS = 1e-6
IN_SIZES = [RET_W, RET_W, RET_W, RET_W, SSD_W, CONV_DIM, H_SSD, D_MODEL, D_MODEL]

LANE = 128
SUBLANE = 8
MIB = 1024 * 1024

ROW_TILE = 256
INPROJ_TILE = 512
META_W = LANE
TOKEN_TILE_ROWS = D_MODEL // LANE
assert TOKEN_TILE_ROWS == SUBLANE
PACKED_ROWS = D_MODEL // (2 * LANE)
HIGH_HALF = 0xFFFF0000
SAMPLE_SEQS_PER_STEP = 8
PROMPT_SEQS_PER_STEP = 4


def _vmem_limit(block_bytes):
    return int(min(2 * block_bytes + 16 * MIB, 56 * MIB))


def _params(semantics, block_bytes):
    return pltpu.CompilerParams(dimension_semantics=semantics,
                                vmem_limit_bytes=_vmem_limit(block_bytes))


def _nbytes(shape, dtype):
    return math.prod(shape) * jnp.dtype(dtype).itemsize


def _full(shape):
    return pl.BlockSpec(shape, lambda *_: (0,) * len(shape))


def _dot(a, b):
    return jnp.dot(a, b, preferred_element_type=F32)


def _dot_nt(a, b):
    return lax.dot_general(a, b, (((1,), (1,)), ((), ())), preferred_element_type=F32)


def _dot_tn(a, b):
    return lax.dot_general(a, b, (((0,), (0,)), ((), ())), preferred_element_type=F32)


def _split3(a):
    hi = a.astype(BF16)
    r1 = a - hi.astype(F32)
    mid = r1.astype(BF16)
    lo = (r1 - mid.astype(F32)).astype(BF16)
    return hi, mid, lo


def _rmsnorm(x, g):
    r = lax.rsqrt(jnp.mean(x * x, axis=-1, keepdims=True) + EPS)
    return x * r * g


def _token_rows(ref, rows):
    return jnp.concatenate([ref[pl.ds(k, rows, stride=SUBLANE), :] for k in range(TOKEN_TILE_ROWS)], axis=1)


def _token_spec(row_offset, rows=ROW_TILE):
    assert row_offset % rows == 0
    first = row_offset // rows
    return pl.BlockSpec((rows * TOKEN_TILE_ROWS, LANE), lambda i: (i + first, 0))


def _inproj_body(has_add, *refs):
    n_in = 6 if has_add else 5
    ins, outs = refs[:n_in], refs[n_in:]
    if has_add:
        x_ref, y_ref, g_ref = ins[:3]
        x = x_ref[...] + _token_rows(y_ref, INPROJ_TILE)
        outs[0][...] = x
        outs = outs[1:]
    else:
        x_ref, g_ref = ins[:2]
        x = x_ref[...]
    w_main, w_dt, w_gates = ins[-3:]
    h = _rmsnorm(x, g_ref[...]).astype(BF16)
    col = 0
    for o_ref in outs[:3]:
        width = o_ref.shape[1]
        o_ref[...] = _dot(h, w_main[:, col:col + width])
        col += width
    outs[3][...] = _dot(h, w_dt[...])
    outs[4][...] = _dot(h, w_gates[...])


def _layer_block(arr, layer):
    shape = arr.shape[1:]
    return pl.BlockSpec((None,) + shape, lambda *_: (layer,) + (0,) * len(shape))


def _inproj(x, y, y_offset, g, ws, layer):
    t = x.shape[0]
    has_add = y is not None
    widths = [4 * RET_W, SSD_W, CONV_DIM, LANE, 2 * D_MODEL]
    assert sum(widths[:3]) == sum(IN_SIZES[:6])
    row = lambda w: pl.BlockSpec((INPROJ_TILE, w), lambda i: (i, 0))
    in_specs = [row(D_MODEL)] + ([_token_spec(y_offset, INPROJ_TILE)] if has_add else []) + [_full((1, D_MODEL))]
    in_specs += [pl.BlockSpec((None,) + w.shape[1:], lambda i: (layer, 0, 0), pipeline_mode=pl.Buffered(1))
                 for w in ws]
    out_shape = [jax.ShapeDtypeStruct((t, w), F32) for w in widths]
    out_specs = [row(w) for w in widths]
    if has_add:
        out_shape = [jax.ShapeDtypeStruct((t, D_MODEL), F32)] + out_shape
        out_specs = [row(D_MODEL)] + out_specs
    weight_bytes = sum(_nbytes(w.shape[1:], BF16) for w in ws)
    block_bytes = _nbytes((INPROJ_TILE, sum(widths) + 3 * D_MODEL), F32)
    args = ([x, y] if has_add else [x]) + [g] + list(ws)
    return pl.pallas_call(
        functools.partial(_inproj_body, has_add),
        grid=(t // INPROJ_TILE,), in_specs=in_specs, out_specs=out_specs, out_shape=out_shape,
        compiler_params=pltpu.CompilerParams(dimension_semantics=("arbitrary",),
                                             vmem_limit_bytes=int(min(weight_bytes + 2 * block_bytes + 6 * MIB,
                                                                      58 * MIB))),
        name="inproj",
    )(*args)


def _retention_body(c, nb, nc, has_state, n_prev, *refs):
    refs = list(refs)
    qkvr_ref, cos_ref, sin_ref, dmat_ref, qdec_ref, kdec_ref, cdec_ref, gn_ref = refs[:8]
    refs = refs[8:]
    if has_state:
        s0_ref = refs.pop(0)
    prev_refs = [refs.pop(0) for _ in range(n_prev)]
    o_ref, ns_ref, st_scr = refs
    ci = pl.program_id(1)

    @pl.when(ci == 0)
    def _():
        if has_state:
            st_scr[...] = s0_ref[...]
        else:
            st_scr[...] = jnp.zeros(st_scr.shape, F32)

    cosv, sinv = cos_ref[...], sin_ref[...]
    for s in range(nb):
        for h in range(H_RET):
            col = lambda j: slice(j * RET_W + h * DK_RET, j * RET_W + (h + 1) * DK_RET)
            q, k = qkvr_ref[s, :, col(0)], qkvr_ref[s, :, col(1)]
            v, rg = qkvr_ref[s, :, col(2)], qkvr_ref[s, :, col(3)]
            qr = q * cosv + pltpu.roll(q, DK_RET // 2, axis=1) * sinv
            kr = (k * cosv + pltpu.roll(k, DK_RET // 2, axis=1) * sinv) * (DK_RET ** -0.5)
            qb, kb, vb = qr.astype(BF16), kr.astype(BF16), v.astype(BF16)
            state = st_scr[s, h]
            sc = _dot_nt(qb, kb) * dmat_ref[h]
            o = _dot(sc.astype(BF16), vb) + _dot(qb, state.astype(BF16)) * qdec_ref[h]
            st_scr[s, h] = cdec_ref[h] * state + _dot_tn((kr * kdec_ref[h]).astype(BF16), vb)
            mu = jnp.mean(o, axis=-1, keepdims=True)
            d = o - mu
            var = jnp.mean(d * d, axis=-1, keepdims=True)
            on = d * lax.rsqrt(var + EPS) * gn_ref[:, h * DV_RET:(h + 1) * DV_RET]
            o_ref[s, :, h * DV_RET:(h + 1) * DV_RET] = (jax.nn.silu(rg) * on).astype(o_ref.dtype)

    @pl.when(ci == nc - 1)
    def _():
        if n_prev:
            for l, prev_ref in enumerate(prev_refs):
                ns_ref[l] = prev_ref[...]
            ns_ref[n_prev] = st_scr[...]
        else:
            ns_ref[...] = st_scr[...]


def _state_out(prev, batch, nb, shape):
    zeros = (0,) * len(shape)
    if prev:
        depth = len(prev) + 1
        return (pl.BlockSpec((depth, nb) + shape, lambda b, ci: (0, b) + zeros),
                jax.ShapeDtypeStruct((depth, batch) + shape, F32))
    return pl.BlockSpec((nb,) + shape, lambda b, ci: (b,) + zeros), jax.ShapeDtypeStruct((batch,) + shape, F32)


def _retention(qkvr, start, gn_g, states, layer, prev, batch, length):
    c = math.gcd(length, CHUNK)
    nc = length // c
    nb = PROMPT_SEQS_PER_STEP if nc > 1 else SAMPLE_SEQS_PER_STEP
    assert batch % nb == 0
    has_state = states is not None
    t = batch * length
    half = DK_RET // 2
    pos = start + jnp.arange(length, dtype=F32)
    inv = ROPE_BASE ** (-jnp.arange(half, dtype=F32) / half)
    ang = pos[:, None] * inv[None, :]
    cosv = jnp.concatenate([jnp.cos(ang), jnp.cos(ang)], axis=1)
    sinv = jnp.concatenate([-jnp.sin(ang), jnp.sin(ang)], axis=1)
    log_g = jnp.log1p(-jnp.exp2(-5.0 - jnp.arange(H_RET, dtype=F32)))
    idx = jnp.arange(c, dtype=F32)
    rel = idx[:, None] - idx[None, :]
    causal = rel >= 0
    dmat = jnp.where(causal[None], jnp.exp(jnp.where(causal, rel, 0.0)[None] * log_g[:, None, None]), 0.0)
    qdec = jnp.broadcast_to(jnp.exp((idx[None, :] + 1.0) * log_g[:, None])[:, :, None], (H_RET, c, LANE))
    kdec = jnp.broadcast_to(jnp.exp((c - 1.0 - idx[None, :]) * log_g[:, None])[:, :, None], (H_RET, c, LANE))
    cdec = jnp.exp(c * log_g)

    rows = nb * c
    state_spec = pl.BlockSpec((nb, H_RET, DK_RET, DV_RET), lambda b, ci: (b, 0, 0, 0))
    seq_rows = lambda w: pl.BlockSpec((nb, c, w), lambda b, ci: (b, ci, 0))
    in_specs = [
        seq_rows(4 * RET_W),
        pl.BlockSpec((c, LANE), lambda b, ci: (ci, 0)),
        pl.BlockSpec((c, LANE), lambda b, ci: (ci, 0)),
        _full((H_RET, c, c)), _full((H_RET, c, LANE)), _full((H_RET, c, LANE)),
        pl.BlockSpec(memory_space=pltpu.SMEM),
        _full((1, RET_W)),
    ]
    args = [qkvr.reshape(batch, length, 4 * RET_W), cosv, sinv, dmat, qdec, kdec, cdec, gn_g]
    if has_state:
        in_specs.append(pl.BlockSpec((None, nb, H_RET, DK_RET, DV_RET), lambda b, ci: (layer, b, 0, 0, 0)))
        args.append(states)
    in_specs += [state_spec] * len(prev)
    args += list(prev)
    block_bytes = (_nbytes((rows, 4 * RET_W + 2 * LANE + RET_W), F32)
                   + (3 + 2 * len(prev)) * _nbytes((nb, H_RET, DK_RET, DV_RET), F32)
                   + 3 * _nbytes((H_RET, c, LANE), F32))
    ns_spec, ns_shape = _state_out(prev, batch, nb, (H_RET, DK_RET, DV_RET))
    out, new_state = pl.pallas_call(
        functools.partial(_retention_body, c, nb, nc, has_state, len(prev)),
        grid=(batch // nb, nc), in_specs=in_specs,
        out_specs=[seq_rows(RET_W), ns_spec],
        out_shape=[jax.ShapeDtypeStruct((batch, length, RET_W), BF16), ns_shape],
        scratch_shapes=[pltpu.VMEM((nb, H_RET, DK_RET, DV_RET), F32)],
        compiler_params=_params(("arbitrary", "arbitrary"), block_bytes), name="retention",
    )(*args)
    return out.reshape(t, RET_W), new_state


def _ssd_body(c, nb, nc, has_state, n_prev, *refs):
    refs = list(refs)
    (xbc_ref, z_ref, dt_ref, cw_ref, cb_ref, dtb_ref, a_ref, dsk_ref, ng_ref,
     tri_ref, eye_ref, exp_ref, sel_ref) = refs[:13]
    refs = refs[13:]
    if has_state:
        sconv_ref, sssm_ref = refs.pop(0), refs.pop(0)
    prev_conv = [refs.pop(0) for _ in range(n_prev)]
    prev_ssm = [refs.pop(0) for _ in range(n_prev)]
    y_ref, nconv_ref, nssm_ref, ext_scr, st_scr = refs
    new_conv = nconv_ref.at[n_prev] if n_prev else nconv_ref
    new_ssm = nssm_ref.at[n_prev] if n_prev else nssm_ref
    ci = pl.program_id(1)
    pad = SUBLANE
    hist = CONV_K - 1

    @pl.when(ci == 0)
    def _():
        ext_scr[:, 0:pad, :] = jnp.zeros((nb, pad, CONV_DIM), F32)
        if has_state:
            ext_scr[:, pad - hist:pad, :] = sconv_ref[...]
            for s in range(nb):
                for g in range(G_SSD):
                    for r in range(R_SSD):
                        st_scr[s, g, r * P_SSD:(r + 1) * P_SSD, :] = sssm_ref[s, g * R_SSD + r]
        else:
            st_scr[...] = jnp.zeros(st_scr.shape, F32)

    causal = (lax.broadcasted_iota(jnp.int32, (c, c), 0) >= lax.broadcasted_iota(jnp.int32, (c, c), 1))
    left_half = (lax.broadcasted_iota(jnp.int32, (c, SSD_W), 1) % LANE) < P_SSD
    tri, eye, expand = tri_ref[...], eye_ref[...], exp_ref[...]

    for s in range(nb):
        rows = slice(s * c, (s + 1) * c)
        ext_scr[s, pad:pad + c, :] = xbc_ref[rows, :]
        acc = ext_scr[s, pad - hist:pad - hist + c, :] * cw_ref[0:1, :]
        for j in range(1, CONV_K):
            acc = acc + ext_scr[s, pad - hist + j:pad - hist + j + c, :] * cw_ref[j:j + 1, :]
        conv = jax.nn.silu(cb_ref[...] + acc)

        @pl.when(ci == nc - 1)
        def _():
            new_conv[s] = ext_scr[s, pad + c - hist:pad + c, :]

        ext_scr[s, 0:pad, :] = ext_scr[s, c:c + pad, :]

        xs = conv[:, :SSD_W]
        bm = conv[:, SSD_W:SSD_W + G_SSD * N_SSD]
        cm = conv[:, SSD_W + G_SSD * N_SSD:]
        dt = jax.nn.softplus(dt_ref[rows, :] + dtb_ref[...])
        dta = dt * a_ref[...]
        cum = sum(_dot(tri, p) for p in _split3(dta))
        cum_parts = _split3(cum)
        cum_t = sum(_dot_nt(eye, p) for p in cum_parts)
        cum_e = sum(_dot(p, expand) for p in cum_parts)
        dt_e = sum(_dot(p, expand) for p in _split3(dt))
        last_e = cum_e[c - 1:c, :]
        xdt = xs * dt_e
        xw = (xs * (jnp.exp(last_e - cum_e) * dt_e)).astype(BF16)
        x_l = jnp.where(left_half, xdt, 0.0).astype(BF16)
        x_r = jnp.where(left_half, 0.0, xdt).astype(BF16)

        y_parts, off_parts = [], []
        for g in range(G_SSD):
            bm_g = bm[:, g * N_SSD:(g + 1) * N_SSD].astype(BF16)
            cm_g = cm[:, g * N_SSD:(g + 1) * N_SSD].astype(BF16)
            cb = _dot_nt(cm_g, bm_g)
            state = st_scr[s, g]
            off_parts.append(_dot_nt(cm_g, state.astype(BF16)))
            decay = jnp.exp(jnp.sum(sel_ref[g] * cum[c - 1:c, :], axis=1, keepdims=True))
            st_scr[s, g] = decay * state + _dot_tn(xw[:, g * GRP_W:(g + 1) * GRP_W], bm_g)
            for jj in range(R_SSD // 2):
                j = g * (R_SSD // 2) + jj
                w = []
                for hd in (2 * j, 2 * j + 1):
                    seg = cum[:, hd:hd + 1] - cum_t[hd:hd + 1, :]
                    w.append((jnp.exp(jnp.where(causal, seg, -jnp.inf)) * cb).astype(BF16))
                cols = slice(j * LANE, (j + 1) * LANE)
                y_parts.append(_dot(w[0], x_l[:, cols]) + _dot(w[1], x_r[:, cols]))
        y = jnp.concatenate(y_parts, axis=1) + jnp.exp(cum_e) * jnp.concatenate(off_parts, axis=1)
        y = (y + dsk_ref[...] * xs) * jax.nn.silu(z_ref[rows, :])
        normed = []
        for g in range(G_SSD):
            yg = y[:, g * GRP_W:(g + 1) * GRP_W]
            normed.append(yg * lax.rsqrt(jnp.mean(yg * yg, axis=-1, keepdims=True) + EPS))
        y_ref[rows, :] = (jnp.concatenate(normed, axis=1) * ng_ref[...]).astype(y_ref.dtype)

    @pl.when(ci == nc - 1)
    def _():
        for l in range(n_prev):
            nconv_ref[l] = prev_conv[l][...]
            nssm_ref[l] = prev_ssm[l][...]
        for s in range(nb):
            for g in range(G_SSD):
                for r in range(R_SSD):
                    new_ssm[s, g * R_SSD + r] = st_scr[s, g, r * P_SSD:(r + 1) * P_SSD, :]


def _ssd(xbc, z, dt, lp, states, layer, prev, batch, length):
    c = math.gcd(length, CHUNK)
    nc = length // c
    nb = 1 if nc > 1 else SAMPLE_SEQS_PER_STEP
    assert batch % nb == 0 and c % SUBLANE == 0
    has_state = states is not None
    t = batch * length
    rows = nb * c
    tri = (jnp.arange(c)[:, None] >= jnp.arange(c)[None, :]).astype(BF16)
    eye = jnp.eye(LANE, dtype=BF16)
    head_of_ch = jnp.arange(SSD_W) // P_SSD
    expand = (jnp.arange(LANE)[:, None] == head_of_ch[None, :]).astype(BF16)
    sel = (head_of_ch[:, None] == jnp.arange(LANE)[None, :]).astype(F32).reshape(G_SSD, GRP_W, LANE)

    seq = lambda shape: pl.BlockSpec((nb,) + shape, lambda b, ci: (b,) + (0,) * len(shape))
    rowb = lambda w: pl.BlockSpec((rows, w), lambda b, ci: (b * nc + ci, 0))
    in_specs = [rowb(CONV_DIM), rowb(SSD_W), rowb(LANE),
                _full((CONV_K, CONV_DIM)), _full((1, CONV_DIM)), _full((1, LANE)), _full((1, LANE)),
                _full((1, SSD_W)), _full((1, SSD_W)),
                _full((c, c)), _full((LANE, LANE)), _full((LANE, SSD_W)), _full((G_SSD, GRP_W, LANE))]
    args = [xbc, z, dt, lp["conv_w"], lp["conv_b"], lp["dt_bias"], lp["a"], lp["d_skip"], lp["ssd_norm_g"],
            tri, eye, expand, sel]
    conv_shape, ssm_shape = (CONV_K - 1, CONV_DIM), (H_SSD, P_SSD, N_SSD)
    if has_state:
        stacked = lambda shape: pl.BlockSpec((None, nb) + shape, lambda b, ci: (layer, b) + (0,) * len(shape))
        in_specs += [stacked(conv_shape), stacked(ssm_shape)]
        args += list(states)
    in_specs += [seq(conv_shape)] * len(prev) + [seq(ssm_shape)] * len(prev)
    args += [p[0] for p in prev] + [p[1] for p in prev]
    block_bytes = (_nbytes((rows, CONV_DIM + 2 * SSD_W + LANE), F32) + _nbytes((nb, c + SUBLANE, CONV_DIM), F32)
                   + (3 + 4 * len(prev)) * _nbytes((nb, SSD_W, LANE), F32) + _nbytes((LANE, SSD_W), F32)
                   + 12 * _nbytes((c, SSD_W), F32))
    nconv_spec, nconv_shape = _state_out(prev, batch, nb, conv_shape)
    nssm_spec, nssm_shape = _state_out(prev, batch, nb, ssm_shape)
    return pl.pallas_call(
        functools.partial(_ssd_body, c, nb, nc, has_state, len(prev)),
        grid=(batch // nb, nc), in_specs=in_specs,
        out_specs=[rowb(SSD_W), nconv_spec, nssm_spec],
        out_shape=[jax.ShapeDtypeStruct((t, SSD_W), BF16), nconv_shape, nssm_shape],
        scratch_shapes=[pltpu.VMEM((nb, c + SUBLANE, CONV_DIM), F32),
                        pltpu.VMEM((nb, G_SSD, GRP_W, N_SSD), F32)],
        compiler_params=_params(("arbitrary", "arbitrary"), block_bytes), name="ssd",
    )(*args)


def _ssd_dec_pre_body(n_prev, x_ref, dt_ref, cw_ref, cb_ref, sconv_ref, *rest):
    prev, (xt_ref, dtt_ref, nconv_ref) = rest[:n_prev], rest[n_prev:]
    length, hist = x_ref.shape[0], CONV_K - 1
    rows = [sconv_ref[j] for j in range(hist)] + [x_ref[m] for m in range(length)]
    for m in range(length):
        acc = rows[m] * cw_ref[0:1, :]
        for j in range(1, CONV_K):
            acc = acc + rows[m + j] * cw_ref[j:j + 1, :]
        conv = jax.nn.silu(cb_ref[...] + acc)
        for k in range(CONV_DIM // LANE):
            xt_ref[m, k * LANE:(k + 1) * LANE, :] = conv[:, k * LANE:(k + 1) * LANE].T
        dtt_ref[m] = dt_ref[m].T
    new_conv = nconv_ref.at[n_prev] if n_prev else nconv_ref
    for l in range(n_prev):
        nconv_ref[l] = prev[l][...]
    for j in range(hist):
        new_conv[j] = rows[length + j]


def _ssd_dec_head_body(n_prev, dtb_ref, a_ref, dsk_ref, x_ref, b_ref, c_ref, dt_ref, s_ref, *rest):
    prev, (y_ref, ns_ref, xw_scr, off_scr) = rest[:n_prev], rest[n_prev:]
    h = pl.program_id(0)
    length = x_ref.shape[0]
    new_state = ns_ref.at[n_prev] if n_prev else ns_ref
    for l in range(n_prev):
        ns_ref[l] = prev[l][...]
    dt = [jax.nn.softplus(dt_ref[m, pl.ds(h, 1), :] + dtb_ref[h]) for m in range(length)]
    cum = []
    for m in range(length):
        cum.append(dt[m] * a_ref[h] if m == 0 else cum[-1] + dt[m] * a_ref[h])
    last = cum[-1]
    decay = jnp.exp(last)
    for j in range(length):
        xw_scr[j] = x_ref[j] * (jnp.exp(last - cum[j]) * dt[j])

    def per_state_row(p, carry):
        sp = s_ref[p]
        new = decay * sp
        for j in range(length):
            new = new + xw_scr[j, pl.ds(p, 1), :] * b_ref[j]
        new_state[p] = new
        for m in range(length):
            off_scr[m, pl.ds(p, 1), :] = jnp.sum(c_ref[m] * sp, axis=0, keepdims=True)
        return carry

    lax.fori_loop(0, P_SSD, per_state_row, 0)
    for m in range(length):
        y = jnp.exp(cum[m]) * off_scr[m] + dsk_ref[h] * x_ref[m]
        for j in range(m + 1):
            cb = jnp.sum(c_ref[m] * b_ref[j], axis=0, keepdims=True)
            y = y + (jnp.exp(cum[m] - cum[j]) * cb * dt[j]) * x_ref[j]
        y_ref[m] = y


def _ssd_dec_post_body(yt_ref, z_ref, ng_ref, o_ref):
    y = jnp.concatenate([yt_ref[k * LANE:(k + 1) * LANE, :].T for k in range(SSD_W // LANE)], axis=1)
    y = y * jax.nn.silu(z_ref[...])
    normed = []
    for g in range(G_SSD):
        yg = y[:, g * GRP_W:(g + 1) * GRP_W]
        normed.append(yg * lax.rsqrt(jnp.mean(yg * yg, axis=-1, keepdims=True) + EPS))
    o_ref[...] = (jnp.concatenate(normed, axis=1) * ng_ref[...]).astype(o_ref.dtype)


def _ssd_decode(xbc, z, dt, lp, states, layer, prev, batch, length):
    assert batch == LANE
    t = batch * length
    depth_out = len(prev) + 1
    tok = lambda a, w: a.reshape(batch, length, w).transpose(1, 0, 2)
    conv_in = states[0].transpose(0, 2, 1, 3)
    ssm_in = states[1].transpose(0, 2, 3, 4, 1)
    hist = CONV_K - 1
    stack = lambda shape: ((depth_out,) + shape) if prev else shape

    pre_in = [_full((length, batch, CONV_DIM)), _full((length, batch, LANE)), _full((CONV_K, CONV_DIM)),
              _full((1, CONV_DIM)),
              pl.BlockSpec((None, hist, batch, CONV_DIM), lambda i: (layer, 0, 0, 0))]
    pre_in += [_full((hist, batch, CONV_DIM))] * len(prev)
    xt, dtt, new_conv = pl.pallas_call(
        functools.partial(_ssd_dec_pre_body, len(prev)), grid=(1,), in_specs=pre_in,
        out_specs=[_full((length, CONV_DIM, batch)), _full((length, LANE, batch)),
                   _full(stack((hist, batch, CONV_DIM)))],
        out_shape=[jax.ShapeDtypeStruct((length, CONV_DIM, batch), F32),
                   jax.ShapeDtypeStruct((length, LANE, batch), F32),
                   jax.ShapeDtypeStruct(stack((hist, batch, CONV_DIM)), F32)],
        compiler_params=_params(("arbitrary",), 4 * _nbytes((length, batch, CONV_DIM), F32)), name="ssd_dec_conv",
    )(tok(xbc, CONV_DIM), tok(dt, LANE), lp["conv_w"], lp["conv_b"], conv_in, *[p[0] for p in prev])

    head_rows = lambda first: pl.BlockSpec((length, P_SSD, batch), lambda h: (0, first(h), 0))
    smem = pl.BlockSpec(memory_space=pltpu.SMEM)
    state_block = (P_SSD, N_SSD, batch)
    head_in = [smem, smem, smem,
               head_rows(lambda h: h),
               head_rows(lambda h: SSD_W // P_SSD + h // R_SSD),
               head_rows(lambda h: (SSD_W + G_SSD * N_SSD) // P_SSD + h // R_SSD),
               _full((length, LANE, batch)),
               pl.BlockSpec((None, None) + state_block, lambda h: (layer, h, 0, 0, 0))]
    head_in += [pl.BlockSpec((None,) + state_block, lambda h: (h, 0, 0, 0))] * len(prev)
    if prev:
        ns_spec = pl.BlockSpec((depth_out, None) + state_block, lambda h: (0, h, 0, 0, 0))
    else:
        ns_spec = pl.BlockSpec((None,) + state_block, lambda h: (h, 0, 0, 0))
    heads = lambda v: v.reshape(LANE)[:H_SSD]
    yt, new_ssm = pl.pallas_call(
        functools.partial(_ssd_dec_head_body, len(prev)), grid=(H_SSD,), in_specs=head_in,
        out_specs=[head_rows(lambda h: h), ns_spec],
        out_shape=[jax.ShapeDtypeStruct((length, SSD_W, batch), F32),
                   jax.ShapeDtypeStruct(stack((H_SSD,) + state_block), F32)],
        scratch_shapes=[pltpu.VMEM((length, P_SSD, batch), F32), pltpu.VMEM((length, P_SSD, batch), F32)],
        compiler_params=_params(("arbitrary",), (4 + 2 * depth_out) * _nbytes(state_block, F32)), name="ssd_dec_heads",
    )(heads(lp["dt_bias"]), heads(lp["a"]), lp["d_skip"].reshape(H_SSD, P_SSD)[:, 0], xt, xt, xt, dtt, ssm_in,
      *[p[1] for p in prev])

    y = pl.pallas_call(
        _ssd_dec_post_body, grid=(length,),
        in_specs=[pl.BlockSpec((None, SSD_W, batch), lambda m: (m, 0, 0)),
                  pl.BlockSpec((None, batch, SSD_W), lambda m: (m, 0, 0)), _full((1, SSD_W))],
        out_specs=pl.BlockSpec((None, batch, SSD_W), lambda m: (m, 0, 0)),
        out_shape=jax.ShapeDtypeStruct((length, batch, SSD_W), BF16),
        compiler_params=_params(("arbitrary",), 3 * _nbytes((batch, SSD_W), F32)), name="ssd_dec_gate",
    )(yt, tok(z, SSD_W), lp["ssd_norm_g"])
    y = y.transpose(1, 0, 2).reshape(t, SSD_W)
    if prev:
        new_conv = new_conv.transpose(0, 2, 1, 3)
        new_ssm = new_ssm.transpose(0, 4, 1, 2, 3)
    return y, new_conv, new_ssm


def _first_argmax(vals):
    best = vals[0]
    for v in vals[1:]:
        best = jnp.maximum(best, v)
    idx = jnp.full(best.shape, len(vals) - 1, jnp.int32)
    for i in range(len(vals) - 2, -1, -1):
        idx = jnp.where(vals[i] == best, i, idx)
    return best, idx


def _merge_body(bounds, *refs):
    n_groups = len(bounds)
    acts, refs = refs[:4 * n_groups], refs[4 * n_groups:]
    weights, refs = refs[:7], refs[7:]
    x2_refs, shared = refs[:n_groups], refs[n_groups:]
    i = pl.program_id(0)
    for gi, (first, count) in enumerate(bounds):
        @pl.when((i >= first) & (i < first + count))
        def _():
            _merge_tile(*acts[4 * gi:4 * gi + 4], *weights, x2_refs[gi], *shared)


def _merge_tile(ret_ref, ssd_ref, gates_ref, x_ref, wr_ref, ws_ref, wo_ref, g2_ref, rhi_ref, rlo_ref, upp_ref,
                x2_ref, hw_ref, route_ref, cnt_ref):
    ret_y = _dot(ret_ref[...], wr_ref[...])
    ssd_y = _dot(ssd_ref[...], ws_ref[...])
    m = (jax.nn.sigmoid(gates_ref[:, :D_MODEL]) * ret_y + jax.nn.sigmoid(gates_ref[:, D_MODEL:]) * ssd_y)
    x2 = x_ref[...] + _dot(m.astype(BF16), wo_ref[...])
    x2_ref[...] = x2
    h2 = _rmsnorm(x2, g2_ref[...])
    bits = pltpu.bitcast(h2.astype(BF16).astype(F32), jnp.uint32)
    half = D_MODEL // 2
    word = lax.bitwise_or(lax.bitwise_and(bits[:, half:], jnp.uint32(HIGH_HALF)),
                          lax.shift_right_logical(bits[:, :half], jnp.uint32(16)))
    n_rows = h2.shape[0]
    for k in range(PACKED_ROWS):
        hw_ref[pl.ds(k, n_rows, stride=SUBLANE), :] = word[:, k * LANE:(k + 1) * LANE]

    h_hi = h2.astype(BF16)
    h_lo = (h2 - h_hi.astype(F32)).astype(BF16)
    logits = _dot_nt(rhi_ref[...], h_hi) + (_dot_nt(rlo_ref[...], h_hi) + _dot_nt(rhi_ref[...], h_lo))
    colv = lambda i: logits[i:i + 1, :]
    glog = [colv(i) for i in range(N_EGROUPS)]
    gmax, gidx = _first_argmax(glog)
    den = glog[0] * 0.0
    for v in glog:
        den = den + jnp.exp(v - gmax)
    gw = 1.0 / den
    sel = []
    for j in range(EXP_PER_GROUP):
        v = colv(N_EGROUPS + (N_EGROUPS - 1) * EXP_PER_GROUP + j)
        for g in range(N_EGROUPS - 2, -1, -1):
            v = jnp.where(gidx == g, colv(N_EGROUPS + g * EXP_PER_GROUP + j), v)
        sel.append(v)
    v1, i1 = _first_argmax(sel)
    v2, i2 = _first_argmax([jnp.where(i1 == j, -jnp.inf, sel[j]) for j in range(EXP_PER_GROUP)])
    e = jnp.exp(v2 - v1)
    w1 = gw / (1.0 + e)
    w2 = gw * e / (1.0 + e)
    first_low = i1 < i2
    lo = jnp.where(first_low, i1, i2)
    hi = jnp.where(first_low, i2, i1)
    pair = jnp.where(lo == 0, hi - 1, jnp.where(lo == 1, hi + 1, N_PAIRS - 1))
    cls = gidx * N_PAIRS + pair
    w_lo = jnp.where(first_low, w1, w2)
    w_hi = jnp.where(first_low, w2, w1)

    sub = lax.broadcasted_iota(jnp.int32, (META_W, n_rows), 0)
    onehot = jnp.where(sub == cls, 1.0, 0.0).astype(BF16)
    before = _dot(onehot, upp_ref[...])
    rank = jnp.sum(onehot.astype(F32) * before, axis=0, keepdims=True)
    meta = jnp.where(sub == 0, w_lo, jnp.where(sub == 1, w_hi, jnp.where(
        sub == 2, cls.astype(F32), jnp.where(sub == 3, rank, 0.0))))
    route_ref[0] = meta[0:SUBLANE, :]
    meta_bits = pltpu.bitcast(meta.T, jnp.uint32)
    for k in range(PACKED_ROWS, TOKEN_TILE_ROWS):
        hw_ref[pl.ds(k, n_rows, stride=SUBLANE), :] = meta_bits
    cnt_ref[0] = _dot_nt(jnp.ones((SUBLANE, n_rows), BF16), onehot)[0:1, :]


def _merge(group_acts, lp, big, layer):
    counts = [a[3].shape[0] // ROW_TILE for a in group_acts]
    bounds = tuple((sum(counts[:gi]), counts[gi]) for gi in range(len(counts)))
    n_tiles = sum(counts)
    total_rows = n_tiles * ROW_TILE
    upp = (jnp.arange(ROW_TILE)[:, None] < jnp.arange(ROW_TILE)[None, :]).astype(BF16)
    stacked = [big["w_ret_out"], big["w_ssd_out"], big["w_o"]]
    ws = [lp["ln2_g"], lp["router_hi"], lp["router_lo"], upp]
    block_bytes = (sum(_nbytes(w.shape[1:], w.dtype) for w in stacked) + sum(_nbytes(w.shape, w.dtype) for w in ws)
                   + len(counts) * (_nbytes((ROW_TILE, RET_W + SSD_W), BF16) + _nbytes((ROW_TILE, 4 * D_MODEL), F32))
                   + _nbytes((ROW_TILE, 3 * D_MODEL), F32))

    def group_row(width, first, count):
        return pl.BlockSpec((ROW_TILE, width), lambda i: (jnp.clip(i - first, 0, count - 1), 0))

    in_specs, args, x2_specs, x2_shapes = [], [], [], []
    for (first, count), acts in zip(bounds, group_acts, strict=True):
        in_specs += [group_row(w, first, count) for w in (RET_W, SSD_W, 2 * D_MODEL, D_MODEL)]
        args += list(acts)
        x2_specs.append(group_row(D_MODEL, first, count))
        x2_shapes.append(jax.ShapeDtypeStruct((count * ROW_TILE, D_MODEL), F32))
    in_specs += [_layer_block(w, layer) for w in stacked] + [_full(w.shape) for w in ws]
    args += stacked + ws
    outs = pl.pallas_call(
        functools.partial(_merge_body, bounds), grid=(n_tiles,), in_specs=in_specs,
        out_specs=x2_specs + [_token_spec(0), pl.BlockSpec((1, SUBLANE, ROW_TILE), lambda i: (i, 0, 0)),
                              pl.BlockSpec((1, 1, META_W), lambda i: (i, 0, 0))],
        out_shape=x2_shapes + [jax.ShapeDtypeStruct((total_rows * TOKEN_TILE_ROWS, LANE), jnp.uint32),
                               jax.ShapeDtypeStruct((n_tiles, SUBLANE, ROW_TILE), F32),
                               jax.ShapeDtypeStruct((n_tiles, 1, META_W), F32)],
        compiler_params=_params(("arbitrary",), block_bytes), name="merge_router",
    )(*args)
    return outs[:len(counts)], outs[len(counts):]


PLAN_ROWS = LANE
TAB_ROWS = SUBLANE
TAB_LO, TAB_HI, TAB_USED = 0, 1, 2
BYTE = 256.0
MOE_RING = 3


def _row_of(col_vals):
    pick = (lax.broadcasted_iota(jnp.int32, (TAB_ROWS, LANE), 0)
            == lax.broadcasted_iota(jnp.int32, (TAB_ROWS, LANE), 1)).astype(BF16)
    hi = jnp.floor(col_vals * (1.0 / BYTE))
    lo = col_vals - hi * BYTE
    return BYTE * _dot_nt(pick, hi.astype(BF16)) + _dot_nt(pick, lo.astype(BF16))


def _plan_body(tile, n_row_tiles, route_ref, cnt_ref, low_ref, upp_ref, pos_ref, tab_ref, offs_scr):
    cnt = cnt_ref[...]
    tot = jnp.sum(cnt, axis=0, keepdims=True)
    cls_tiles = jnp.floor((tot + (tile - 1.0)) * (1.0 / tile))
    first_tile = _dot(jnp.broadcast_to(cls_tiles, (SUBLANE, LANE)).astype(BF16), upp_ref[...])[0:1, :]
    base = first_tile * tile
    ends = base + cls_tiles * tile
    offs_scr[...] = base + _dot(low_ref[...], cnt.astype(BF16))

    def per_row_tile(j, carry):
        meta = route_ref[j]
        sub = lax.broadcasted_iota(jnp.int32, (LANE, ROW_TILE), 0).astype(F32)
        onehot = jnp.where(sub == meta[2:3, :], 1.0, 0.0).astype(BF16)
        offs = offs_scr[pl.ds(j, 1), :]
        hi = jnp.floor(offs * (1.0 / BYTE))
        rows8 = lambda v: jnp.broadcast_to(v, (SUBLANE, LANE)).astype(BF16)
        pos = BYTE * _dot(rows8(hi), onehot) + _dot(rows8(offs - hi * BYTE), onehot)
        pos_ref[j] = (pos[0:1, :] + meta[3:4, :]).astype(jnp.int32)
        return carry

    lax.fori_loop(0, n_row_tiles, per_row_tile, 0)

    lane = lax.broadcasted_iota(jnp.int32, (PLAN_ROWS, LANE), 1)
    tile_idx = lax.broadcasted_iota(jnp.int32, (PLAN_ROWS, LANE), 0).astype(F32)[:, 0:1]
    n_used = jnp.sum(cls_tiles, axis=1, keepdims=True)
    start = jnp.minimum(tile_idx, n_used - 1.0) * tile
    tcls = jnp.sum(jnp.where((lane < N_CLASSES) & (ends <= start), 1.0, 0.0), axis=1, keepdims=True)
    tcls = jnp.minimum(tcls, N_CLASSES - 1.0)
    group = sum(jnp.where(tcls >= k * N_PAIRS, 1.0, 0.0) for k in range(1, N_EGROUPS))
    pair = tcls - group * N_PAIRS
    p_lo = jnp.where(pair >= 3, 1.0, 0.0) + jnp.where(pair >= 5, 1.0, 0.0)
    p_hi = jnp.where(pair == 0, 1.0, jnp.where((pair == 1) | (pair == 3), 2.0, 3.0))
    table = jnp.where(lane == TAB_LO, group * EXP_PER_GROUP + p_lo,
                      jnp.where(lane == TAB_HI, group * EXP_PER_GROUP + p_hi,
                                jnp.where(lane == TAB_USED, n_used, 0.0)))
    tab_ref[...] = _row_of(table).astype(jnp.int32)


def _plan(route, counts, tile):
    nrt = route.shape[0]
    t = nrt * ROW_TILE
    assert nrt <= PLAN_ROWS and t // tile + N_CLASSES <= PLAN_ROWS
    cnt = jnp.pad(counts[:, 0, :], ((0, PLAN_ROWS - nrt), (0, 0)))
    idx = jnp.arange(PLAN_ROWS)
    low = (idx[:, None] > idx[None, :]).astype(BF16)
    upp = (idx[:, None] < idx[None, :]).astype(BF16)
    pos, tab = pl.pallas_call(
        functools.partial(_plan_body, tile, nrt),
        out_shape=[jax.ShapeDtypeStruct((nrt, 1, ROW_TILE), jnp.int32),
                   jax.ShapeDtypeStruct((TAB_ROWS, PLAN_ROWS), jnp.int32)],
        scratch_shapes=[pltpu.VMEM((PLAN_ROWS, LANE), F32)],
        compiler_params=pltpu.CompilerParams(vmem_limit_bytes=_vmem_limit(_nbytes(route.shape, F32))),
        name="moe_plan",
    )(route, cnt, low, upp)
    return pos.reshape(t), tab


def _moe_body(tile, n_tok, pos_ref, tab_ref, hw_hbm, spare_hbm, wgl_ref, wul_ref, wdl_ref, wgh_ref, wuh_ref,
              wdh_ref, y_hbm, buf, obuf, wgu_scr, wd_scr, tok_smem, gsem, ssem, isem):
    i = pl.program_id(0)
    n_used = tab_ref[TAB_USED, 0]
    ring = buf.shape[0]
    cur = lax.rem(i, ring)
    nxt1 = lax.rem(i + 1, ring)
    nxt2 = lax.rem(i + 2, ring)
    n_points = 16
    per_point = tile // n_points
    tr = TOKEN_TILE_ROWS

    def token_tile(ref, token):
        start = token * tr if isinstance(token, int) else pl.multiple_of(token * tr, tr)
        return ref.at[pl.ds(start, tr)]

    def dma_thread(r):
        return r % 2 if isinstance(r, int) else 0

    def gather_start(block, r, sl):
        src = jnp.minimum(tok_smem[block * tile + r], n_tok - 1)
        pltpu.make_async_copy(token_tile(hw_hbm, src), token_tile(buf.at[sl], r), gsem.at[sl]).start(dma_thread(r))

    def scatter_start(block, r, sl):
        dst = tok_smem[block * tile + r]
        pltpu.make_async_copy(token_tile(obuf.at[sl], r), token_tile(y_hbm, dst), ssem.at[sl]).start(dma_thread(r))

    def gather_wait(sl):
        pltpu.make_async_copy(hw_hbm.at[pl.ds(0, tile * tr)], buf.at[sl], gsem.at[sl]).wait()

    def scatter_wait(sl):
        pltpu.make_async_copy(obuf.at[sl], y_hbm.at[pl.ds(0, tile * tr)], ssem.at[sl]).wait()

    def for_rows(n, fn):
        lax.fori_loop(0, n, lambda r, carry: (fn(r), carry)[1], 0, unroll=8)

    def block_of(t):
        return jnp.minimum(t, n_used - 1) + 1

    @pl.when(i < n_used)
    def _():
        @pl.when(i == 0)
        def _():
            fill = pltpu.make_async_copy(spare_hbm, tok_smem, isem)
            fill.start()
            fill.wait()

            def place(t):
                tok_smem[tile + pos_ref[t]] = t
            for_rows(n_tok, place)
            obuf[ring - 1] = jnp.zeros(obuf.shape[1:], F32)
            for_rows(tile, lambda r: gather_start(block_of(0), r, 0))
            for_rows(tile, lambda r: gather_start(block_of(1), r, 1))

        before = jnp.maximum(i - 1, 0)
        for row, srcs in ((TAB_LO, (wgl_ref, wul_ref, wdl_ref)), (TAB_HI, (wgh_ref, wuh_ref, wdh_ref))):
            slot = 0 if row == TAB_LO else 1

            @pl.when((i == 0) | (tab_ref[row, i] != tab_ref[row, before]))
            def _():
                wgu_scr[2 * slot] = srcs[0][...].astype(BF16)
                wgu_scr[2 * slot + 1] = srcs[1][...].astype(BF16)
                wd_scr[slot] = srcs[2][...].astype(BF16)

        gather_wait(cur)
        issued = [0]

        def issue_point():
            for r in range(issued[0] * per_point, (issued[0] + 1) * per_point):
                gather_start(block_of(i + 2), r, nxt2)
                scatter_start(i, r, nxt2)
            issued[0] += 1

        def tile_row(k):
            return buf[cur, pl.ds(k, tile, stride=tr), :]

        words = [tile_row(k) for k in range(PACKED_ROWS)]
        low = [pltpu.bitcast(lax.shift_left(w, jnp.uint32(16)), F32).astype(BF16) for w in words]
        high = [pltpu.bitcast(lax.bitwise_and(w, jnp.uint32(HIGH_HALF)), F32).astype(BF16) for w in words]
        xb = jnp.concatenate(low + high, axis=1)
        meta = pltpu.bitcast(tile_row(PACKED_ROWS), F32)
        w_lo, w_hi = meta[:, 0:1], meta[:, 1:2]
        half = D_FF // 2
        acts = []
        for slot in range(2):
            parts = []
            for n in range(2):
                gate = _dot(xb, wgu_scr[2 * slot, :, n * half:(n + 1) * half])
                issue_point()
                up = _dot(xb, wgu_scr[2 * slot + 1, :, n * half:(n + 1) * half])
                issue_point()
                parts.append((jax.nn.silu(gate) * up).astype(BF16))
            acts.append(jnp.concatenate(parts, axis=1))
        quarter = D_MODEL // 4
        for n in range(4):
            cols = slice(n * quarter, (n + 1) * quarter)
            d_lo = _dot(acts[0], wd_scr[0, :, cols])
            issue_point()
            d_hi = _dot(acts[1], wd_scr[1, :, cols])
            issue_point()
            out = d_lo * w_lo + d_hi * w_hi
            for kk in range(quarter // LANE):
                obuf[cur, pl.ds(n * (quarter // LANE) + kk, tile, stride=tr), :] = out[:, kk * LANE:(kk + 1) * LANE]
        assert issued[0] == n_points

        @pl.when(i >= 1)
        def _():
            scatter_wait(nxt1)

        @pl.when(i == n_used - 1)
        def _():
            scatter_wait(nxt2)
            for_rows(tile, lambda r: scatter_start(i + 1, r, cur))
            scatter_wait(cur)
            gather_wait(nxt1)
            gather_wait(nxt2)


def _moe(hw, pos, tab, big, layer, tile):
    t = hw.shape[0] // TOKEN_TILE_ROWS
    n_tiles = t // tile + N_CLASSES
    wspec = lambda shape, row: pl.BlockSpec((None, None) + shape, lambda i, pos, tab: (layer, tab[row, i], 0, 0))
    in_specs = [pl.BlockSpec(memory_space=pl.ANY)] * 2
    spare = t + jnp.arange((n_tiles + 1) * tile, dtype=jnp.int32) % tile
    for row in (TAB_LO, TAB_HI):
        in_specs += [wspec((D_MODEL, D_FF), row), wspec((D_MODEL, D_FF), row), wspec((D_FF, D_MODEL), row)]
    block_bytes = 6 * _nbytes((D_MODEL, D_FF), F32) + 3 * _nbytes((D_MODEL, D_FF), BF16) + 3 * _nbytes((tile, D_MODEL), F32)
    tile_rows = tile * TOKEN_TILE_ROWS
    grid_spec = pltpu.PrefetchScalarGridSpec(
        num_scalar_prefetch=2, grid=(n_tiles,), in_specs=in_specs,
        out_specs=pl.BlockSpec(memory_space=pl.ANY),
        scratch_shapes=[pltpu.VMEM((MOE_RING, tile_rows, LANE), jnp.uint32),
                        pltpu.VMEM((MOE_RING, tile_rows, LANE), F32),
                        pltpu.VMEM((4, D_MODEL, D_FF), BF16), pltpu.VMEM((2, D_FF, D_MODEL), BF16),
                        pltpu.SMEM(((n_tiles + 1) * tile,), jnp.int32),
                        pltpu.SemaphoreType.DMA((MOE_RING,)), pltpu.SemaphoreType.DMA((MOE_RING,)),
                        pltpu.SemaphoreType.DMA(())])
    return pl.pallas_call(
        functools.partial(_moe_body, tile, t), grid_spec=grid_spec,
        out_shape=jax.ShapeDtypeStruct(((t + tile) * TOKEN_TILE_ROWS, LANE), F32),
        compiler_params=_params(("arbitrary",), block_bytes), name="moe",
    )(pos, tab, hw, spare, big["w_gate"], big["w_up"], big["w_down"], big["w_gate"], big["w_up"], big["w_down"])


FINAL_TILE = 1024


def _final_body(x_ref, y_ref, g_ref, o_ref):
    o_ref[...] = _rmsnorm(x_ref[...] + _token_rows(y_ref, FINAL_TILE), g_ref[...])


def _final_norm(x, y, y_offset, g):
    t = x.shape[0]
    row = pl.BlockSpec((FINAL_TILE, D_MODEL), lambda i: (i, 0))
    return pl.pallas_call(
        _final_body, grid=(t // FINAL_TILE,),
        in_specs=[row, _token_spec(y_offset, FINAL_TILE), _full((1, D_MODEL))],
        out_specs=row, out_shape=jax.ShapeDtypeStruct((t, D_MODEL), F32),
        compiler_params=_params(("arbitrary",), 3 * _nbytes((FINAL_TILE, D_MODEL), F32)), name="final_norm",
    )(x, y, g)


def _layer_params(i, ln1_g, ret_gn_g, conv_w, conv_b, dt_bias, a_log, d_skip, ssd_norm_g, ln2_g, w_rg, w_re):
    pad_heads = lambda v: jnp.pad(v.astype(F32), (0, LANE - H_SSD)).reshape(1, LANE)
    router = jnp.pad(jnp.concatenate([w_rg[i], w_re[i]], axis=1),
                     ((0, 0), (0, META_W - N_EGROUPS - N_EXPERTS)))
    router_hi = router.astype(BF16)
    return {
        "ln1_g": ln1_g[i].reshape(1, D_MODEL),
        "ret_gn_g": ret_gn_g[i].reshape(1, RET_W),
        "conv_w": conv_w[i], "conv_b": conv_b[i].reshape(1, CONV_DIM),
        "dt_bias": pad_heads(dt_bias[i]), "a": pad_heads(-jnp.exp(a_log[i].astype(F32))),
        "d_skip": jnp.repeat(d_skip[i].astype(F32), P_SSD).reshape(1, SSD_W),
        "ssd_norm_g": ssd_norm_g[i].reshape(1, SSD_W),
        "ln2_g": ln2_g[i].reshape(1, D_MODEL),
        "router_hi": router_hi.T, "router_lo": (router - router_hi.astype(F32)).astype(BF16).T,
    }


def _stacked_params(w_in, w_ret_out, w_ssd_out, w_o, w_gate, w_up, w_down):
    offs = [0]
    for s in IN_SIZES:
        offs.append(offs[-1] + s)
    return {
        "w_in": [w_in.astype(BF16),
                 jnp.pad(w_in[:, :, offs[6]:offs[7]], ((0, 0), (0, 0), (0, LANE - H_SSD))).astype(BF16),
                 w_in[:, :, offs[7]:offs[9]].astype(BF16)],
        "w_ret_out": w_ret_out.astype(BF16), "w_ssd_out": w_ssd_out.astype(BF16), "w_o": w_o.astype(BF16),
        "w_gate": w_gate, "w_up": w_up, "w_down": w_down,
    }


class _Group:
    def __init__(self, x, start, states, row_offset):
        self.batch, self.length, _ = x.shape
        self.x = x.reshape(self.batch * self.length, D_MODEL)
        self.start, self.states, self.row_offset = start, states, row_offset
        self.rets, self.ssms, self.convs = [], [], []


def kernel(x_prompt, x_sample, state_ret, state_ssm, state_conv, ln1_g, w_in, ret_gn_g, w_ret_out, conv_w, conv_b, dt_bias, a_log, d_skip, ssd_norm_g, w_ssd_out, w_o, ln2_g, w_router_group, w_router_expert, w_e_gate, w_e_up, w_e_down, lnf_g):
    depth = w_in.shape[0]
    past_len = 16384.0
    layers = [_layer_params(i, ln1_g, ret_gn_g, conv_w, conv_b, dt_bias, a_log, d_skip, ssd_norm_g, ln2_g,
                            w_router_group, w_router_expert) for i in range(depth)]
    big = _stacked_params(w_in, w_ret_out, w_ssd_out, w_o, w_e_gate, w_e_up, w_e_down)
    n_prompt = x_prompt.shape[0] * x_prompt.shape[1]
    groups = [_Group(x_prompt, 0.0, None, 0),
              _Group(x_sample, past_len, (state_ret, state_ssm, state_conv), n_prompt)]
    y = None
    for i, lp in enumerate(layers):
        group_acts = []
        last = i == depth - 1
        for grp in groups:
            outs = _inproj(grp.x, y, grp.row_offset, lp["ln1_g"], big["w_in"], i)
            if y is not None:
                grp.x, outs = outs[0], outs[1:]
            qkvr, z, xbc, dt, gates = outs
            ret_state = None if grp.states is None else grp.states[0]
            ssd_state = None if grp.states is None else (grp.states[2], grp.states[1])
            ret, new_ret = _retention(qkvr, grp.start, lp["ret_gn_g"], ret_state, i,
                                      grp.rets if last else (), grp.batch, grp.length)
            one_chunk = grp.length == math.gcd(grp.length, CHUNK)
            ssd_fn = _ssd_decode if (one_chunk and ssd_state is not None and grp.batch == LANE) else _ssd
            ssd, new_conv, new_ssm = ssd_fn(xbc, z, dt, lp, ssd_state, i,
                                            list(zip(grp.convs, grp.ssms)) if last else (), grp.batch, grp.length)
            grp.rets.append(new_ret)
            grp.ssms.append(new_ssm)
            grp.convs.append(new_conv)
            group_acts.append((ret, ssd, gates, grp.x))
        new_x, (hw, route, counts) = _merge(group_acts, lp, big, i)
        for grp, x2 in zip(groups, new_x, strict=True):
            grp.x = x2
        pos, tab = _plan(route, counts, ROW_TILE)
        y = _moe(hw, pos, tab, big, i, ROW_TILE)
    outs = []
    for grp in groups:
        out = _final_norm(grp.x, y, grp.row_offset, lnf_g.reshape(1, D_MODEL))
        outs.append(out.reshape(grp.batch, grp.length, D_MODEL))
    states = []
    for grp in groups:
        states += [grp.rets[-1], grp.ssms[-1], grp.convs[-1]]
    return tuple(outs + states)
```
